```python
import jax, jax.numpy as jnp
from jax import lax
import numpy as np

D_MODEL = 1024
BATCH = 2
SEQ = 8192
DEPTH = 4

N_MIXERS = 3
EPS = 1e-6
CHUNK = 128
A_WIDTH = 2 * D_MODEL
A_GROUPS = 8
A_GROUP_DIM = A_WIDTH // A_GROUPS
HEAD_DIM = 128
B_HEADS = D_MODEL // HEAD_DIM
B_PATTERNS = ((128, 1), (512, 4), (2048, 16))
N_B_GROUPS = len(B_PATTERNS)
B_WIDTH = B_HEADS * HEAD_DIM
B_IN_WIDTH = 3 * N_B_GROUPS * B_WIDTH + B_WIDTH
ROPE_DIM = HEAD_DIM // 4
ROPE_THETA = 500000.0
POOL_SIZES = (2, 4, 8, 16)
N_POOL = len(POOL_SIZES)
C_WIDTH = 2 * D_MODEL
C_GROUP = C_WIDTH // N_POOL
N_A = (DEPTH + 2) // 3
N_B = (DEPTH + 1) // 3
N_C = DEPTH // 3

kernel_name = "hybrid_gmlp_dilated_attn_pool_interleaved"


def rms_norm(x, g):
    xf = x.astype(jnp.float32)
    y = xf * lax.rsqrt(jnp.mean(xf * xf, axis=-1, keepdims=True) + EPS)
    return (y * g.astype(jnp.float32)).astype(x.dtype)


def rotary_tables(seq_len):
    half = ROPE_DIM // 2
    inv_freq = jnp.power(jnp.float32(ROPE_THETA), -jnp.arange(half, dtype=jnp.float32) / half)
    ang = jnp.arange(seq_len, dtype=jnp.float32)[:, None] * inv_freq[None, :]
    return jnp.cos(ang)[None, :, None, :], jnp.sin(ang)[None, :, None, :]


def apply_partial_rotary(x, cos, sin):
    half = ROPE_DIM // 2
    x1 = x[..., :half].astype(jnp.float32)
    x2 = x[..., half:ROPE_DIM].astype(jnp.float32)
    rot = jnp.concatenate([x1 * cos - x2 * sin, x2 * cos + x1 * sin], axis=-1)
    return jnp.concatenate([rot.astype(x.dtype), x[..., ROPE_DIM:]], axis=-1)


def dilated_window_attention(q, k, v, span, dilation):
    bsz, S, H, hd = q.shape
    blk = span
    L = S // dilation
    nb = -(-L // blk)
    Lp = nb * blk

    def to_blocks(t):
        t = t.reshape(bsz, L, dilation, H, hd).transpose(0, 2, 1, 3, 4)
        t = jnp.pad(t, ((0, 0), (0, 0), (0, Lp - L), (0, 0), (0, 0)))
        return t.reshape(bsz, dilation, nb, blk, H, hd)

    def with_prev(t):
        prev = jnp.pad(t, ((0, 0), (0, 0), (1, 0), (0, 0), (0, 0), (0, 0)))[:, :, :-1]
        return jnp.concatenate([prev, t], axis=3)

    qb = to_blocks(q).astype(jnp.float32)
    kk = with_prev(to_blocks(k)).astype(jnp.float32)
    vv = with_prev(to_blocks(v)).astype(jnp.float32)
    scores = jnp.einsum('brnqhd,brnkhd->brnhqk', qb, kk) * (1.0 / np.sqrt(hd)).astype(np.float32)
    qi = jnp.arange(blk)[:, None]
    ki = jnp.arange(2 * blk)[None, :]
    dist = blk + qi - ki
    band = (dist >= 0) & (dist <= span)
    has_prev = (jnp.arange(nb) > 0)[:, None, None] | (ki >= blk)[None]
    mask = band[None] & has_prev
    scores = jnp.where(mask[None, None, :, None], scores, -jnp.inf)
    lse = jax.nn.logsumexp(scores, axis=-1)
    p = jnp.exp(scores - lse[..., None])
    o = jnp.einsum('brnhqk,brnkhd->brnqhd', p, vv)
    o = o.reshape(bsz, dilation, Lp, H, hd)[:, :, :L].transpose(0, 2, 1, 3, 4).reshape(bsz, S, H, hd)
    lse = lse.transpose(0, 1, 2, 4, 3).reshape(bsz, dilation, Lp, H)[:, :, :L]
    lse = lse.transpose(0, 2, 1, 3).reshape(bsz, S, H)
    return o, lse


def mixer_a(h, w_in, v_gain, w_s, b_s, w_out):
    bsz, S, _ = h.shape
    proj = h @ w_in
    u = proj[..., :A_WIDTH]
    v = rms_norm(proj[..., A_WIDTH:2 * A_WIDTH], v_gain)
    z = proj[..., 2 * A_WIDTH:]
    nc = S // CHUNK
    v = v.reshape(bsz, nc, CHUNK, A_GROUPS, A_GROUP_DIM)
    causal = jnp.tril(jnp.ones((CHUNK, CHUNK), dtype=bool))
    ws = jnp.where(causal[None], w_s, jnp.zeros_like(w_s))
    mixed = jnp.einsum('gij,bcjgd->bcigd', ws, v) + b_s.T[None, None, :, :, None]
    mixed = mixed.reshape(bsz, S, A_WIDTH)
    y = u * mixed * jax.nn.silu(z)
    return y @ w_out


def mixer_b(h, w_in, q_gain, k_gain, w_out):
    bsz, S, _ = h.shape
    proj = h @ w_in
    n_qkv = 3 * N_B_GROUPS * B_WIDTH
    qkv = proj[..., :n_qkv].reshape(bsz, S, 3, N_B_GROUPS, B_HEADS, HEAD_DIM)
    z = proj[..., n_qkv:]
    cos, sin = rotary_tables(S)
    outs, lses = [], []
    for g, (window, dilation) in enumerate(B_PATTERNS):
        q = apply_partial_rotary(rms_norm(qkv[:, :, 0, g], q_gain[g]), cos, sin)
        k = apply_partial_rotary(rms_norm(qkv[:, :, 1, g], k_gain[g]), cos, sin)
        o, lse = dilated_window_attention(q, k, qkv[:, :, 2, g], window // dilation, dilation)
        outs.append(o)
        lses.append(lse)
    wgt = jax.nn.softmax(jnp.stack(lses), axis=0)
    o = jnp.einsum('gbsh,gbshd->bshd', wgt, jnp.stack(outs))
    y = o.reshape(bsz, S, B_WIDTH).astype(h.dtype) * jax.nn.silu(z)
    return y @ w_out


def causal_mean(x, window):
    S = x.shape[1]
    c = jnp.cumsum(x.astype(jnp.float32), axis=1)
    c_prev = jnp.pad(c, ((0, 0), (window, 0), (0, 0)))[:, :S]
    cnt = jnp.minimum(jnp.arange(S) + 1, window).astype(jnp.float32)
    return ((c - c_prev) / cnt[None, :, None]).astype(x.dtype)


def mixer_c(h, w_in, w_grp, scale, w_out):
    bsz, S, _ = h.shape
    proj = h @ w_in
    xc = proj[..., :C_WIDTH].reshape(bsz, S, N_POOL, C_GROUP)
    z = proj[..., C_WIDTH:]
    pooled = jnp.stack([causal_mean(xc[:, :, g], w) for g, w in enumerate(POOL_SIZES)], axis=2)
    mixed = jnp.einsum('bsgc,gcd->bsgd', pooled - xc, w_grp).reshape(bsz, S, C_WIDTH) * scale
    y = mixed * jax.nn.silu(z)
    return y @ w_out


def setup_inputs(seed: int = 0) -> dict:
    key = jax.random.key(seed)
    ks = jax.random.split(key, 16)
    f32 = jnp.float32

    def nrm(k, shape, fan_in):
        return jax.random.normal(k, shape, f32) * (fan_in ** -0.5)

    def gain(k, shape):
        return 1.0 + 0.1 * jax.random.normal(k, shape, f32)

    return {
        "x": jax.random.normal(ks[0], (BATCH, SEQ, D_MODEL), f32),
        "norm_gain": gain(ks[1], (DEPTH, D_MODEL)),
        "a_w_in": nrm(ks[2], (N_A, D_MODEL, 3 * A_WIDTH), D_MODEL),
        "a_v_gain": gain(ks[3], (N_A, A_WIDTH)),
        "a_w_s": nrm(ks[4], (N_A, A_GROUPS, CHUNK, CHUNK), CHUNK),
        "a_b_s": gain(ks[5], (N_A, A_GROUPS, CHUNK)),
        "a_w_out": nrm(ks[6], (N_A, A_WIDTH, D_MODEL), A_WIDTH),
        "b_w_in": nrm(ks[7], (N_B, D_MODEL, B_IN_WIDTH), D_MODEL),
        "b_q_gain": gain(ks[8], (N_B, N_B_GROUPS, HEAD_DIM)),
        "b_k_gain": gain(ks[9], (N_B, N_B_GROUPS, HEAD_DIM)),
        "b_w_out": nrm(ks[10], (N_B, B_WIDTH, D_MODEL), B_WIDTH),
        "c_w_in": nrm(ks[11], (N_C, D_MODEL, 2 * C_WIDTH), D_MODEL),
        "c_w_grp": nrm(ks[12], (N_C, N_POOL, C_GROUP, C_GROUP), C_GROUP),
        "c_scale": gain(ks[13], (N_C, C_WIDTH)),
        "c_w_out": nrm(ks[14], (N_C, C_WIDTH, D_MODEL), C_WIDTH),
    }


def reference(x, norm_gain, a_w_in, a_v_gain, a_w_s, a_b_s, a_w_out,
              b_w_in, b_q_gain, b_k_gain, b_w_out,
              c_w_in, c_w_grp, c_scale, c_w_out):
    for i in range(DEPTH):
        kind, j = i % N_MIXERS, i // N_MIXERS
        h = rms_norm(x, norm_gain[i])
        if kind == 0:
            y = mixer_a(h, a_w_in[j], a_v_gain[j], a_w_s[j], a_b_s[j], a_w_out[j])
        elif kind == 1:
            y = mixer_b(h, b_w_in[j], b_q_gain[j], b_k_gain[j], b_w_out[j])
        else:
            y = mixer_c(h, c_w_in[j], c_w_grp[j], c_scale[j], c_w_out[j])
        x = x + y.astype(x.dtype)
    return x
```

```python
import functools
import math

import jax
import jax.numpy as jnp
import numpy as np
from jax import lax
from jax.experimental import pallas as pl
from jax.experimental.pallas import tpu as pltpu

D_MODEL = 1024
EPS = 1e-6
CHUNK = 128
A_WIDTH = 2 * D_MODEL
A_GROUPS = 8
A_GROUP_DIM = A_WIDTH // A_GROUPS
HEAD_DIM = 128
B_HEADS = D_MODEL // HEAD_DIM
B_PATTERNS = ((128, 1), (512, 4), (2048, 16))
N_B_GROUPS = len(B_PATTERNS)
B_WIDTH = B_HEADS * HEAD_DIM
ROPE_DIM = HEAD_DIM // 4
ROPE_THETA = 500000.0
SPAN = 128
POOL_SIZES = (2, 4, 8, 16)
N_POOL = len(POOL_SIZES)
C_WIDTH = 2 * D_MODEL
C_GROUP = C_WIDTH // N_POOL
HALO = 16

VMEM_LIMIT_BYTES = 56 * 1024 * 1024

TOKEN_TILE = 512
MASK_VALUE = -1e30

F32 = jnp.float32
BF16 = jnp.bfloat16


def _resident(shape, index_map):
    return pl.BlockSpec(shape, index_map, pipeline_mode=pl.Buffered(1))


def _rms_norm(x, gain):
    return x * lax.rsqrt(jnp.mean(x * x, axis=-1, keepdims=True) + EPS) * gain


def _dot(a, b):
    return jnp.dot(a, b, preferred_element_type=F32)


def _silu(z):
    return z * jax.nn.sigmoid(z)


def _layer_a_kernel(x_ref, ng_ref, w_in_ref, vg_ref, ws_ref, bs_ref, w_out_ref,
                    o_ref, vn_ref, y_ref):
    x = x_ref[...]
    h = _rms_norm(x, ng_ref[...]).astype(BF16)
    v = _dot(h, w_in_ref[:, A_WIDTH:2 * A_WIDTH])
    vn_ref[...] = _rms_norm(v, vg_ref[...]).astype(BF16)
    row = lax.broadcasted_iota(jnp.int32, (CHUNK, CHUNK), 0)
    col = lax.broadcasted_iota(jnp.int32, (CHUNK, CHUNK), 1)
    causal = col <= row
    n_chunks = x.shape[0] // CHUNK
    for g in range(A_GROUPS):
        cols = slice(g * A_GROUP_DIM, (g + 1) * A_GROUP_DIM)
        u = _dot(h, w_in_ref[:, cols])
        z = _dot(h, w_in_ref[:, 2 * A_WIDTH + g * A_GROUP_DIM:2 * A_WIDTH + (g + 1) * A_GROUP_DIM])
        ws = jnp.where(causal, ws_ref[g], 0.0).astype(BF16)
        bias = bs_ref[:, g:g + 1]
        mixed = jnp.concatenate(
            [_dot(ws, vn_ref[c * CHUNK:(c + 1) * CHUNK, cols]) + bias for c in range(n_chunks)],
            axis=0)
        y_ref[:, cols] = (u * mixed * _silu(z)).astype(BF16)
    o_ref[...] = x + _dot(y_ref[...], w_out_ref[...])


def _layer_a(x2, ng, w_in, v_gain, w_s, b_s, w_out):
    tokens = x2.shape[0]
    tm = TOKEN_TILE
    return pl.pallas_call(
        _layer_a_kernel,
        grid=(tokens // tm,),
        in_specs=[
            pl.BlockSpec((tm, D_MODEL), lambda i: (i, 0)),
            _resident((1, D_MODEL), lambda i: (0, 0)),
            _resident((D_MODEL, 3 * A_WIDTH), lambda i: (0, 0)),
            _resident((1, A_WIDTH), lambda i: (0, 0)),
            _resident((A_GROUPS, CHUNK, CHUNK), lambda i: (0, 0, 0)),
            _resident((CHUNK, A_GROUPS), lambda i: (0, 0)),
            _resident((A_WIDTH, D_MODEL), lambda i: (0, 0)),
        ],
        out_specs=pl.BlockSpec((tm, D_MODEL), lambda i: (i, 0)),
        out_shape=jax.ShapeDtypeStruct((tokens, D_MODEL), F32),
        scratch_shapes=[pltpu.VMEM((tm, A_WIDTH), BF16), pltpu.VMEM((tm, A_WIDTH), BF16)],
        compiler_params=pltpu.CompilerParams(
            dimension_semantics=("arbitrary",), vmem_limit_bytes=VMEM_LIMIT_BYTES),
        name="layer_a",
    )(x2, ng.reshape(1, D_MODEL), w_in.astype(BF16), v_gain.reshape(1, A_WIDTH), w_s,
      b_s.T, w_out.astype(BF16))


def _layer_c_kernel(x_ref, ng_ref, w_in_ref, w_grp_ref, scale_ref, w_out_ref,
                    o_ref, ext_ref, y_ref):
    t = pl.program_id(1)
    tm = x_ref.shape[0]
    x = x_ref[...]
    h = _rms_norm(x, ng_ref[...]).astype(BF16)

    @pl.when(t == 0)
    def _():
        ext_ref[0:HALO, :] = jnp.zeros((HALO, C_WIDTH), F32)

    @pl.when(t > 0)
    def _():
        ext_ref[0:HALO, :] = ext_ref[tm:tm + HALO, :]

    ext_ref[HALO:HALO + tm, :] = _dot(h, w_in_ref[:, 0:C_WIDTH])
    pos = t * tm + lax.broadcasted_iota(jnp.int32, (tm, 1), 0)
    for g, window in enumerate(POOL_SIZES):
        cols = slice(g * C_GROUP, (g + 1) * C_GROUP)
        xc = ext_ref[HALO:HALO + tm, cols]
        acc = xc
        for j in range(1, window):
            acc = acc + ext_ref[HALO - j:HALO - j + tm, cols]
        cnt = jnp.minimum(pos + 1, window).astype(F32)
        diff = (acc / cnt - xc).astype(BF16)
        mixed = _dot(diff, w_grp_ref[g]) * scale_ref[:, cols]
        z = _dot(h, w_in_ref[:, C_WIDTH + g * C_GROUP:C_WIDTH + (g + 1) * C_GROUP])
        y_ref[:, cols] = (mixed * _silu(z)).astype(BF16)
    o_ref[...] = x + _dot(y_ref[...], w_out_ref[...])


def _layer_c(x3, ng, w_in, w_grp, scale, w_out):
    bsz, seq, _ = x3.shape
    tm = TOKEN_TILE
    return pl.pallas_call(
        _layer_c_kernel,
        grid=(bsz, seq // tm),
        in_specs=[
            pl.BlockSpec((None, tm, D_MODEL), lambda b, t: (b, t, 0)),
            _resident((1, D_MODEL), lambda b, t: (0, 0)),
            _resident((D_MODEL, 2 * C_WIDTH), lambda b, t: (0, 0)),
            _resident((N_POOL, C_GROUP, C_GROUP), lambda b, t: (0, 0, 0)),
            _resident((1, C_WIDTH), lambda b, t: (0, 0)),
            _resident((C_WIDTH, D_MODEL), lambda b, t: (0, 0)),
        ],
        out_specs=pl.BlockSpec((None, tm, D_MODEL), lambda b, t: (b, t, 0)),
        out_shape=jax.ShapeDtypeStruct((bsz, seq, D_MODEL), F32),
        scratch_shapes=[pltpu.VMEM((tm + HALO, C_WIDTH), F32), pltpu.VMEM((tm, C_WIDTH), BF16)],
        compiler_params=pltpu.CompilerParams(
            dimension_semantics=("arbitrary", "arbitrary"), vmem_limit_bytes=VMEM_LIMIT_BYTES),
        name="layer_c",
    )(x3, ng.reshape(1, D_MODEL), w_in.astype(BF16), w_grp.astype(BF16),
      scale.reshape(1, C_WIDTH), w_out.astype(BF16))


def _rotary_table(seq_len):
    half = ROPE_DIM // 2
    inv_freq = jnp.power(jnp.float32(ROPE_THETA), -jnp.arange(half, dtype=F32) / half)
    ang = jnp.arange(seq_len, dtype=F32)[:, None] * inv_freq[None, :]
    cos, sin = jnp.cos(ang), jnp.sin(ang)
    pad = jnp.zeros((seq_len, HEAD_DIM - ROPE_DIM), F32)
    zero = jnp.zeros((seq_len, half), F32)
    c_tab = jnp.concatenate([cos, cos, pad + 1.0], axis=1)
    s_up = jnp.concatenate([zero, sin, pad], axis=1)
    s_lo = jnp.concatenate([-sin, zero, pad], axis=1)
    return jnp.concatenate([c_tab, s_up, s_lo], axis=1)


def _qk_prepare(t, gain, rot):
    half = ROPE_DIM // 2
    tn = _rms_norm(t, gain)
    c_tab = rot[:, 0:HEAD_DIM]
    s_up = rot[:, HEAD_DIM:2 * HEAD_DIM]
    s_lo = rot[:, 2 * HEAD_DIM:3 * HEAD_DIM]
    return (tn * c_tab + pltpu.roll(tn, half, 1) * s_up
            + pltpu.roll(tn, HEAD_DIM - half, 1) * s_lo)


def _attn_group_kernel(x_ref, ng_ref, rot_ref, wq_ref, wk_ref, wv_ref, qg_ref, kg_ref,
                       o_ref, st_ref, q_scr, k_scr, v_scr):
    lt = pl.program_id(2)
    tl = x_ref.shape[0]
    n_blocks = tl // SPAN
    h = _rms_norm(x_ref[...], ng_ref[...]).astype(BF16)
    rot = rot_ref[...]

    @pl.when(lt == 0)
    def _():
        k_scr[0:SPAN, :] = jnp.zeros((SPAN, B_WIDTH), BF16)
        v_scr[0:SPAN, :] = jnp.zeros((SPAN, B_WIDTH), BF16)

    @pl.when(lt > 0)
    def _():
        k_scr[0:SPAN, :] = k_scr[tl:tl + SPAN, :]
        v_scr[0:SPAN, :] = v_scr[tl:tl + SPAN, :]

    v_scr[SPAN:SPAN + tl, :] = _dot(h, wv_ref[...]).astype(BF16)
    k = _dot(h, wk_ref[...])
    q = _dot(h, wq_ref[...])
    for hd in range(B_HEADS):
        cols = slice(hd * HEAD_DIM, (hd + 1) * HEAD_DIM)
        k_scr[SPAN:SPAN + tl, cols] = _qk_prepare(k[:, cols], kg_ref[...], rot).astype(BF16)
        q_scr[:, cols] = _qk_prepare(q[:, cols], qg_ref[...], rot).astype(BF16)

    qi = lax.broadcasted_iota(jnp.int32, (SPAN, 2 * SPAN), 0)
    ki = lax.broadcasted_iota(jnp.int32, (SPAN, 2 * SPAN), 1)
    band = (ki >= qi) & (ki <= qi + SPAN)
    first_lo = jnp.where(lt > 0, 0, SPAN)
    band_first = band & (ki >= first_lo)
    lane = lax.broadcasted_iota(jnp.int32, (SPAN, HEAD_DIM), 1)
    scale = np.float32(1.0 / np.sqrt(HEAD_DIM))

    for j in range(n_blocks):
        mask = band_first if j == 0 else band
        rows = slice(j * SPAN, (j + 1) * SPAN)
        keys = slice(j * SPAN, (j + 2) * SPAN)
        stats = jnp.zeros((SPAN, HEAD_DIM), F32)
        for hd in range(B_HEADS):
            cols = slice(hd * HEAD_DIM, (hd + 1) * HEAD_DIM)
            s = lax.dot_general(q_scr[rows, cols], k_scr[keys, cols],
                                (((1,), (1,)), ((), ())), preferred_element_type=F32) * scale
            s = jnp.where(mask, s, MASK_VALUE)
            m = jnp.max(s, axis=-1, keepdims=True)
            p = jnp.exp(s - m)
            denom = jnp.sum(p, axis=-1, keepdims=True)
            o = _dot(p.astype(BF16), v_scr[keys, cols])
            o_ref[rows, cols] = o / denom
            stats = jnp.where(lane == hd, m + jnp.log(denom), stats)
        st_ref[rows, :] = stats


def _attn_group(x3, ng, rot, w_in, q_gain, k_gain, group, dilation):
    bsz, seq, _ = x3.shape
    strided_len = seq // dilation
    tl = min(TOKEN_TILE, strided_len)
    x_view = x3.reshape(bsz, strided_len, dilation * D_MODEL)
    rot_view = rot.reshape(strided_len, dilation * 3 * HEAD_DIM)
    o, st = pl.pallas_call(
        _attn_group_kernel,
        grid=(bsz, dilation, strided_len // tl),
        in_specs=[
            pl.BlockSpec((None, tl, D_MODEL), lambda b, r, l: (b, l, r)),
            _resident((1, D_MODEL), lambda b, r, l: (0, 0)),
            pl.BlockSpec((tl, 3 * HEAD_DIM), lambda b, r, l: (l, r)),
            _resident((D_MODEL, B_WIDTH), lambda b, r, l: (0, group)),
            _resident((D_MODEL, B_WIDTH), lambda b, r, l: (0, N_B_GROUPS + group)),
            _resident((D_MODEL, B_WIDTH), lambda b, r, l: (0, 2 * N_B_GROUPS + group)),
            _resident((1, HEAD_DIM), lambda b, r, l: (0, 0)),
            _resident((1, HEAD_DIM), lambda b, r, l: (0, 0)),
        ],
        out_specs=[
            pl.BlockSpec((None, tl, B_WIDTH), lambda b, r, l: (b, l, r)),
            pl.BlockSpec((None, tl, HEAD_DIM), lambda b, r, l: (b, l, r)),
        ],
        out_shape=[
            jax.ShapeDtypeStruct((bsz, strided_len, dilation * B_WIDTH), F32),
            jax.ShapeDtypeStruct((bsz, strided_len, dilation * HEAD_DIM), F32),
        ],
        scratch_shapes=[
            pltpu.VMEM((tl, B_WIDTH), BF16),
            pltpu.VMEM((tl + SPAN, B_WIDTH), BF16),
            pltpu.VMEM((tl + SPAN, B_WIDTH), BF16),
        ],
        compiler_params=pltpu.CompilerParams(
            dimension_semantics=("arbitrary", "arbitrary", "arbitrary"),
            vmem_limit_bytes=VMEM_LIMIT_BYTES),
        name=f"attn_group_d{dilation}",
    )(x_view, ng.reshape(1, D_MODEL), rot_view, w_in, w_in, w_in,
      q_gain.reshape(1, HEAD_DIM), k_gain.reshape(1, HEAD_DIM))
    return o.reshape(bsz * seq, B_WIDTH), st.reshape(bsz * seq, HEAD_DIM)


def _attn_merge_kernel(x_ref, ng_ref, wz_ref, o0_ref, o1_ref, o2_ref, s0_ref, s1_ref, s2_ref,
                       w_out_ref, out_ref, y_ref):
    x = x_ref[...]
    h = _rms_norm(x, ng_ref[...]).astype(BF16)
    z = _dot(h, wz_ref[...])
    for hd in range(B_HEADS):
        cols = slice(hd * HEAD_DIM, (hd + 1) * HEAD_DIM)
        l0 = s0_ref[:, hd:hd + 1]
        l1 = s1_ref[:, hd:hd + 1]
        l2 = s2_ref[:, hd:hd + 1]
        m = jnp.maximum(jnp.maximum(l0, l1), l2)
        e0, e1, e2 = jnp.exp(l0 - m), jnp.exp(l1 - m), jnp.exp(l2 - m)
        total = e0 + e1 + e2
        o = (e0 / total) * o0_ref[:, cols] + (e1 / total) * o1_ref[:, cols] + (e2 / total) * o2_ref[:, cols]
        y_ref[:, cols] = (o * _silu(z[:, cols])).astype(BF16)
    out_ref[...] = x + _dot(y_ref[...], w_out_ref[...])


def _attn_merge(x2, ng, w_in, outs, stats, w_out):
    tokens = x2.shape[0]
    tm = TOKEN_TILE
    tile = lambda width: pl.BlockSpec((tm, width), lambda i: (i, 0))
    return pl.pallas_call(
        _attn_merge_kernel,
        grid=(tokens // tm,),
        in_specs=[
            tile(D_MODEL),
            _resident((1, D_MODEL), lambda i: (0, 0)),
            _resident((D_MODEL, B_WIDTH), lambda i: (0, 3 * N_B_GROUPS)),
            tile(B_WIDTH), tile(B_WIDTH), tile(B_WIDTH),
            tile(HEAD_DIM), tile(HEAD_DIM), tile(HEAD_DIM),
            _resident((B_WIDTH, D_MODEL), lambda i: (0, 0)),
        ],
        out_specs=tile(D_MODEL),
        out_shape=jax.ShapeDtypeStruct((tokens, D_MODEL), F32),
        scratch_shapes=[pltpu.VMEM((tm, B_WIDTH), BF16)],
        compiler_params=pltpu.CompilerParams(
            dimension_semantics=("arbitrary",), vmem_limit_bytes=VMEM_LIMIT_BYTES),
        name="attn_merge",
    )(x2, ng.reshape(1, D_MODEL), w_in, *outs, *stats, w_out)


def _layer_b(x3, ng, w_in, q_gain, k_gain, w_out):
    bsz, seq, _ = x3.shape
    w_in16 = w_in.astype(BF16)
    rot = _rotary_table(seq)
    outs, stats = [], []
    for g, (window, dilation) in enumerate(B_PATTERNS):
        assert window // dilation == SPAN
        o, st = _attn_group(x3, ng, rot, w_in16, q_gain[g], k_gain[g], g, dilation)
        outs.append(o)
        stats.append(st)
    x2 = x3.reshape(bsz * seq, D_MODEL)
    return _attn_merge(x2, ng, w_in16, outs, stats, w_out.astype(BF16)).reshape(bsz, seq, D_MODEL)


def kernel(x, norm_gain, a_w_in, a_v_gain, a_w_s, a_b_s, a_w_out, b_w_in, b_q_gain, b_k_gain,
           b_w_out, c_w_in, c_w_grp, c_scale, c_w_out):
    bsz, seq, d_model = x.shape
    assert d_model == D_MODEL and seq % (TOKEN_TILE * max(d for _, d in B_PATTERNS)) == 0
    depth = norm_gain.shape[0]
    for i in range(depth):
        kind, j = i % 3, i // 3
        if kind == 0:
            x = _layer_a(x.reshape(bsz * seq, D_MODEL), norm_gain[i], a_w_in[j], a_v_gain[j],
                         a_w_s[j], a_b_s[j], a_w_out[j]).reshape(bsz, seq, D_MODEL)
        elif kind == 1:
            x = _layer_b(x, norm_gain[i], b_w_in[j], b_q_gain[j], b_k_gain[j], b_w_out[j])
        else:
            x = _layer_c(x, norm_gain[i], c_w_in[j], c_w_grp[j], c_scale[j], c_w_out[j])
    return x
```

```python
import functools

import jax
import jax.numpy as jnp
import numpy as np
from jax import lax
from jax.experimental import pallas as pl
from jax.experimental.pallas import tpu as pltpu

D_MODEL = 1024
EPS = 1e-6
CHUNK = 128
A_WIDTH = 2 * D_MODEL
A_GROUPS = 8
A_GROUP_DIM = A_WIDTH // A_GROUPS
HEAD_DIM = 128
B_HEADS = D_MODEL // HEAD_DIM
B_PATTERNS = ((128, 1), (512, 4), (2048, 16))
N_B_GROUPS = len(B_PATTERNS)
B_WIDTH = B_HEADS * HEAD_DIM
ROPE_DIM = HEAD_DIM // 4
ROPE_HALF = ROPE_DIM // 2
ROPE_THETA = 500000.0
SPAN = 128
SUPER = SPAN * max(d for _, d in B_PATTERNS)
HEADS_PER_STEP = 2
POOL_SIZES = (2, 4, 8, 16)
N_POOL = len(POOL_SIZES)
C_WIDTH = 2 * D_MODEL
C_GROUP = C_WIDTH // N_POOL
HALO = 16

LANES = 128
VMEM_LIMIT_BYTES = 56 * 1024 * 1024

TOKEN_TILE = 512
MASK_VALUE = -1e30

F32 = jnp.float32
BF16 = jnp.bfloat16


def _resident(shape, index_map):
    return pl.BlockSpec(shape, index_map, pipeline_mode=pl.Buffered(1))


def _rms_norm(x, gain):
    return x * lax.rsqrt(jnp.mean(x * x, axis=-1, keepdims=True) + EPS) * gain


def _dot(a, b):
    return jnp.dot(a, b, preferred_element_type=F32)


def _dot_nt(a, b):
    return lax.dot_general(a, b, (((1,), (1,)), ((), ())), preferred_element_type=F32)


def _silu(z):
    return z * jax.nn.sigmoid(z)


def _layer_a_kernel(x_ref, ng_ref, w_in_ref, vg_ref, ws_ref, bs_ref, w_out_ref,
                    o_ref, vn_ref, y_ref):
    x = x_ref[...]
    h = _rms_norm(x, ng_ref[...]).astype(BF16)
    v = _dot(h, w_in_ref[:, A_WIDTH:2 * A_WIDTH])
    vn_ref[...] = _rms_norm(v, vg_ref[...]).astype(BF16)
    row = lax.broadcasted_iota(jnp.int32, (CHUNK, CHUNK), 0)
    col = lax.broadcasted_iota(jnp.int32, (CHUNK, CHUNK), 1)
    causal = col <= row
    n_chunks = x.shape[0] // CHUNK
    for g in range(A_GROUPS):
        cols = slice(g * A_GROUP_DIM, (g + 1) * A_GROUP_DIM)
        u = _dot(h, w_in_ref[:, cols])
        z = _dot(h, w_in_ref[:, 2 * A_WIDTH + g * A_GROUP_DIM:2 * A_WIDTH + (g + 1) * A_GROUP_DIM])
        ws = jnp.where(causal, ws_ref[g], 0.0).astype(BF16)
        bias = bs_ref[:, g:g + 1]
        mixed = jnp.concatenate(
            [_dot(ws, vn_ref[c * CHUNK:(c + 1) * CHUNK, cols]) + bias for c in range(n_chunks)],
            axis=0)
        y_ref[:, cols] = (u * mixed * _silu(z)).astype(BF16)
    o_ref[...] = x + _dot(y_ref[...], w_out_ref[...])


def _layer_a(x2, ng, w_in, v_gain, w_s, b_s, w_out):
    tokens = x2.shape[0]
    tm = TOKEN_TILE
    return pl.pallas_call(
        _layer_a_kernel,
        grid=(tokens // tm,),
        in_specs=[
            pl.BlockSpec((tm, D_MODEL), lambda i: (i, 0)),
            _resident((1, D_MODEL), lambda i: (0, 0)),
            _resident((D_MODEL, 3 * A_WIDTH), lambda i: (0, 0)),
            _resident((1, A_WIDTH), lambda i: (0, 0)),
            _resident((A_GROUPS, CHUNK, CHUNK), lambda i: (0, 0, 0)),
            _resident((CHUNK, A_GROUPS), lambda i: (0, 0)),
            _resident((A_WIDTH, D_MODEL), lambda i: (0, 0)),
        ],
        out_specs=pl.BlockSpec((tm, D_MODEL), lambda i: (i, 0)),
        out_shape=jax.ShapeDtypeStruct((tokens, D_MODEL), F32),
        scratch_shapes=[pltpu.VMEM((tm, A_WIDTH), BF16), pltpu.VMEM((tm, A_WIDTH), BF16)],
        compiler_params=pltpu.CompilerParams(
            dimension_semantics=("arbitrary",), vmem_limit_bytes=VMEM_LIMIT_BYTES),
        name="layer_a",
    )(x2, ng.reshape(1, D_MODEL), w_in.astype(BF16), v_gain.reshape(1, A_WIDTH), w_s,
      b_s.T, w_out.astype(BF16))


def _layer_c_kernel(x_ref, ng_ref, w_in_ref, w_grp_ref, scale_ref, w_out_ref,
                    o_ref, ext_ref, y_ref):
    t = pl.program_id(1)
    tm = x_ref.shape[0]
    x = x_ref[...]
    h = _rms_norm(x, ng_ref[...]).astype(BF16)

    @pl.when(t == 0)
    def _():
        ext_ref[0:HALO, :] = jnp.zeros((HALO, C_WIDTH), F32)

    @pl.when(t > 0)
    def _():
        ext_ref[0:HALO, :] = ext_ref[tm:tm + HALO, :]

    ext_ref[HALO:HALO + tm, :] = _dot(h, w_in_ref[:, 0:C_WIDTH])
    pos = t * tm + lax.broadcasted_iota(jnp.int32, (tm, 1), 0)
    for g, window in enumerate(POOL_SIZES):
        cols = slice(g * C_GROUP, (g + 1) * C_GROUP)
        xc = ext_ref[HALO:HALO + tm, cols]
        acc = xc
        for j in range(1, window):
            acc = acc + ext_ref[HALO - j:HALO - j + tm, cols]
        cnt = jnp.minimum(pos + 1, window).astype(F32)
        diff = (acc / cnt - xc).astype(BF16)
        mixed = _dot(diff, w_grp_ref[g]) * scale_ref[:, cols]
        z = _dot(h, w_in_ref[:, C_WIDTH + g * C_GROUP:C_WIDTH + (g + 1) * C_GROUP])
        y_ref[:, cols] = (mixed * _silu(z)).astype(BF16)
    o_ref[...] = x + _dot(y_ref[...], w_out_ref[...])


def _layer_c(x3, ng, w_in, w_grp, scale, w_out):
    bsz, seq, _ = x3.shape
    tm = TOKEN_TILE
    return pl.pallas_call(
        _layer_c_kernel,
        grid=(bsz, seq // tm),
        in_specs=[
            pl.BlockSpec((None, tm, D_MODEL), lambda b, t: (b, t, 0)),
            _resident((1, D_MODEL), lambda b, t: (0, 0)),
            _resident((D_MODEL, 2 * C_WIDTH), lambda b, t: (0, 0)),
            _resident((N_POOL, C_GROUP, C_GROUP), lambda b, t: (0, 0, 0)),
            _resident((1, C_WIDTH), lambda b, t: (0, 0)),
            _resident((C_WIDTH, D_MODEL), lambda b, t: (0, 0)),
        ],
        out_specs=pl.BlockSpec((None, tm, D_MODEL), lambda b, t: (b, t, 0)),
        out_shape=jax.ShapeDtypeStruct((bsz, seq, D_MODEL), F32),
        scratch_shapes=[pltpu.VMEM((tm + HALO, C_WIDTH), F32), pltpu.VMEM((tm, C_WIDTH), BF16)],
        compiler_params=pltpu.CompilerParams(
            dimension_semantics=("arbitrary", "arbitrary"), vmem_limit_bytes=VMEM_LIMIT_BYTES),
        name="layer_c",
    )(x3, ng.reshape(1, D_MODEL), w_in.astype(BF16), w_grp.astype(BF16),
      scale.reshape(1, C_WIDTH), w_out.astype(BF16))


def _rotary_tables(seq_len, dilation):
    per = SUPER // dilation
    p = jnp.arange(SUPER, dtype=jnp.int32)
    local = (p % per) * dilation + p // per
    pos = (jnp.arange(seq_len // SUPER, dtype=jnp.int32)[:, None] * SUPER + local[None, :]).reshape(-1)
    inv_freq = jnp.power(jnp.float32(ROPE_THETA), -jnp.arange(ROPE_HALF, dtype=F32) / ROPE_HALF)
    ang = pos.astype(F32)[:, None] * inv_freq[None, :]
    cos, sin = jnp.cos(ang), jnp.sin(ang)
    rest = jnp.zeros((seq_len, HEAD_DIM - ROPE_DIM), F32)
    row_major = jnp.concatenate([cos, cos, rest + 1.0, -sin, sin, rest], axis=1)
    transposed = jnp.concatenate([cos.T, sin.T], axis=0)
    return row_major, transposed


def _attn_proj_kernel(x_ref, ng_ref, rot_ref, rot_t_ref, wq_ref, wkt_ref, wv_ref, qg_ref, kg_ref,
                      q_ref, kt_ref, v_ref, h_scr, *, dilation):
    c = pl.program_id(2)
    rows = q_ref.shape[0]
    per = SUPER // dilation
    n_cols = D_MODEL // LANES

    @pl.when(c == 0)
    def _():
        for i in range(SUPER // rows):
            rs = slice(i * rows, (i + 1) * rows)
            hh = _rms_norm(x_ref[rs, :], ng_ref[...])
            for j in range(n_cols):
                h_scr[j, rs, :] = hh[:, j * LANES:(j + 1) * LANES]

    def gather(j):
        if dilation == 1:
            return h_scr[j, pl.ds(pl.multiple_of(c * rows, rows), rows), :]
        run = min(per, rows)
        first = c * (rows // run)
        if per > rows:
            raise NotImplementedError("a residue class longer than the chunk needs a row offset")
        pieces = [h_scr[j, pl.ds(first + rr, run, stride=dilation), :] for rr in range(rows // run)]
        return pieces[0] if len(pieces) == 1 else jnp.concatenate(pieces, axis=0)

    h = jnp.concatenate([gather(j) for j in range(n_cols)], axis=1).astype(BF16)

    v_ref[...] = _dot(h, wv_ref[...]).astype(BF16)

    q = _dot(h, wq_ref[...])
    cos_tab = rot_ref[:, 0:HEAD_DIM]
    sin_tab = rot_ref[:, HEAD_DIM:2 * HEAD_DIM]
    lane = lax.broadcasted_iota(jnp.int32, (rows, HEAD_DIM), 1)
    for hd in range(B_HEADS):
        cols = slice(hd * HEAD_DIM, (hd + 1) * HEAD_DIM)
        qn = _rms_norm(q[:, cols], qg_ref[...])
        partner = jnp.where(lane < ROPE_HALF, pltpu.roll(qn, HEAD_DIM - ROPE_HALF, 1),
                            pltpu.roll(qn, ROPE_HALF, 1))
        q_ref[:, cols] = (qn * cos_tab + partner * sin_tab).astype(BF16)

    kt = _dot_nt(wkt_ref[...], h)
    cos_t = rot_t_ref[0:ROPE_HALF, :]
    sin_t = rot_t_ref[ROPE_HALF:ROPE_DIM, :]
    for hd in range(B_HEADS):
        slab = kt[hd * HEAD_DIM:(hd + 1) * HEAD_DIM, :]
        kn = slab * lax.rsqrt(jnp.mean(slab * slab, axis=0, keepdims=True) + EPS) * kg_ref[...]
        x1, x2 = kn[0:ROPE_HALF, :], kn[ROPE_HALF:ROPE_DIM, :]
        rotated = jnp.concatenate(
            [x1 * cos_t - x2 * sin_t, x2 * cos_t + x1 * sin_t, kn[ROPE_DIM:, :]], axis=0)
        kt_ref[hd * HEAD_DIM:(hd + 1) * HEAD_DIM, :] = rotated.astype(BF16)


def _attn_proj(x3, ng, w_in16, w_kt16, q_gain, k_gain, group, dilation):
    bsz, seq, _ = x3.shape
    rows = TOKEN_TILE
    n_super, n_chunk = seq // SUPER, SUPER // rows
    rot, rot_t = _rotary_tables(seq, dilation)
    return pl.pallas_call(
        functools.partial(_attn_proj_kernel, dilation=dilation),
        grid=(bsz, n_super, n_chunk),
        in_specs=[
            pl.BlockSpec((None, SUPER, D_MODEL), lambda b, n, c: (b, n, 0)),
            _resident((1, D_MODEL), lambda b, n, c: (0, 0)),
            pl.BlockSpec((rows, 2 * HEAD_DIM), lambda b, n, c: (n * n_chunk + c, 0)),
            pl.BlockSpec((ROPE_DIM, rows), lambda b, n, c: (0, n * n_chunk + c)),
            _resident((D_MODEL, B_WIDTH), lambda b, n, c: (0, group)),
            _resident((B_WIDTH, D_MODEL), lambda b, n, c: (group, 0)),
            _resident((D_MODEL, B_WIDTH), lambda b, n, c: (0, 2 * N_B_GROUPS + group)),
            _resident((1, HEAD_DIM), lambda b, n, c: (0, 0)),
            _resident((HEAD_DIM, 1), lambda b, n, c: (0, 0)),
        ],
        out_specs=[
            pl.BlockSpec((None, rows, B_WIDTH), lambda b, n, c: (b, n * n_chunk + c, 0)),
            pl.BlockSpec((None, B_WIDTH, rows), lambda b, n, c: (b, 0, n * n_chunk + c)),
            pl.BlockSpec((None, rows, B_WIDTH), lambda b, n, c: (b, n * n_chunk + c, 0)),
        ],
        out_shape=[
            jax.ShapeDtypeStruct((bsz, seq, B_WIDTH), BF16),
            jax.ShapeDtypeStruct((bsz, B_WIDTH, seq), BF16),
            jax.ShapeDtypeStruct((bsz, seq, B_WIDTH), BF16),
        ],
        scratch_shapes=[pltpu.VMEM((D_MODEL // LANES, SUPER, LANES), F32)],
        compiler_params=pltpu.CompilerParams(
            dimension_semantics=("arbitrary", "arbitrary", "arbitrary"),
            vmem_limit_bytes=VMEM_LIMIT_BYTES),
        name=f"attn_proj_d{dilation}",
    )(x3, ng.reshape(1, D_MODEL), rot, rot_t, w_in16, w_kt16, w_in16,
      q_gain.reshape(1, HEAD_DIM), k_gain.reshape(HEAD_DIM, 1))


def _attn_kernel(*refs):
    qkv_refs = refs[:3 * N_B_GROUPS]
    y_ref = refs[3 * N_B_GROUPS]
    scratch = refs[3 * N_B_GROUPS + 1:]
    kt_scrs, v_scrs = scratch[0:N_B_GROUPS], scratch[N_B_GROUPS:2 * N_B_GROUPS]
    o_scr, lse_scr = scratch[2 * N_B_GROUPS:]
    n = pl.program_id(2)

    qi = lax.broadcasted_iota(jnp.int32, (SPAN, 2 * SPAN), 0)
    ki = lax.broadcasted_iota(jnp.int32, (SPAN, 2 * SPAN), 1)
    band = (ki >= qi) & (ki <= qi + SPAN)
    band_first = band & (ki >= jnp.where(n > 0, 0, SPAN))
    scale = np.float32(1.0 / np.sqrt(HEAD_DIM))

    for g, (_, d) in enumerate(B_PATTERNS):
        q_ref, kt_ref, v_ref = qkv_refs[3 * g:3 * g + 3]
        kt_scr, v_scr = kt_scrs[g], v_scrs[g]
        per = SUPER // d
        seg = SPAN + per

        @pl.when(n == 0)
        def _():
            for r in range(d):
                kt_scr[:, r * seg:r * seg + SPAN] = jnp.zeros((kt_scr.shape[0], SPAN), BF16)
                v_scr[r * seg:r * seg + SPAN, :] = jnp.zeros((SPAN, v_scr.shape[1]), BF16)

        @pl.when(n > 0)
        def _():
            for r in range(d):
                kt_scr[:, r * seg:r * seg + SPAN] = kt_scr[:, r * seg + per:r * seg + per + SPAN]
                v_scr[r * seg:r * seg + SPAN, :] = v_scr[r * seg + per:r * seg + per + SPAN, :]

        for r in range(d):
            kt_scr[:, r * seg + SPAN:(r + 1) * seg] = kt_ref[:, r * per:(r + 1) * per]
            v_scr[r * seg + SPAN:(r + 1) * seg, :] = v_ref[r * per:(r + 1) * per, :]

        for r in range(d):
            for j in range(per // SPAN):
                mask = band_first if j == 0 else band
                q_rows = slice(r * per + j * SPAN, r * per + (j + 1) * SPAN)
                keys = slice(r * seg + j * SPAN, r * seg + (j + 2) * SPAN)
                nat = pl.ds(j * SPAN * d + r, SPAN, stride=d) if d > 1 else slice(j * SPAN, (j + 1) * SPAN)
                for hh in range(HEADS_PER_STEP):
                    cols = slice(hh * HEAD_DIM, (hh + 1) * HEAD_DIM)
                    s = _dot(q_ref[q_rows, cols], kt_scr[cols, keys]) * scale
                    s = jnp.where(mask, s, MASK_VALUE)
                    m = jnp.max(s, axis=-1, keepdims=True)
                    p = jnp.exp(s - m)
                    denom = jnp.sum(p, axis=-1, keepdims=True)
                    o = _dot(p.astype(BF16), v_scr[keys, cols])
                    o_scr[g, hh, nat, :] = o / denom
                    lse_scr[g, hh, nat, :] = jnp.broadcast_to(m + jnp.log(denom), (SPAN, HEAD_DIM))

    rows = 256
    for hh in range(HEADS_PER_STEP):
        for i in range(SUPER // rows):
            rs = slice(i * rows, (i + 1) * rows)
            lses = [lse_scr[g, hh, rs, :] for g in range(N_B_GROUPS)]
            m = jnp.maximum(jnp.maximum(lses[0], lses[1]), lses[2])
            es = [jnp.exp(l - m) for l in lses]
            total = es[0] + es[1] + es[2]
            o = sum((es[g] / total) * o_scr[g, hh, rs, :] for g in range(N_B_GROUPS))
            y_ref[rs, hh * HEAD_DIM:(hh + 1) * HEAD_DIM] = o


def _attn(qkvs, bsz, seq):
    width = HEADS_PER_STEP * HEAD_DIM
    n_super = seq // SUPER
    in_specs, operands, kt_scr, v_scr = [], [], [], []
    for (q, kt, v), (_, d) in zip(qkvs, B_PATTERNS):
        in_specs += [
            pl.BlockSpec((None, SUPER, width), lambda b, hp, n: (b, n, hp)),
            pl.BlockSpec((None, width, SUPER), lambda b, hp, n: (b, hp, n)),
            pl.BlockSpec((None, SUPER, width), lambda b, hp, n: (b, n, hp)),
        ]
        operands += [q, kt, v]
        kt_scr.append(pltpu.VMEM((width, SUPER + d * SPAN), BF16))
        v_scr.append(pltpu.VMEM((SUPER + d * SPAN, width), BF16))
    return pl.pallas_call(
        _attn_kernel,
        grid=(bsz, B_HEADS // HEADS_PER_STEP, n_super),
        in_specs=in_specs,
        out_specs=pl.BlockSpec((None, SUPER, width), lambda b, hp, n: (b, n, hp)),
        out_shape=jax.ShapeDtypeStruct((bsz, seq, B_WIDTH), F32),
        scratch_shapes=kt_scr + v_scr + [
            pltpu.VMEM((N_B_GROUPS, HEADS_PER_STEP, SUPER, HEAD_DIM), F32),
            pltpu.VMEM((N_B_GROUPS, HEADS_PER_STEP, SUPER, HEAD_DIM), F32),
        ],
        compiler_params=pltpu.CompilerParams(
            dimension_semantics=("arbitrary", "arbitrary", "arbitrary"),
            vmem_limit_bytes=VMEM_LIMIT_BYTES),
        name="attn",
    )(*operands)


def _attn_out_kernel(x_ref, ng_ref, wz_ref, o_ref, w_out_ref, out_ref):
    x = x_ref[...]
    h = _rms_norm(x, ng_ref[...]).astype(BF16)
    z = _dot(h, wz_ref[...])
    y = (o_ref[...] * _silu(z)).astype(BF16)
    out_ref[...] = x + _dot(y, w_out_ref[...])


def _attn_out(x2, ng, w_in16, o2, w_out):
    tokens = x2.shape[0]
    tm = TOKEN_TILE
    tile = lambda width: pl.BlockSpec((tm, width), lambda i: (i, 0))
    return pl.pallas_call(
        _attn_out_kernel,
        grid=(tokens // tm,),
        in_specs=[
            tile(D_MODEL),
            _resident((1, D_MODEL), lambda i: (0, 0)),
            _resident((D_MODEL, B_WIDTH), lambda i: (0, 3 * N_B_GROUPS)),
            tile(B_WIDTH),
            _resident((B_WIDTH, D_MODEL), lambda i: (0, 0)),
        ],
        out_specs=tile(D_MODEL),
        out_shape=jax.ShapeDtypeStruct((tokens, D_MODEL), F32),
        compiler_params=pltpu.CompilerParams(
            dimension_semantics=("arbitrary",), vmem_limit_bytes=VMEM_LIMIT_BYTES),
        name="attn_out",
    )(x2, ng.reshape(1, D_MODEL), w_in16, o2, w_out.astype(BF16))


def _layer_b(x3, ng, w_in, q_gain, k_gain, w_out):
    bsz, seq, _ = x3.shape
    w_in16 = w_in.astype(BF16)
    w_kt16 = w_in16[:, N_B_GROUPS * B_WIDTH:2 * N_B_GROUPS * B_WIDTH].T
    qkvs = []
    for g, (window, dilation) in enumerate(B_PATTERNS):
        assert window // dilation == SPAN
        qkvs.append(_attn_proj(x3, ng, w_in16, w_kt16, q_gain[g], k_gain[g], g, dilation))
    o = _attn(qkvs, bsz, seq)
    x2 = x3.reshape(bsz * seq, D_MODEL)
    return _attn_out(x2, ng, w_in16, o.reshape(bsz * seq, B_WIDTH), w_out).reshape(bsz, seq, D_MODEL)


def kernel(x, norm_gain, a_w_in, a_v_gain, a_w_s, a_b_s, a_w_out, b_w_in, b_q_gain, b_k_gain,
           b_w_out, c_w_in, c_w_grp, c_scale, c_w_out):
    bsz, seq, d_model = x.shape
    assert d_model == D_MODEL and seq % SUPER == 0 and SUPER % TOKEN_TILE == 0
    depth = norm_gain.shape[0]
    for i in range(depth):
        kind, j = i % 3, i // 3
        if kind == 0:
            x = _layer_a(x.reshape(bsz * seq, D_MODEL), norm_gain[i], a_w_in[j], a_v_gain[j],
                         a_w_s[j], a_b_s[j], a_w_out[j]).reshape(bsz, seq, D_MODEL)
        elif kind == 1:
            x = _layer_b(x, norm_gain[i], b_w_in[j], b_q_gain[j], b_k_gain[j], b_w_out[j])
        else:
            x = _layer_c(x, norm_gain[i], c_w_in[j], c_w_grp[j], c_scale[j], c_w_out[j])
    return x
```

```python
import functools

import jax
import jax.numpy as jnp
import numpy as np
from jax import lax
from jax.experimental import pallas as pl
from jax.experimental.pallas import tpu as pltpu

D_MODEL = 1024
EPS = 1e-6
CHUNK = 128
A_WIDTH = 2 * D_MODEL
A_GROUPS = 8
A_GROUP_DIM = A_WIDTH // A_GROUPS
HEAD_DIM = 128
B_HEADS = D_MODEL // HEAD_DIM
B_PATTERNS = ((128, 1), (512, 4), (2048, 16))
N_B_GROUPS = len(B_PATTERNS)
B_WIDTH = B_HEADS * HEAD_DIM
ROPE_DIM = HEAD_DIM // 4
ROPE_HALF = ROPE_DIM // 2
ROPE_THETA = 500000.0
SPAN = 128
SUPER = SPAN * max(d for _, d in B_PATTERNS)
HEADS_PER_STEP = 2
POOL_SIZES = (2, 4, 8, 16)
N_POOL = len(POOL_SIZES)
C_WIDTH = 2 * D_MODEL
C_GROUP = C_WIDTH // N_POOL
HALO = 16

LANES = 128
VMEM_LIMIT_BYTES = 56 * 1024 * 1024

TOKEN_TILE = 512
MASK_VALUE = -1e30

F32 = jnp.float32
BF16 = jnp.bfloat16


def _resident(shape, index_map):
    return pl.BlockSpec(shape, index_map, pipeline_mode=pl.Buffered(1))


def _rms_norm(x, gain):
    return x * lax.rsqrt(jnp.mean(x * x, axis=-1, keepdims=True) + EPS) * gain


def _dot(a, b):
    return jnp.dot(a, b, preferred_element_type=F32)


def _dot_nt(a, b):
    return lax.dot_general(a, b, (((1,), (1,)), ((), ())), preferred_element_type=F32)


def _silu(z):
    return z * jax.nn.sigmoid(z)


def _layer_a_kernel(x_ref, ng_ref, w_in_ref, vg_ref, ws_ref, bs_ref, w_out_ref,
                    o_ref, vn_ref, y_ref):
    x = x_ref[...]
    h = _rms_norm(x, ng_ref[...]).astype(BF16)
    v = _dot(h, w_in_ref[:, A_WIDTH:2 * A_WIDTH])
    vn_ref[...] = _rms_norm(v, vg_ref[...]).astype(BF16)
    row = lax.broadcasted_iota(jnp.int32, (CHUNK, CHUNK), 0)
    col = lax.broadcasted_iota(jnp.int32, (CHUNK, CHUNK), 1)
    causal = col <= row
    n_chunks = x.shape[0] // CHUNK
    for g in range(A_GROUPS):
        cols = slice(g * A_GROUP_DIM, (g + 1) * A_GROUP_DIM)
        u = _dot(h, w_in_ref[:, cols])
        z = _dot(h, w_in_ref[:, 2 * A_WIDTH + g * A_GROUP_DIM:2 * A_WIDTH + (g + 1) * A_GROUP_DIM])
        ws = jnp.where(causal, ws_ref[g], 0.0).astype(BF16)
        bias = bs_ref[:, g:g + 1]
        mixed = jnp.concatenate(
            [_dot(ws, vn_ref[c * CHUNK:(c + 1) * CHUNK, cols]) + bias for c in range(n_chunks)],
            axis=0)
        y_ref[:, cols] = (u * mixed * _silu(z)).astype(BF16)
    o_ref[...] = x + _dot(y_ref[...], w_out_ref[...])


def _layer_a(x2, ng, w_in, v_gain, w_s, b_s, w_out):
    tokens = x2.shape[0]
    tm = TOKEN_TILE
    return pl.pallas_call(
        _layer_a_kernel,
        grid=(tokens // tm,),
        in_specs=[
            pl.BlockSpec((tm, D_MODEL), lambda i: (i, 0)),
            _resident((1, D_MODEL), lambda i: (0, 0)),
            _resident((D_MODEL, 3 * A_WIDTH), lambda i: (0, 0)),
            _resident((1, A_WIDTH), lambda i: (0, 0)),
            _resident((A_GROUPS, CHUNK, CHUNK), lambda i: (0, 0, 0)),
            _resident((CHUNK, A_GROUPS), lambda i: (0, 0)),
            _resident((A_WIDTH, D_MODEL), lambda i: (0, 0)),
        ],
        out_specs=pl.BlockSpec((tm, D_MODEL), lambda i: (i, 0)),
        out_shape=jax.ShapeDtypeStruct((tokens, D_MODEL), F32),
        scratch_shapes=[pltpu.VMEM((tm, A_WIDTH), BF16), pltpu.VMEM((tm, A_WIDTH), BF16)],
        compiler_params=pltpu.CompilerParams(
            dimension_semantics=("arbitrary",), vmem_limit_bytes=VMEM_LIMIT_BYTES),
        name="layer_a",
    )(x2, ng.reshape(1, D_MODEL), w_in.astype(BF16), v_gain.reshape(1, A_WIDTH), w_s,
      b_s.T, w_out.astype(BF16))


def _layer_c_kernel(x_ref, ng_ref, w_in_ref, w_grp_ref, scale_ref, w_out_ref,
                    o_ref, ext_ref, y_ref):
    t = pl.program_id(1)
    tm = x_ref.shape[0]
    x = x_ref[...]
    h = _rms_norm(x, ng_ref[...]).astype(BF16)

    @pl.when(t == 0)
    def _():
        ext_ref[0:HALO, :] = jnp.zeros((HALO, C_WIDTH), F32)

    @pl.when(t > 0)
    def _():
        ext_ref[0:HALO, :] = ext_ref[tm:tm + HALO, :]

    ext_ref[HALO:HALO + tm, :] = _dot(h, w_in_ref[:, 0:C_WIDTH])
    pos = t * tm + lax.broadcasted_iota(jnp.int32, (tm, 1), 0)
    for g, window in enumerate(POOL_SIZES):
        cols = slice(g * C_GROUP, (g + 1) * C_GROUP)
        xc = ext_ref[HALO:HALO + tm, cols]
        acc = xc
        for j in range(1, window):
            acc = acc + ext_ref[HALO - j:HALO - j + tm, cols]
        cnt = jnp.minimum(pos + 1, window).astype(F32)
        diff = (acc / cnt - xc).astype(BF16)
        mixed = _dot(diff, w_grp_ref[g]) * scale_ref[:, cols]
        z = _dot(h, w_in_ref[:, C_WIDTH + g * C_GROUP:C_WIDTH + (g + 1) * C_GROUP])
        y_ref[:, cols] = (mixed * _silu(z)).astype(BF16)
    o_ref[...] = x + _dot(y_ref[...], w_out_ref[...])


def _layer_c(x3, ng, w_in, w_grp, scale, w_out):
    bsz, seq, _ = x3.shape
    tm = TOKEN_TILE
    return pl.pallas_call(
        _layer_c_kernel,
        grid=(bsz, seq // tm),
        in_specs=[
            pl.BlockSpec((None, tm, D_MODEL), lambda b, t: (b, t, 0)),
            _resident((1, D_MODEL), lambda b, t: (0, 0)),
            _resident((D_MODEL, 2 * C_WIDTH), lambda b, t: (0, 0)),
            _resident((N_POOL, C_GROUP, C_GROUP), lambda b, t: (0, 0, 0)),
            _resident((1, C_WIDTH), lambda b, t: (0, 0)),
            _resident((C_WIDTH, D_MODEL), lambda b, t: (0, 0)),
        ],
        out_specs=pl.BlockSpec((None, tm, D_MODEL), lambda b, t: (b, t, 0)),
        out_shape=jax.ShapeDtypeStruct((bsz, seq, D_MODEL), F32),
        scratch_shapes=[pltpu.VMEM((tm + HALO, C_WIDTH), F32), pltpu.VMEM((tm, C_WIDTH), BF16)],
        compiler_params=pltpu.CompilerParams(
            dimension_semantics=("arbitrary", "arbitrary"), vmem_limit_bytes=VMEM_LIMIT_BYTES),
        name="layer_c",
    )(x3, ng.reshape(1, D_MODEL), w_in.astype(BF16), w_grp.astype(BF16),
      scale.reshape(1, C_WIDTH), w_out.astype(BF16))


def _attn_proj_kernel(x_ref, ng_ref, freq_ref, wq_ref, wkt_ref, wv_ref, qg_ref, kg_ref,
                      q_ref, kt_ref, v_ref, h_scr, q_a, kt_a, v_a, q_b, kt_b, v_b,
                      *, dilation, n_steps, n_super):
    s = pl.program_id(0)
    rows = q_ref.shape[0]
    n_chunk = SUPER // rows
    per = SUPER // dilation
    n_cols = D_MODEL // LANES
    cur = jnp.minimum(s, n_steps - 1)
    c = cur % n_chunk
    prev = jnp.maximum(s - 1, 0)
    c_prev = prev % n_chunk
    n_prev = (prev // n_chunk) % n_super

    @pl.when(s == 0)
    def _():
        q_b[...] = jnp.zeros(q_b.shape, q_b.dtype)
        kt_b[...] = jnp.zeros(kt_b.shape, kt_b.dtype)
        v_b[...] = jnp.zeros(v_b.shape, v_b.dtype)

    @pl.when((c == 0) & (s < n_steps))
    def _():
        for i in range(SUPER // rows):
            rs = slice(i * rows, (i + 1) * rows)
            hh = _rms_norm(x_ref[rs, :], ng_ref[...])
            for j in range(n_cols):
                h_scr[j, rs, :] = hh[:, j * LANES:(j + 1) * LANES]

    def gather(j):
        if dilation == 1:
            return h_scr[j, pl.ds(pl.multiple_of(c * rows, rows), rows), :]
        run = min(per, rows)
        first = c * (rows // run)
        if per > rows:
            raise NotImplementedError("a residue class longer than the chunk needs a row offset")
        pieces = [h_scr[j, pl.ds(first + rr, run, stride=dilation), :] for rr in range(rows // run)]
        return pieces[0] if len(pieces) == 1 else jnp.concatenate(pieces, axis=0)

    def project(q_raw, kt_raw, v_raw):
        h = jnp.concatenate([gather(j) for j in range(n_cols)], axis=1).astype(BF16)
        q_raw[...] = _dot(h, wq_ref[...])
        kt_raw[...] = _dot_nt(wkt_ref[...], h)
        v_raw[...] = _dot(h, wv_ref[...]).astype(BF16)

    def finish(q_raw, kt_raw, v_raw):
        p = c_prev * rows + lax.broadcasted_iota(jnp.int32, (1, rows), 1)
        residue = lax.shift_right_logical(p, per.bit_length() - 1)
        strided = jnp.bitwise_and(p, per - 1)
        pos = n_prev * SUPER + strided * dilation + residue
        ang = pos.astype(F32) * freq_ref[...]
        cos_t, sin_t = jnp.cos(ang), jnp.sin(ang)

        v_ref[...] = v_raw[...]

        for hd in range(B_HEADS):
            slab = kt_raw[hd * HEAD_DIM:(hd + 1) * HEAD_DIM, :]
            kn = slab * lax.rsqrt(jnp.mean(slab * slab, axis=0, keepdims=True) + EPS) * kg_ref[...]
            x1, x2 = kn[0:ROPE_HALF, :], kn[ROPE_HALF:ROPE_DIM, :]
            rotated = jnp.concatenate(
                [x1 * cos_t - x2 * sin_t, x2 * cos_t + x1 * sin_t, kn[ROPE_DIM:, :]], axis=0)
            kt_ref[hd * HEAD_DIM:(hd + 1) * HEAD_DIM, :] = rotated.astype(BF16)

        rest = HEAD_DIM - ROPE_DIM
        cos_tab = jnp.concatenate([cos_t, cos_t, jnp.ones((rest, rows), F32)], axis=0).T
        sin_tab = jnp.concatenate([-sin_t, sin_t, jnp.zeros((rest, rows), F32)], axis=0).T
        lane = lax.broadcasted_iota(jnp.int32, (rows, HEAD_DIM), 1)
        for hd in range(B_HEADS):
            cols = slice(hd * HEAD_DIM, (hd + 1) * HEAD_DIM)
            qn = _rms_norm(q_raw[:, cols], qg_ref[...])
            partner = jnp.where(lane < ROPE_HALF, pltpu.roll(qn, HEAD_DIM - ROPE_HALF, 1),
                                pltpu.roll(qn, ROPE_HALF, 1))
            q_ref[:, cols] = (qn * cos_tab + partner * sin_tab).astype(BF16)

    @pl.when(s % 2 == 0)
    def _():
        project(q_a, kt_a, v_a)
        finish(q_b, kt_b, v_b)

    @pl.when(s % 2 == 1)
    def _():
        project(q_b, kt_b, v_b)
        finish(q_a, kt_a, v_a)


def _attn_proj(x3, ng, inv_freq, w_in16, w_kt16, q_gain, k_gain, group, dilation):
    bsz, seq, _ = x3.shape
    rows = TOKEN_TILE
    n_super, n_chunk = seq // SUPER, SUPER // rows
    per_batch = n_super * n_chunk
    n_steps = bsz * per_batch
    cur = lambda s: jnp.minimum(s, n_steps - 1)
    prev = lambda s: jnp.maximum(s - 1, 0)
    raw = [pltpu.VMEM((rows, B_WIDTH), F32), pltpu.VMEM((B_WIDTH, rows), F32),
           pltpu.VMEM((rows, B_WIDTH), BF16)]
    return pl.pallas_call(
        functools.partial(_attn_proj_kernel, dilation=dilation, n_steps=n_steps, n_super=n_super),
        grid=(n_steps + 1,),
        in_specs=[
            pl.BlockSpec((None, SUPER, D_MODEL),
                         lambda s: (cur(s) // per_batch, (cur(s) // n_chunk) % n_super, 0)),
            _resident((1, D_MODEL), lambda s: (0, 0)),
            _resident((ROPE_HALF, 1), lambda s: (0, 0)),
            _resident((D_MODEL, B_WIDTH), lambda s: (0, group)),
            _resident((B_WIDTH, D_MODEL), lambda s: (group, 0)),
            _resident((D_MODEL, B_WIDTH), lambda s: (0, 2 * N_B_GROUPS + group)),
            _resident((1, HEAD_DIM), lambda s: (0, 0)),
            _resident((HEAD_DIM, 1), lambda s: (0, 0)),
        ],
        out_specs=[
            pl.BlockSpec((None, rows, B_WIDTH), lambda s: (prev(s) // per_batch, prev(s) % per_batch, 0)),
            pl.BlockSpec((None, B_WIDTH, rows), lambda s: (prev(s) // per_batch, 0, prev(s) % per_batch)),
            pl.BlockSpec((None, rows, B_WIDTH), lambda s: (prev(s) // per_batch, prev(s) % per_batch, 0)),
        ],
        out_shape=[
            jax.ShapeDtypeStruct((bsz, seq, B_WIDTH), BF16),
            jax.ShapeDtypeStruct((bsz, B_WIDTH, seq), BF16),
            jax.ShapeDtypeStruct((bsz, seq, B_WIDTH), BF16),
        ],
        scratch_shapes=[pltpu.VMEM((D_MODEL // LANES, SUPER, LANES), F32)] + raw + raw,
        compiler_params=pltpu.CompilerParams(
            dimension_semantics=("arbitrary",), vmem_limit_bytes=VMEM_LIMIT_BYTES),
        name=f"attn_proj_d{dilation}",
    )(x3, ng.reshape(1, D_MODEL), inv_freq, w_in16, w_kt16, w_in16,
      q_gain.reshape(1, HEAD_DIM), k_gain.reshape(HEAD_DIM, 1))


def _attn_kernel(*refs):
    qkv_refs = refs[:3 * N_B_GROUPS]
    y_ref = refs[3 * N_B_GROUPS]
    scratch = refs[3 * N_B_GROUPS + 1:]
    kt_scrs, v_scrs = scratch[0:N_B_GROUPS], scratch[N_B_GROUPS:2 * N_B_GROUPS]
    o_scr, lse_scr = scratch[2 * N_B_GROUPS:]
    n = pl.program_id(2)

    qi = lax.broadcasted_iota(jnp.int32, (SPAN, 2 * SPAN), 0)
    ki = lax.broadcasted_iota(jnp.int32, (SPAN, 2 * SPAN), 1)
    band = (ki >= qi) & (ki <= qi + SPAN)
    band_first = band & (ki >= jnp.where(n > 0, 0, SPAN))
    scale = np.float32(1.0 / np.sqrt(HEAD_DIM))

    for g, (_, d) in enumerate(B_PATTERNS):
        q_ref, kt_ref, v_ref = qkv_refs[3 * g:3 * g + 3]
        kt_scr, v_scr = kt_scrs[g], v_scrs[g]
        per = SUPER // d
        seg = SPAN + per

        @pl.when(n == 0)
        def _():
            for r in range(d):
                kt_scr[:, r * seg:r * seg + SPAN] = jnp.zeros((kt_scr.shape[0], SPAN), BF16)
                v_scr[r * seg:r * seg + SPAN, :] = jnp.zeros((SPAN, v_scr.shape[1]), BF16)

        @pl.when(n > 0)
        def _():
            for r in range(d):
                kt_scr[:, r * seg:r * seg + SPAN] = kt_scr[:, r * seg + per:r * seg + per + SPAN]
                v_scr[r * seg:r * seg + SPAN, :] = v_scr[r * seg + per:r * seg + per + SPAN, :]

        for r in range(d):
            kt_scr[:, r * seg + SPAN:(r + 1) * seg] = kt_ref[:, r * per:(r + 1) * per]
            v_scr[r * seg + SPAN:(r + 1) * seg, :] = v_ref[r * per:(r + 1) * per, :]

        for r in range(d):
            for j in range(per // SPAN):
                mask = band_first if j == 0 else band
                q_rows = slice(r * per + j * SPAN, r * per + (j + 1) * SPAN)
                keys = slice(r * seg + j * SPAN, r * seg + (j + 2) * SPAN)
                nat = pl.ds(j * SPAN * d + r, SPAN, stride=d) if d > 1 else slice(j * SPAN, (j + 1) * SPAN)
                for hh in range(HEADS_PER_STEP):
                    cols = slice(hh * HEAD_DIM, (hh + 1) * HEAD_DIM)
                    s = _dot(q_ref[q_rows, cols], kt_scr[cols, keys]) * scale
                    s = jnp.where(mask, s, MASK_VALUE)
                    m = jnp.max(s, axis=-1, keepdims=True)
                    p = jnp.exp(s - m)
                    denom = jnp.sum(p, axis=-1, keepdims=True)
                    o = _dot(p.astype(BF16), v_scr[keys, cols])
                    o_scr[g, hh, nat, :] = o / denom
                    lse_scr[g, hh, nat, :] = jnp.broadcast_to(m + jnp.log(denom), (SPAN, HEAD_DIM))

    rows = 256
    for hh in range(HEADS_PER_STEP):
        for i in range(SUPER // rows):
            rs = slice(i * rows, (i + 1) * rows)
            lses = [lse_scr[g, hh, rs, :] for g in range(N_B_GROUPS)]
            m = jnp.maximum(jnp.maximum(lses[0], lses[1]), lses[2])
            es = [jnp.exp(l - m) for l in lses]
            total = es[0] + es[1] + es[2]
            o = sum((es[g] / total) * o_scr[g, hh, rs, :] for g in range(N_B_GROUPS))
            y_ref[rs, hh * HEAD_DIM:(hh + 1) * HEAD_DIM] = o


def _attn(qkvs, bsz, seq):
    width = HEADS_PER_STEP * HEAD_DIM
    n_super = seq // SUPER
    in_specs, operands, kt_scr, v_scr = [], [], [], []
    for (q, kt, v), (_, d) in zip(qkvs, B_PATTERNS):
        in_specs += [
            pl.BlockSpec((None, SUPER, width), lambda b, hp, n: (b, n, hp)),
            pl.BlockSpec((None, width, SUPER), lambda b, hp, n: (b, hp, n)),
            pl.BlockSpec((None, SUPER, width), lambda b, hp, n: (b, n, hp)),
        ]
        operands += [q, kt, v]
        kt_scr.append(pltpu.VMEM((width, SUPER + d * SPAN), BF16))
        v_scr.append(pltpu.VMEM((SUPER + d * SPAN, width), BF16))
    return pl.pallas_call(
        _attn_kernel,
        grid=(bsz, B_HEADS // HEADS_PER_STEP, n_super),
        in_specs=in_specs,
        out_specs=pl.BlockSpec((None, SUPER, width), lambda b, hp, n: (b, n, hp)),
        out_shape=jax.ShapeDtypeStruct((bsz, seq, B_WIDTH), F32),
        scratch_shapes=kt_scr + v_scr + [
            pltpu.VMEM((N_B_GROUPS, HEADS_PER_STEP, SUPER, HEAD_DIM), F32),
            pltpu.VMEM((N_B_GROUPS, HEADS_PER_STEP, SUPER, HEAD_DIM), F32),
        ],
        compiler_params=pltpu.CompilerParams(
            dimension_semantics=("arbitrary", "arbitrary", "arbitrary"),
            vmem_limit_bytes=VMEM_LIMIT_BYTES),
        name="attn",
    )(*operands)


def _attn_out_kernel(x_ref, ng_ref, wz_ref, o_ref, w_out_ref, out_ref):
    x = x_ref[...]
    h = _rms_norm(x, ng_ref[...]).astype(BF16)
    z = _dot(h, wz_ref[...])
    y = (o_ref[...] * _silu(z)).astype(BF16)
    out_ref[...] = x + _dot(y, w_out_ref[...])


def _attn_out(x2, ng, w_in16, o2, w_out):
    tokens = x2.shape[0]
    tm = TOKEN_TILE
    tile = lambda width: pl.BlockSpec((tm, width), lambda i: (i, 0))
    return pl.pallas_call(
        _attn_out_kernel,
        grid=(tokens // tm,),
        in_specs=[
            tile(D_MODEL),
            _resident((1, D_MODEL), lambda i: (0, 0)),
            _resident((D_MODEL, B_WIDTH), lambda i: (0, 3 * N_B_GROUPS)),
            tile(B_WIDTH),
            _resident((B_WIDTH, D_MODEL), lambda i: (0, 0)),
        ],
        out_specs=tile(D_MODEL),
        out_shape=jax.ShapeDtypeStruct((tokens, D_MODEL), F32),
        compiler_params=pltpu.CompilerParams(
            dimension_semantics=("arbitrary",), vmem_limit_bytes=VMEM_LIMIT_BYTES),
        name="attn_out",
    )(x2, ng.reshape(1, D_MODEL), w_in16, o2, w_out.astype(BF16))


def _layer_b(x3, ng, w_in, q_gain, k_gain, w_out):
    bsz, seq, _ = x3.shape
    w_in16 = w_in.astype(BF16)
    w_kt16 = w_in16[:, N_B_GROUPS * B_WIDTH:2 * N_B_GROUPS * B_WIDTH].T
    inv_freq = jnp.power(jnp.float32(ROPE_THETA), -jnp.arange(ROPE_HALF, dtype=F32) / ROPE_HALF)
    inv_freq = inv_freq.reshape(ROPE_HALF, 1)
    qkvs = []
    for g, (window, dilation) in enumerate(B_PATTERNS):
        assert window // dilation == SPAN
        qkvs.append(_attn_proj(x3, ng, inv_freq, w_in16, w_kt16, q_gain[g], k_gain[g], g, dilation))
    o = _attn(qkvs, bsz, seq)
    x2 = x3.reshape(bsz * seq, D_MODEL)
    return _attn_out(x2, ng, w_in16, o.reshape(bsz * seq, B_WIDTH), w_out).reshape(bsz, seq, D_MODEL)


def kernel(x, norm_gain, a_w_in, a_v_gain, a_w_s, a_b_s, a_w_out, b_w_in, b_q_gain, b_k_gain,
           b_w_out, c_w_in, c_w_grp, c_scale, c_w_out):
    bsz, seq, d_model = x.shape
    assert d_model == D_MODEL and seq % SUPER == 0 and SUPER % TOKEN_TILE == 0
    depth = norm_gain.shape[0]
    for i in range(depth):
        kind, j = i % 3, i // 3
        if kind == 0:
            x = _layer_a(x.reshape(bsz * seq, D_MODEL), norm_gain[i], a_w_in[j], a_v_gain[j],
                         a_w_s[j], a_b_s[j], a_w_out[j]).reshape(bsz, seq, D_MODEL)
        elif kind == 1:
            x = _layer_b(x, norm_gain[i], b_w_in[j], b_q_gain[j], b_k_gain[j], b_w_out[j])
        else:
            x = _layer_c(x, norm_gain[i], c_w_in[j], c_w_grp[j], c_scale[j], c_w_out[j])
    return x
```

```python
import functools
import math

import jax
import jax.numpy as jnp
import numpy as np
from jax import lax
from jax.experimental import pallas as pl
from jax.experimental.pallas import tpu as pltpu

D_MODEL = 1024
EPS = 1e-6
CHUNK = 128
A_WIDTH = 2 * D_MODEL
A_GROUPS = 8
A_GROUP_DIM = A_WIDTH // A_GROUPS
HEAD_DIM = 128
B_HEADS = D_MODEL // HEAD_DIM
B_PATTERNS = ((128, 1), (512, 4), (2048, 16))
N_B_GROUPS = len(B_PATTERNS)
B_WIDTH = B_HEADS * HEAD_DIM
ROPE_DIM = HEAD_DIM // 4
ROPE_HALF = ROPE_DIM // 2
ROPE_THETA = 500000.0
SPAN = 128
SUPER = SPAN * max(d for _, d in B_PATTERNS)
HEADS_PER_STEP = 2
MERGE_DIL = 4
POOL_SIZES = (2, 4, 8, 16)
N_POOL = len(POOL_SIZES)
C_WIDTH = 2 * D_MODEL
C_GROUP = C_WIDTH // N_POOL
HALO = 16

LANES = 128
VMEM_LIMIT_BYTES = 56 * 1024 * 1024

TOKEN_TILE = 512
MASK_VALUE = -1e30

F32 = jnp.float32
BF16 = jnp.bfloat16


def _resident(shape, index_map):
    return pl.BlockSpec(shape, index_map, pipeline_mode=pl.Buffered(1))


def _rms_norm(x, gain):
    return x * lax.rsqrt(jnp.mean(x * x, axis=-1, keepdims=True) + EPS) * gain


def _dot(a, b):
    return jnp.dot(a, b, preferred_element_type=F32)


def _dot_nt(a, b):
    return lax.dot_general(a, b, (((1,), (1,)), ((), ())), preferred_element_type=F32)


def _silu(z):
    return z * jax.nn.sigmoid(z)


def _layer_a_kernel(x_ref, ng_ref, w_in_ref, vg_ref, ws_ref, bs_ref, w_out_ref,
                    o_ref, vn_ref, y_ref):
    x = x_ref[...]
    h = _rms_norm(x, ng_ref[...]).astype(BF16)
    v = _dot(h, w_in_ref[:, A_WIDTH:2 * A_WIDTH])
    vn_ref[...] = _rms_norm(v, vg_ref[...]).astype(BF16)
    row = lax.broadcasted_iota(jnp.int32, (CHUNK, CHUNK), 0)
    col = lax.broadcasted_iota(jnp.int32, (CHUNK, CHUNK), 1)
    causal = col <= row
    n_chunks = x.shape[0] // CHUNK
    for g in range(A_GROUPS):
        cols = slice(g * A_GROUP_DIM, (g + 1) * A_GROUP_DIM)
        u = _dot(h, w_in_ref[:, cols])
        z = _dot(h, w_in_ref[:, 2 * A_WIDTH + g * A_GROUP_DIM:2 * A_WIDTH + (g + 1) * A_GROUP_DIM])
        ws = jnp.where(causal, ws_ref[g], 0.0).astype(BF16)
        bias = bs_ref[:, g:g + 1]
        mixed = jnp.concatenate(
            [_dot(ws, vn_ref[c * CHUNK:(c + 1) * CHUNK, cols]) + bias for c in range(n_chunks)],
            axis=0)
        y_ref[:, cols] = (u * mixed * _silu(z)).astype(BF16)
    o_ref[...] = x + _dot(y_ref[...], w_out_ref[...])


def _layer_a(x2, ng, w_in, v_gain, w_s, b_s, w_out):
    tokens = x2.shape[0]
    tm = TOKEN_TILE
    return pl.pallas_call(
        _layer_a_kernel,
        grid=(tokens // tm,),
        in_specs=[
            pl.BlockSpec((tm, D_MODEL), lambda i: (i, 0)),
            _resident((1, D_MODEL), lambda i: (0, 0)),
            _resident((D_MODEL, 3 * A_WIDTH), lambda i: (0, 0)),
            _resident((1, A_WIDTH), lambda i: (0, 0)),
            _resident((A_GROUPS, CHUNK, CHUNK), lambda i: (0, 0, 0)),
            _resident((CHUNK, A_GROUPS), lambda i: (0, 0)),
            _resident((A_WIDTH, D_MODEL), lambda i: (0, 0)),
        ],
        out_specs=pl.BlockSpec((tm, D_MODEL), lambda i: (i, 0)),
        out_shape=jax.ShapeDtypeStruct((tokens, D_MODEL), F32),
        scratch_shapes=[pltpu.VMEM((tm, A_WIDTH), BF16), pltpu.VMEM((tm, A_WIDTH), BF16)],
        compiler_params=pltpu.CompilerParams(
            dimension_semantics=("arbitrary",), vmem_limit_bytes=VMEM_LIMIT_BYTES),
        name="layer_a",
    )(x2, ng.reshape(1, D_MODEL), w_in.astype(BF16), v_gain.reshape(1, A_WIDTH), w_s,
      b_s.T, w_out.astype(BF16))


def _layer_c_kernel(x_ref, ng_ref, w_in_ref, w_grp_ref, scale_ref, w_out_ref,
                    o_ref, ext_ref, y_ref):
    t = pl.program_id(1)
    tm = x_ref.shape[0]
    x = x_ref[...]
    h = _rms_norm(x, ng_ref[...]).astype(BF16)

    @pl.when(t == 0)
    def _():
        ext_ref[0:HALO, :] = jnp.zeros((HALO, C_WIDTH), F32)

    @pl.when(t > 0)
    def _():
        ext_ref[0:HALO, :] = ext_ref[tm:tm + HALO, :]

    ext_ref[HALO:HALO + tm, :] = _dot(h, w_in_ref[:, 0:C_WIDTH])
    pos = t * tm + lax.broadcasted_iota(jnp.int32, (tm, 1), 0)
    for g, window in enumerate(POOL_SIZES):
        cols = slice(g * C_GROUP, (g + 1) * C_GROUP)
        xc = ext_ref[HALO:HALO + tm, cols]
        acc = xc
        for j in range(1, window):
            acc = acc + ext_ref[HALO - j:HALO - j + tm, cols]
        cnt = jnp.minimum(pos + 1, window).astype(F32)
        diff = (acc / cnt - xc).astype(BF16)
        mixed = _dot(diff, w_grp_ref[g]) * scale_ref[:, cols]
        z = _dot(h, w_in_ref[:, C_WIDTH + g * C_GROUP:C_WIDTH + (g + 1) * C_GROUP])
        y_ref[:, cols] = (mixed * _silu(z)).astype(BF16)
    o_ref[...] = x + _dot(y_ref[...], w_out_ref[...])


def _layer_c(x3, ng, w_in, w_grp, scale, w_out):
    bsz, seq, _ = x3.shape
    tm = TOKEN_TILE
    return pl.pallas_call(
        _layer_c_kernel,
        grid=(bsz, seq // tm),
        in_specs=[
            pl.BlockSpec((None, tm, D_MODEL), lambda b, t: (b, t, 0)),
            _resident((1, D_MODEL), lambda b, t: (0, 0)),
            _resident((D_MODEL, 2 * C_WIDTH), lambda b, t: (0, 0)),
            _resident((N_POOL, C_GROUP, C_GROUP), lambda b, t: (0, 0, 0)),
            _resident((1, C_WIDTH), lambda b, t: (0, 0)),
            _resident((C_WIDTH, D_MODEL), lambda b, t: (0, 0)),
        ],
        out_specs=pl.BlockSpec((None, tm, D_MODEL), lambda b, t: (b, t, 0)),
        out_shape=jax.ShapeDtypeStruct((bsz, seq, D_MODEL), F32),
        scratch_shapes=[pltpu.VMEM((tm + HALO, C_WIDTH), F32), pltpu.VMEM((tm, C_WIDTH), BF16)],
        compiler_params=pltpu.CompilerParams(
            dimension_semantics=("arbitrary", "arbitrary"), vmem_limit_bytes=VMEM_LIMIT_BYTES),
        name="layer_c",
    )(x3, ng.reshape(1, D_MODEL), w_in.astype(BF16), w_grp.astype(BF16),
      scale.reshape(1, C_WIDTH), w_out.astype(BF16))


def _attn_proj_kernel(x_ref, ng_ref, freq_ref, wq_ref, wkt_ref, wv_ref, qg_ref, kg_ref,
                      q_ref, kt_ref, v_ref, h_scr, q_a, kt_a, v_a, q_b, kt_b, v_b,
                      *, dilation, n_steps, n_super):
    s = pl.program_id(0)
    rows = q_ref.shape[0]
    n_chunk = SUPER // rows
    per = SUPER // dilation
    n_cols = D_MODEL // LANES
    cur = jnp.minimum(s, n_steps - 1)
    c = cur % n_chunk
    prev = jnp.maximum(s - 1, 0)
    c_prev = prev % n_chunk
    n_prev = (prev // n_chunk) % n_super

    @pl.when(s == 0)
    def _():
        q_b[...] = jnp.zeros(q_b.shape, q_b.dtype)
        kt_b[...] = jnp.zeros(kt_b.shape, kt_b.dtype)
        v_b[...] = jnp.zeros(v_b.shape, v_b.dtype)

    @pl.when((c == 0) & (s < n_steps))
    def _():
        for i in range(SUPER // rows):
            rs = slice(i * rows, (i + 1) * rows)
            hh = _rms_norm(x_ref[rs, :], ng_ref[...])
            for j in range(n_cols):
                h_scr[j, rs, :] = hh[:, j * LANES:(j + 1) * LANES]

    def gather(j):
        if dilation == 1:
            return h_scr[j, pl.ds(pl.multiple_of(c * rows, rows), rows), :]
        run = min(per, rows)
        first = c * (rows // run)
        if per > rows:
            raise NotImplementedError("a residue class longer than the chunk needs a row offset")
        pieces = [h_scr[j, pl.ds(first + rr, run, stride=dilation), :] for rr in range(rows // run)]
        return pieces[0] if len(pieces) == 1 else jnp.concatenate(pieces, axis=0)

    def project(q_raw, kt_raw, v_raw):
        h = jnp.concatenate([gather(j) for j in range(n_cols)], axis=1).astype(BF16)
        q_raw[...] = _dot(h, wq_ref[...])
        kt_raw[...] = _dot_nt(wkt_ref[...], h)
        v_raw[...] = _dot(h, wv_ref[...]).astype(BF16)

    def finish(q_raw, kt_raw, v_raw):
        p = c_prev * rows + lax.broadcasted_iota(jnp.int32, (1, rows), 1)
        residue = lax.shift_right_logical(p, per.bit_length() - 1)
        strided = jnp.bitwise_and(p, per - 1)
        pos = n_prev * SUPER + strided * dilation + residue
        ang = pos.astype(F32) * freq_ref[...]
        cos_t, sin_t = jnp.cos(ang), jnp.sin(ang)

        v_ref[...] = v_raw[...]

        for hd in range(B_HEADS):
            slab = kt_raw[hd * HEAD_DIM:(hd + 1) * HEAD_DIM, :]
            kn = slab * lax.rsqrt(jnp.mean(slab * slab, axis=0, keepdims=True) + EPS) * kg_ref[...]
            x1, x2 = kn[0:ROPE_HALF, :], kn[ROPE_HALF:ROPE_DIM, :]
            rotated = jnp.concatenate(
                [x1 * cos_t - x2 * sin_t, x2 * cos_t + x1 * sin_t, kn[ROPE_DIM:, :]], axis=0)
            kt_ref[hd * HEAD_DIM:(hd + 1) * HEAD_DIM, :] = rotated.astype(BF16)

        rest = HEAD_DIM - ROPE_DIM
        cos_tab = jnp.concatenate([cos_t, cos_t, jnp.ones((rest, rows), F32)], axis=0).T
        sin_tab = jnp.concatenate([-sin_t, sin_t, jnp.zeros((rest, rows), F32)], axis=0).T
        lane = lax.broadcasted_iota(jnp.int32, (rows, HEAD_DIM), 1)
        for hd in range(B_HEADS):
            cols = slice(hd * HEAD_DIM, (hd + 1) * HEAD_DIM)
            qn = _rms_norm(q_raw[:, cols], qg_ref[...])
            partner = jnp.where(lane < ROPE_HALF, pltpu.roll(qn, HEAD_DIM - ROPE_HALF, 1),
                                pltpu.roll(qn, ROPE_HALF, 1))
            q_ref[:, cols] = (qn * cos_tab + partner * sin_tab).astype(BF16)

    @pl.when(s % 2 == 0)
    def _():
        project(q_a, kt_a, v_a)
        finish(q_b, kt_b, v_b)

    @pl.when(s % 2 == 1)
    def _():
        project(q_b, kt_b, v_b)
        finish(q_a, kt_a, v_a)


def _attn_proj(x3, ng, inv_freq, w_in16, w_kt16, q_gain, k_gain, group, dilation):
    bsz, seq, _ = x3.shape
    rows = TOKEN_TILE
    n_super, n_chunk = seq // SUPER, SUPER // rows
    per_batch = n_super * n_chunk
    n_steps = bsz * per_batch
    cur = lambda s: jnp.minimum(s, n_steps - 1)
    prev = lambda s: jnp.maximum(s - 1, 0)
    raw = [pltpu.VMEM((rows, B_WIDTH), F32), pltpu.VMEM((B_WIDTH, rows), F32),
           pltpu.VMEM((rows, B_WIDTH), BF16)]
    return pl.pallas_call(
        functools.partial(_attn_proj_kernel, dilation=dilation, n_steps=n_steps, n_super=n_super),
        grid=(n_steps + 1,),
        in_specs=[
            pl.BlockSpec((None, SUPER, D_MODEL),
                         lambda s: (cur(s) // per_batch, (cur(s) // n_chunk) % n_super, 0)),
            _resident((1, D_MODEL), lambda s: (0, 0)),
            _resident((ROPE_HALF, 1), lambda s: (0, 0)),
            _resident((D_MODEL, B_WIDTH), lambda s: (0, group)),
            _resident((B_WIDTH, D_MODEL), lambda s: (group, 0)),
            _resident((D_MODEL, B_WIDTH), lambda s: (0, 2 * N_B_GROUPS + group)),
            _resident((1, HEAD_DIM), lambda s: (0, 0)),
            _resident((HEAD_DIM, 1), lambda s: (0, 0)),
        ],
        out_specs=[
            pl.BlockSpec((None, rows, B_WIDTH), lambda s: (prev(s) // per_batch, prev(s) % per_batch, 0)),
            pl.BlockSpec((None, B_WIDTH, rows), lambda s: (prev(s) // per_batch, 0, prev(s) % per_batch)),
            pl.BlockSpec((None, rows, B_WIDTH), lambda s: (prev(s) // per_batch, prev(s) % per_batch, 0)),
        ],
        out_shape=[
            jax.ShapeDtypeStruct((bsz, seq, B_WIDTH), BF16),
            jax.ShapeDtypeStruct((bsz, B_WIDTH, seq), BF16),
            jax.ShapeDtypeStruct((bsz, seq, B_WIDTH), BF16),
        ],
        scratch_shapes=[pltpu.VMEM((D_MODEL // LANES, SUPER, LANES), F32)] + raw + raw,
        compiler_params=pltpu.CompilerParams(
            dimension_semantics=("arbitrary",), vmem_limit_bytes=VMEM_LIMIT_BYTES),
        name=f"attn_proj_d{dilation}",
    )(x3, ng.reshape(1, D_MODEL), inv_freq, w_in16, w_kt16, w_in16,
      q_gain.reshape(1, HEAD_DIM), k_gain.reshape(HEAD_DIM, 1))


def _attn_kernel(*refs):
    qkv_refs = refs[:3 * N_B_GROUPS]
    y_ref = refs[3 * N_B_GROUPS]
    scratch = refs[3 * N_B_GROUPS + 1:]
    kt_scrs, v_scrs = scratch[0:N_B_GROUPS], scratch[N_B_GROUPS:2 * N_B_GROUPS]
    o_scr, lse_scr, bias_scr = scratch[2 * N_B_GROUPS:]
    n = pl.program_id(2)

    qi = lax.broadcasted_iota(jnp.int32, (SPAN, 2 * SPAN), 0)
    ki = lax.broadcasted_iota(jnp.int32, (SPAN, 2 * SPAN), 1)
    band = (ki >= qi) & (ki <= qi + SPAN)
    bias_scr[0] = jnp.where(band, 0.0, MASK_VALUE)
    bias_scr[1] = jnp.where(band & (ki >= jnp.where(n > 0, 0, SPAN)), 0.0, MASK_VALUE)
    scale = 1.0 / math.sqrt(HEAD_DIM)
    exp2_scale = np.float32(scale * math.log2(math.e))

    for g, (_, d) in enumerate(B_PATTERNS):
        q_ref, kt_ref, v_ref = qkv_refs[3 * g:3 * g + 3]
        kt_scr, v_scr = kt_scrs[g], v_scrs[g]
        per = SUPER // d
        seg = SPAN + per

        @pl.when(n == 0)
        def _():
            v_scr[...] = jnp.ones(v_scr.shape, BF16)
            for r in range(d):
                kt_scr[:, r * seg:r * seg + SPAN] = jnp.zeros((kt_scr.shape[0], SPAN), BF16)
                for hh in range(HEADS_PER_STEP):
                    v_scr[r * seg:r * seg + SPAN, 2 * hh * HEAD_DIM:(2 * hh + 1) * HEAD_DIM] = (
                        jnp.zeros((SPAN, HEAD_DIM), BF16))

        @pl.when(n > 0)
        def _():
            for r in range(d):
                kt_scr[:, r * seg:r * seg + SPAN] = kt_scr[:, r * seg + per:r * seg + per + SPAN]
                v_scr[r * seg:r * seg + SPAN, :] = v_scr[r * seg + per:r * seg + per + SPAN, :]

        for r in range(d):
            kt_scr[:, r * seg + SPAN:(r + 1) * seg] = kt_ref[:, r * per:(r + 1) * per]
            for hh in range(HEADS_PER_STEP):
                v_scr[r * seg + SPAN:(r + 1) * seg, 2 * hh * HEAD_DIM:(2 * hh + 1) * HEAD_DIM] = (
                    v_ref[r * per:(r + 1) * per, hh * HEAD_DIM:(hh + 1) * HEAD_DIM])

        for r in range(d):
            for j in range(per // SPAN):
                q_rows = slice(r * per + j * SPAN, r * per + (j + 1) * SPAN)
                keys = slice(r * seg + j * SPAN, r * seg + (j + 2) * SPAN)
                if d < MERGE_DIL:
                    assert d == 1
                    nat = slice(j * SPAN, (j + 1) * SPAN)
                else:
                    fine = d // MERGE_DIL
                    start = (r % MERGE_DIL) * (SUPER // MERGE_DIL) + fine * j * SPAN + r // MERGE_DIL
                    nat = pl.ds(start, SPAN, stride=fine) if fine > 1 else slice(start, start + SPAN)
                for hh in range(HEADS_PER_STEP):
                    cols = slice(hh * HEAD_DIM, (hh + 1) * HEAD_DIM)
                    s = _dot(q_ref[q_rows, cols], kt_scr[cols, keys]) + bias_scr[1 if j == 0 else 0]
                    m = jnp.max(s, axis=-1, keepdims=True)
                    p = jnp.exp2((s - m) * exp2_scale)
                    pv = _dot(p.astype(BF16), v_scr[keys, 2 * hh * HEAD_DIM:(2 * hh + 2) * HEAD_DIM])
                    denom = pv[:, HEAD_DIM:]
                    o_scr[g, hh, nat, :] = pv[:, :HEAD_DIM] * (1.0 / denom)
                    lse_scr[g, hh, nat, :] = m * np.float32(scale) + jnp.log(denom)

    rows = 256
    for hh in range(HEADS_PER_STEP):
        for b in range(MERGE_DIL):
            for u0 in range(0, SUPER // MERGE_DIL, rows):
                natural = pl.ds(b + MERGE_DIL * u0, rows, stride=MERGE_DIL)
                ordered = slice(b * (SUPER // MERGE_DIL) + u0, b * (SUPER // MERGE_DIL) + u0 + rows)
                idx = [natural if d < MERGE_DIL else ordered for _, d in B_PATTERNS]
                lses = [lse_scr[g, hh, idx[g], :] for g in range(N_B_GROUPS)]
                m = jnp.maximum(jnp.maximum(lses[0], lses[1]), lses[2])
                es = [jnp.exp(l - m) for l in lses]
                mixed = (es[0] * o_scr[0, hh, idx[0], :] + es[1] * o_scr[1, hh, idx[1], :]
                         + es[2] * o_scr[2, hh, idx[2], :])
                y_ref[hh, natural, :] = mixed * (1.0 / (es[0] + es[1] + es[2]))


def _attn(qkvs, bsz, seq):
    width = HEADS_PER_STEP * HEAD_DIM
    n_super = seq // SUPER
    in_specs, operands, kt_scr, v_scr = [], [], [], []
    for (q, kt, v), (_, d) in zip(qkvs, B_PATTERNS):
        in_specs += [
            pl.BlockSpec((None, SUPER, width), lambda b, hp, n: (b, n, hp)),
            pl.BlockSpec((None, width, SUPER), lambda b, hp, n: (b, hp, n)),
            pl.BlockSpec((None, SUPER, width), lambda b, hp, n: (b, n, hp)),
        ]
        operands += [q, kt, v]
        kt_scr.append(pltpu.VMEM((width, SUPER + d * SPAN), BF16))
        v_scr.append(pltpu.VMEM((SUPER + d * SPAN, 2 * width), BF16))
    return pl.pallas_call(
        _attn_kernel,
        grid=(bsz, B_HEADS // HEADS_PER_STEP, n_super),
        in_specs=in_specs,
        out_specs=pl.BlockSpec((None, HEADS_PER_STEP, SUPER, HEAD_DIM), lambda b, hp, n: (b, hp, n, 0)),
        out_shape=jax.ShapeDtypeStruct((bsz, B_HEADS, seq, HEAD_DIM), F32),
        scratch_shapes=kt_scr + v_scr + [
            pltpu.VMEM((N_B_GROUPS, HEADS_PER_STEP, SUPER, HEAD_DIM), F32),
            pltpu.VMEM((N_B_GROUPS, HEADS_PER_STEP, SUPER, HEAD_DIM), F32),
            pltpu.VMEM((2, SPAN, 2 * SPAN), F32),
        ],
        compiler_params=pltpu.CompilerParams(
            dimension_semantics=("arbitrary", "arbitrary", "arbitrary"),
            vmem_limit_bytes=VMEM_LIMIT_BYTES),
        name="attn",
    )(*operands)


def _attn_out_kernel(x_ref, ng_ref, wz_ref, o_ref, w_out_ref, out_ref):
    x = x_ref[...]
    h = _rms_norm(x, ng_ref[...]).astype(BF16)
    z = _dot(h, wz_ref[...])
    o = jnp.concatenate([o_ref[hd] for hd in range(B_HEADS)], axis=1)
    y = (o * _silu(z)).astype(BF16)
    out_ref[...] = x + _dot(y, w_out_ref[...])


def _attn_out(x3, ng, w_in16, o, w_out):
    bsz, seq, _ = x3.shape
    tm = TOKEN_TILE
    tile = pl.BlockSpec((None, tm, D_MODEL), lambda b, t: (b, t, 0))
    return pl.pallas_call(
        _attn_out_kernel,
        grid=(bsz, seq // tm),
        in_specs=[
            tile,
            _resident((1, D_MODEL), lambda b, t: (0, 0)),
            _resident((D_MODEL, B_WIDTH), lambda b, t: (0, 3 * N_B_GROUPS)),
            pl.BlockSpec((None, B_HEADS, tm, HEAD_DIM), lambda b, t: (b, 0, t, 0)),
            _resident((B_WIDTH, D_MODEL), lambda b, t: (0, 0)),
        ],
        out_specs=tile,
        out_shape=jax.ShapeDtypeStruct((bsz, seq, D_MODEL), F32),
        compiler_params=pltpu.CompilerParams(
            dimension_semantics=("arbitrary", "arbitrary"), vmem_limit_bytes=VMEM_LIMIT_BYTES),
        name="attn_out",
    )(x3, ng.reshape(1, D_MODEL), w_in16, o, w_out.astype(BF16))


def _layer_b(x3, ng, w_in, q_gain, k_gain, w_out):
    bsz, seq, _ = x3.shape
    w_in16 = w_in.astype(BF16)
    w_kt16 = w_in16[:, N_B_GROUPS * B_WIDTH:2 * N_B_GROUPS * B_WIDTH].T
    inv_freq = jnp.power(jnp.float32(ROPE_THETA), -jnp.arange(ROPE_HALF, dtype=F32) / ROPE_HALF)
    inv_freq = inv_freq.reshape(ROPE_HALF, 1)
    qkvs = []
    for g, (window, dilation) in enumerate(B_PATTERNS):
        assert window // dilation == SPAN
        qkvs.append(_attn_proj(x3, ng, inv_freq, w_in16, w_kt16, q_gain[g], k_gain[g], g, dilation))
    return _attn_out(x3, ng, w_in16, _attn(qkvs, bsz, seq), w_out)


def kernel(x, norm_gain, a_w_in, a_v_gain, a_w_s, a_b_s, a_w_out, b_w_in, b_q_gain, b_k_gain,
           b_w_out, c_w_in, c_w_grp, c_scale, c_w_out):
    bsz, seq, d_model = x.shape
    assert d_model == D_MODEL and seq % SUPER == 0 and SUPER % TOKEN_TILE == 0
    depth = norm_gain.shape[0]
    for i in range(depth):
        kind, j = i % 3, i // 3
        if kind == 0:
            x = _layer_a(x.reshape(bsz * seq, D_MODEL), norm_gain[i], a_w_in[j], a_v_gain[j],
                         a_w_s[j], a_b_s[j], a_w_out[j]).reshape(bsz, seq, D_MODEL)
        elif kind == 1:
            x = _layer_b(x, norm_gain[i], b_w_in[j], b_q_gain[j], b_k_gain[j], b_w_out[j])
        else:
            x = _layer_c(x, norm_gain[i], c_w_in[j], c_w_grp[j], c_scale[j], c_w_out[j])
    return x
```

```python
import functools
import math

import jax
import jax.numpy as jnp
import numpy as np
from jax import lax
from jax.experimental import pallas as pl
from jax.experimental.pallas import tpu as pltpu

D_MODEL = 1024
EPS = 1e-6
CHUNK = 128
A_WIDTH = 2 * D_MODEL
A_GROUPS = 8
A_GROUP_DIM = A_WIDTH // A_GROUPS
HEAD_DIM = 128
B_HEADS = D_MODEL // HEAD_DIM
B_PATTERNS = ((128, 1), (512, 4), (2048, 16))
N_B_GROUPS = len(B_PATTERNS)
B_WIDTH = B_HEADS * HEAD_DIM
ROPE_DIM = HEAD_DIM // 4
ROPE_HALF = ROPE_DIM // 2
ROPE_THETA = 500000.0
Q_PRESCALE = math.log2(math.e) / math.sqrt(HEAD_DIM)
SPAN = 128
SUPER = SPAN * max(d for _, d in B_PATTERNS)
HEADS_PER_STEP = 2
MERGE_DIL = 4
POOL_SIZES = (2, 4, 8, 16)
N_POOL = len(POOL_SIZES)
C_WIDTH = 2 * D_MODEL
C_GROUP = C_WIDTH // N_POOL
HALO = 16

LANES = 128
VMEM_LIMIT_BYTES = 56 * 1024 * 1024

TOKEN_TILE = 512
LAYER_TILE = 1024
MASK_VALUE = -1e30

F32 = jnp.float32
BF16 = jnp.bfloat16


def _resident(shape, index_map):
    return pl.BlockSpec(shape, index_map, pipeline_mode=pl.Buffered(1))


def _rms_norm(x, gain):
    return x * lax.rsqrt(jnp.mean(x * x, axis=-1, keepdims=True) + EPS) * gain


def _dot(a, b):
    return jnp.dot(a, b, preferred_element_type=F32)


def _dot_nt(a, b):
    return lax.dot_general(a, b, (((1,), (1,)), ((), ())), preferred_element_type=F32)


def _silu(z):
    return z * jax.nn.sigmoid(z)


def _layer_a_kernel(x_ref, ng_ref, w_in_ref, vg_ref, ws_ref, bs_ref, w_out_ref,
                    o_ref, vn_ref, y_ref):
    x = x_ref[...]
    h = _rms_norm(x, ng_ref[...]).astype(BF16)
    v = _dot(h, w_in_ref[:, A_WIDTH:2 * A_WIDTH])
    vn_ref[...] = _rms_norm(v, vg_ref[...]).astype(BF16)
    row = lax.broadcasted_iota(jnp.int32, (CHUNK, CHUNK), 0)
    col = lax.broadcasted_iota(jnp.int32, (CHUNK, CHUNK), 1)
    causal = col <= row
    n_chunks = x.shape[0] // CHUNK
    for g in range(A_GROUPS):
        cols = slice(g * A_GROUP_DIM, (g + 1) * A_GROUP_DIM)
        u = _dot(h, w_in_ref[:, cols])
        z = _dot(h, w_in_ref[:, 2 * A_WIDTH + g * A_GROUP_DIM:2 * A_WIDTH + (g + 1) * A_GROUP_DIM])
        ws = jnp.where(causal, ws_ref[g], 0.0).astype(BF16)
        bias = bs_ref[:, g:g + 1]
        mixed = jnp.concatenate(
            [_dot(ws, vn_ref[c * CHUNK:(c + 1) * CHUNK, cols]) + bias for c in range(n_chunks)],
            axis=0)
        y_ref[:, cols] = (u * mixed * _silu(z)).astype(BF16)
    o_ref[...] = x + _dot(y_ref[...], w_out_ref[...])


def _layer_a(x2, ng, w_in, v_gain, w_s, b_s, w_out):
    tokens = x2.shape[0]
    tm = LAYER_TILE
    return pl.pallas_call(
        _layer_a_kernel,
        grid=(tokens // tm,),
        in_specs=[
            pl.BlockSpec((tm, D_MODEL), lambda i: (i, 0)),
            _resident((1, D_MODEL), lambda i: (0, 0)),
            _resident((D_MODEL, 3 * A_WIDTH), lambda i: (0, 0)),
            _resident((1, A_WIDTH), lambda i: (0, 0)),
            _resident((A_GROUPS, CHUNK, CHUNK), lambda i: (0, 0, 0)),
            _resident((CHUNK, A_GROUPS), lambda i: (0, 0)),
            _resident((A_WIDTH, D_MODEL), lambda i: (0, 0)),
        ],
        out_specs=pl.BlockSpec((tm, D_MODEL), lambda i: (i, 0)),
        out_shape=jax.ShapeDtypeStruct((tokens, D_MODEL), F32),
        scratch_shapes=[pltpu.VMEM((tm, A_WIDTH), BF16), pltpu.VMEM((tm, A_WIDTH), BF16)],
        compiler_params=pltpu.CompilerParams(
            dimension_semantics=("arbitrary",), vmem_limit_bytes=VMEM_LIMIT_BYTES),
        name="layer_a",
    )(x2, ng.reshape(1, D_MODEL), w_in.astype(BF16), v_gain.reshape(1, A_WIDTH), w_s,
      b_s.T, w_out.astype(BF16))


def _layer_c_kernel(x_ref, ng_ref, w_in_ref, w_grp_ref, scale_ref, w_out_ref,
                    o_ref, ext_ref, y_ref):
    t = pl.program_id(1)
    tm = x_ref.shape[0]
    x = x_ref[...]
    h = _rms_norm(x, ng_ref[...]).astype(BF16)

    @pl.when(t == 0)
    def _():
        ext_ref[0:HALO, :] = jnp.zeros((HALO, C_WIDTH), F32)

    @pl.when(t > 0)
    def _():
        ext_ref[0:HALO, :] = ext_ref[tm:tm + HALO, :]

    ext_ref[HALO:HALO + tm, :] = _dot(h, w_in_ref[:, 0:C_WIDTH])
    pos = t * tm + lax.broadcasted_iota(jnp.int32, (tm, 1), 0)
    for g, window in enumerate(POOL_SIZES):
        cols = slice(g * C_GROUP, (g + 1) * C_GROUP)
        xc = ext_ref[HALO:HALO + tm, cols]
        acc = xc
        for j in range(1, window):
            acc = acc + ext_ref[HALO - j:HALO - j + tm, cols]
        cnt = jnp.minimum(pos + 1, window).astype(F32)
        diff = (acc / cnt - xc).astype(BF16)
        mixed = _dot(diff, w_grp_ref[g]) * scale_ref[:, cols]
        z = _dot(h, w_in_ref[:, C_WIDTH + g * C_GROUP:C_WIDTH + (g + 1) * C_GROUP])
        y_ref[:, cols] = (mixed * _silu(z)).astype(BF16)
    o_ref[...] = x + _dot(y_ref[...], w_out_ref[...])


def _layer_c(x3, ng, w_in, w_grp, scale, w_out):
    bsz, seq, _ = x3.shape
    tm = LAYER_TILE
    return pl.pallas_call(
        _layer_c_kernel,
        grid=(bsz, seq // tm),
        in_specs=[
            pl.BlockSpec((None, tm, D_MODEL), lambda b, t: (b, t, 0)),
            _resident((1, D_MODEL), lambda b, t: (0, 0)),
            _resident((D_MODEL, 2 * C_WIDTH), lambda b, t: (0, 0)),
            _resident((N_POOL, C_GROUP, C_GROUP), lambda b, t: (0, 0, 0)),
            _resident((1, C_WIDTH), lambda b, t: (0, 0)),
            _resident((C_WIDTH, D_MODEL), lambda b, t: (0, 0)),
        ],
        out_specs=pl.BlockSpec((None, tm, D_MODEL), lambda b, t: (b, t, 0)),
        out_shape=jax.ShapeDtypeStruct((bsz, seq, D_MODEL), F32),
        scratch_shapes=[pltpu.VMEM((tm + HALO, C_WIDTH), F32), pltpu.VMEM((tm, C_WIDTH), BF16)],
        compiler_params=pltpu.CompilerParams(
            dimension_semantics=("arbitrary", "arbitrary"), vmem_limit_bytes=VMEM_LIMIT_BYTES),
        name="layer_c",
    )(x3, ng.reshape(1, D_MODEL), w_in.astype(BF16), w_grp.astype(BF16),
      scale.reshape(1, C_WIDTH), w_out.astype(BF16))


def _attn_proj_kernel(x_ref, ng_ref, freq_ref, wq_ref, wkt_ref, wv_ref, qg_ref, kg_ref,
                      q_ref, kt_ref, v_ref, h_scr, q_a, kt_a, v_a, q_b, kt_b, v_b,
                      *, dilation, n_steps, n_super):
    s = pl.program_id(0)
    rows = q_ref.shape[0]
    n_chunk = SUPER // rows
    per = SUPER // dilation
    n_cols = D_MODEL // LANES
    cur = jnp.minimum(s, n_steps - 1)
    c = cur % n_chunk
    prev = jnp.maximum(s - 1, 0)
    c_prev = prev % n_chunk
    n_prev = (prev // n_chunk) % n_super

    @pl.when(s == 0)
    def _():
        q_b[...] = jnp.zeros(q_b.shape, q_b.dtype)
        kt_b[...] = jnp.zeros(kt_b.shape, kt_b.dtype)
        v_b[...] = jnp.zeros(v_b.shape, v_b.dtype)

    @pl.when((c == 0) & (s < n_steps))
    def _():
        for i in range(SUPER // rows):
            rs = slice(i * rows, (i + 1) * rows)
            hh = _rms_norm(x_ref[rs, :], ng_ref[...])
            for j in range(n_cols):
                h_scr[j, rs, :] = hh[:, j * LANES:(j + 1) * LANES]

    def gather(j):
        if dilation == 1:
            return h_scr[j, pl.ds(pl.multiple_of(c * rows, rows), rows), :]
        run = min(per, rows)
        first = c * (rows // run)
        if per > rows:
            raise NotImplementedError("a residue class longer than the chunk needs a row offset")
        pieces = [h_scr[j, pl.ds(first + rr, run, stride=dilation), :] for rr in range(rows // run)]
        return pieces[0] if len(pieces) == 1 else jnp.concatenate(pieces, axis=0)

    def project(q_raw, kt_raw, v_raw):
        h = jnp.concatenate([gather(j) for j in range(n_cols)], axis=1).astype(BF16)
        q_raw[...] = _dot(h, wq_ref[...])
        kt_raw[...] = _dot_nt(wkt_ref[...], h)
        v_raw[...] = _dot(h, wv_ref[...]).astype(BF16)

    def finish(q_raw, kt_raw, v_raw):
        p = c_prev * rows + lax.broadcasted_iota(jnp.int32, (1, rows), 1)
        residue = lax.shift_right_logical(p, per.bit_length() - 1)
        strided = jnp.bitwise_and(p, per - 1)
        pos = n_prev * SUPER + strided * dilation + residue
        ang = pos.astype(F32) * freq_ref[...]
        cos_t, sin_t = jnp.cos(ang), jnp.sin(ang)

        v_ref[...] = v_raw[...]

        for hd in range(B_HEADS):
            slab = kt_raw[hd * HEAD_DIM:(hd + 1) * HEAD_DIM, :]
            kn = slab * lax.rsqrt(jnp.mean(slab * slab, axis=0, keepdims=True) + EPS) * kg_ref[...]
            x1, x2 = kn[0:ROPE_HALF, :], kn[ROPE_HALF:ROPE_DIM, :]
            rotated = jnp.concatenate(
                [x1 * cos_t - x2 * sin_t, x2 * cos_t + x1 * sin_t, kn[ROPE_DIM:, :]], axis=0)
            kt_ref[hd * HEAD_DIM:(hd + 1) * HEAD_DIM, :] = rotated.astype(BF16)

        rest = HEAD_DIM - ROPE_DIM
        cos_tab = jnp.concatenate([cos_t, cos_t, jnp.ones((rest, rows), F32)], axis=0).T
        sin_tab = jnp.concatenate([-sin_t, sin_t, jnp.zeros((rest, rows), F32)], axis=0).T
        lane = lax.broadcasted_iota(jnp.int32, (rows, HEAD_DIM), 1)
        for hd in range(B_HEADS):
            cols = slice(hd * HEAD_DIM, (hd + 1) * HEAD_DIM)
            qn = _rms_norm(q_raw[:, cols], qg_ref[...])
            partner = jnp.where(lane < ROPE_HALF, pltpu.roll(qn, HEAD_DIM - ROPE_HALF, 1),
                                pltpu.roll(qn, ROPE_HALF, 1))
            q_ref[:, cols] = ((qn * cos_tab + partner * sin_tab) * Q_PRESCALE).astype(BF16)

    @pl.when(s % 2 == 0)
    def _():
        project(q_a, kt_a, v_a)
        finish(q_b, kt_b, v_b)

    @pl.when(s % 2 == 1)
    def _():
        project(q_b, kt_b, v_b)
        finish(q_a, kt_a, v_a)


def _attn_proj(x3, ng, inv_freq, w_in16, w_kt16, q_gain, k_gain, group, dilation):
    bsz, seq, _ = x3.shape
    rows = TOKEN_TILE
    n_super, n_chunk = seq // SUPER, SUPER // rows
    per_batch = n_super * n_chunk
    n_steps = bsz * per_batch
    cur = lambda s: jnp.minimum(s, n_steps - 1)
    prev = lambda s: jnp.maximum(s - 1, 0)
    raw = [pltpu.VMEM((rows, B_WIDTH), F32), pltpu.VMEM((B_WIDTH, rows), F32),
           pltpu.VMEM((rows, B_WIDTH), BF16)]
    return pl.pallas_call(
        functools.partial(_attn_proj_kernel, dilation=dilation, n_steps=n_steps, n_super=n_super),
        grid=(n_steps + 1,),
        in_specs=[
            pl.BlockSpec((None, SUPER, D_MODEL),
                         lambda s: (cur(s) // per_batch, (cur(s) // n_chunk) % n_super, 0)),
            _resident((1, D_MODEL), lambda s: (0, 0)),
            _resident((ROPE_HALF, 1), lambda s: (0, 0)),
            _resident((D_MODEL, B_WIDTH), lambda s: (0, group)),
            _resident((B_WIDTH, D_MODEL), lambda s: (group, 0)),
            _resident((D_MODEL, B_WIDTH), lambda s: (0, 2 * N_B_GROUPS + group)),
            _resident((1, HEAD_DIM), lambda s: (0, 0)),
            _resident((HEAD_DIM, 1), lambda s: (0, 0)),
        ],
        out_specs=[
            pl.BlockSpec((None, rows, B_WIDTH), lambda s: (prev(s) // per_batch, prev(s) % per_batch, 0)),
            pl.BlockSpec((None, B_WIDTH, rows), lambda s: (prev(s) // per_batch, 0, prev(s) % per_batch)),
            pl.BlockSpec((None, rows, B_WIDTH), lambda s: (prev(s) // per_batch, prev(s) % per_batch, 0)),
        ],
        out_shape=[
            jax.ShapeDtypeStruct((bsz, seq, B_WIDTH), BF16),
            jax.ShapeDtypeStruct((bsz, B_WIDTH, seq), BF16),
            jax.ShapeDtypeStruct((bsz, seq, B_WIDTH), BF16),
        ],
        scratch_shapes=[pltpu.VMEM((D_MODEL // LANES, SUPER, LANES), F32)] + raw + raw,
        compiler_params=pltpu.CompilerParams(
            dimension_semantics=("arbitrary",), vmem_limit_bytes=VMEM_LIMIT_BYTES),
        name=f"attn_proj_d{dilation}",
    )(x3, ng.reshape(1, D_MODEL), inv_freq, w_in16, w_kt16, w_in16,
      q_gain.reshape(1, HEAD_DIM), k_gain.reshape(HEAD_DIM, 1))


def _attn_kernel(*refs):
    qkv_refs = refs[:3 * N_B_GROUPS]
    y_ref = refs[3 * N_B_GROUPS]
    scratch = refs[3 * N_B_GROUPS + 1:]
    kt_scrs, v_scrs = scratch[0:N_B_GROUPS], scratch[N_B_GROUPS:2 * N_B_GROUPS]
    o_scr, lse_scr, bias_scr = scratch[2 * N_B_GROUPS:]
    n = pl.program_id(2)

    qi = lax.broadcasted_iota(jnp.int32, (SPAN, 2 * SPAN), 0)
    ki = lax.broadcasted_iota(jnp.int32, (SPAN, 2 * SPAN), 1)
    band = (ki >= qi) & (ki <= qi + SPAN)
    bias_scr[0] = jnp.where(band, 0.0, MASK_VALUE)
    bias_scr[1] = jnp.where(band & (ki >= jnp.where(n > 0, 0, SPAN)), 0.0, MASK_VALUE)

    for g, (_, d) in enumerate(B_PATTERNS):
        q_ref, kt_ref, v_ref = qkv_refs[3 * g:3 * g + 3]
        kt_scr, v_scr = kt_scrs[g], v_scrs[g]
        per = SUPER // d
        seg = SPAN + per

        @pl.when(n == 0)
        def _():
            v_scr[...] = jnp.ones(v_scr.shape, BF16)
            for r in range(d):
                kt_scr[:, r * seg:r * seg + SPAN] = jnp.zeros((kt_scr.shape[0], SPAN), BF16)
                for hh in range(HEADS_PER_STEP):
                    v_scr[r * seg:r * seg + SPAN, 2 * hh * HEAD_DIM:(2 * hh + 1) * HEAD_DIM] = (
                        jnp.zeros((SPAN, HEAD_DIM), BF16))

        @pl.when(n > 0)
        def _():
            for r in range(d):
                kt_scr[:, r * seg:r * seg + SPAN] = kt_scr[:, r * seg + per:r * seg + per + SPAN]
                v_scr[r * seg:r * seg + SPAN, :] = v_scr[r * seg + per:r * seg + per + SPAN, :]

        for r in range(d):
            kt_scr[:, r * seg + SPAN:(r + 1) * seg] = kt_ref[:, r * per:(r + 1) * per]
            for hh in range(HEADS_PER_STEP):
                v_scr[r * seg + SPAN:(r + 1) * seg, 2 * hh * HEAD_DIM:(2 * hh + 1) * HEAD_DIM] = (
                    v_ref[r * per:(r + 1) * per, hh * HEAD_DIM:(hh + 1) * HEAD_DIM])

        for r in range(d):
            for j in range(per // SPAN):
                q_rows = slice(r * per + j * SPAN, r * per + (j + 1) * SPAN)
                keys = slice(r * seg + j * SPAN, r * seg + (j + 2) * SPAN)
                if d < MERGE_DIL:
                    assert d == 1
                    nat = slice(j * SPAN, (j + 1) * SPAN)
                else:
                    fine = d // MERGE_DIL
                    start = (r % MERGE_DIL) * (SUPER // MERGE_DIL) + fine * j * SPAN + r // MERGE_DIL
                    nat = pl.ds(start, SPAN, stride=fine) if fine > 1 else slice(start, start + SPAN)
                for hh in range(HEADS_PER_STEP):
                    cols = slice(hh * HEAD_DIM, (hh + 1) * HEAD_DIM)
                    s = _dot(q_ref[q_rows, cols], kt_scr[cols, keys]) + bias_scr[1 if j == 0 else 0]
                    m = jnp.max(s, axis=-1, keepdims=True)
                    p = jnp.exp2(s - m)
                    pv = _dot(p.astype(BF16), v_scr[keys, 2 * hh * HEAD_DIM:(2 * hh + 2) * HEAD_DIM])
                    denom = pv[:, HEAD_DIM:]
                    o_scr[g, hh, nat, :] = pv[:, :HEAD_DIM] * (1.0 / denom)
                    lse_scr[g, hh, nat, :] = m + jnp.log2(denom)

    rows = 256
    for hh in range(HEADS_PER_STEP):
        for b in range(MERGE_DIL):
            for u0 in range(0, SUPER // MERGE_DIL, rows):
                natural = pl.ds(b + MERGE_DIL * u0, rows, stride=MERGE_DIL)
                ordered = slice(b * (SUPER // MERGE_DIL) + u0, b * (SUPER // MERGE_DIL) + u0 + rows)
                idx = [natural if d < MERGE_DIL else ordered for _, d in B_PATTERNS]
                lses = [lse_scr[g, hh, idx[g], :] for g in range(N_B_GROUPS)]
                m = jnp.maximum(jnp.maximum(lses[0], lses[1]), lses[2])
                es = [jnp.exp2(l - m) for l in lses]
                mixed = (es[0] * o_scr[0, hh, idx[0], :] + es[1] * o_scr[1, hh, idx[1], :]
                         + es[2] * o_scr[2, hh, idx[2], :])
                y_ref[hh, natural, :] = mixed * (1.0 / (es[0] + es[1] + es[2]))


def _attn(qkvs, bsz, seq):
    width = HEADS_PER_STEP * HEAD_DIM
    n_super = seq // SUPER
    in_specs, operands, kt_scr, v_scr = [], [], [], []
    for (q, kt, v), (_, d) in zip(qkvs, B_PATTERNS):
        in_specs += [
            pl.BlockSpec((None, SUPER, width), lambda b, hp, n: (b, n, hp)),
            pl.BlockSpec((None, width, SUPER), lambda b, hp, n: (b, hp, n)),
            pl.BlockSpec((None, SUPER, width), lambda b, hp, n: (b, n, hp)),
        ]
        operands += [q, kt, v]
        kt_scr.append(pltpu.VMEM((width, SUPER + d * SPAN), BF16))
        v_scr.append(pltpu.VMEM((SUPER + d * SPAN, 2 * width), BF16))
    return pl.pallas_call(
        _attn_kernel,
        grid=(bsz, B_HEADS // HEADS_PER_STEP, n_super),
        in_specs=in_specs,
        out_specs=pl.BlockSpec((None, HEADS_PER_STEP, SUPER, HEAD_DIM), lambda b, hp, n: (b, hp, n, 0)),
        out_shape=jax.ShapeDtypeStruct((bsz, B_HEADS, seq, HEAD_DIM), F32),
        scratch_shapes=kt_scr + v_scr + [
            pltpu.VMEM((N_B_GROUPS, HEADS_PER_STEP, SUPER, HEAD_DIM), F32),
            pltpu.VMEM((N_B_GROUPS, HEADS_PER_STEP, SUPER, HEAD_DIM), F32),
            pltpu.VMEM((2, SPAN, 2 * SPAN), F32),
        ],
        compiler_params=pltpu.CompilerParams(
            dimension_semantics=("arbitrary", "arbitrary", "arbitrary"),
            vmem_limit_bytes=VMEM_LIMIT_BYTES),
        name="attn",
    )(*operands)


def _attn_out_kernel(x_ref, ng_ref, wz_ref, o_ref, w_out_ref, out_ref):
    x = x_ref[...]
    h = _rms_norm(x, ng_ref[...]).astype(BF16)
    z = _dot(h, wz_ref[...])
    o = jnp.concatenate([o_ref[hd] for hd in range(B_HEADS)], axis=1)
    y = (o * _silu(z)).astype(BF16)
    out_ref[...] = x + _dot(y, w_out_ref[...])


def _attn_out(x3, ng, w_in16, o, w_out):
    bsz, seq, _ = x3.shape
    tm = TOKEN_TILE
    tile = pl.BlockSpec((None, tm, D_MODEL), lambda b, t: (b, t, 0))
    return pl.pallas_call(
        _attn_out_kernel,
        grid=(bsz, seq // tm),
        in_specs=[
            tile,
            _resident((1, D_MODEL), lambda b, t: (0, 0)),
            _resident((D_MODEL, B_WIDTH), lambda b, t: (0, 3 * N_B_GROUPS)),
            pl.BlockSpec((None, B_HEADS, tm, HEAD_DIM), lambda b, t: (b, 0, t, 0)),
            _resident((B_WIDTH, D_MODEL), lambda b, t: (0, 0)),
        ],
        out_specs=tile,
        out_shape=jax.ShapeDtypeStruct((bsz, seq, D_MODEL), F32),
        compiler_params=pltpu.CompilerParams(
            dimension_semantics=("arbitrary", "arbitrary"), vmem_limit_bytes=VMEM_LIMIT_BYTES),
        name="attn_out",
    )(x3, ng.reshape(1, D_MODEL), w_in16, o, w_out.astype(BF16))


def _layer_b(x3, ng, w_in, q_gain, k_gain, w_out):
    bsz, seq, _ = x3.shape
    w_in16 = w_in.astype(BF16)
    w_kt16 = w_in16[:, N_B_GROUPS * B_WIDTH:2 * N_B_GROUPS * B_WIDTH].T
    inv_freq = jnp.power(jnp.float32(ROPE_THETA), -jnp.arange(ROPE_HALF, dtype=F32) / ROPE_HALF)
    inv_freq = inv_freq.reshape(ROPE_HALF, 1)
    qkvs = []
    for g, (window, dilation) in enumerate(B_PATTERNS):
        assert window // dilation == SPAN
        qkvs.append(_attn_proj(x3, ng, inv_freq, w_in16, w_kt16, q_gain[g], k_gain[g], g, dilation))
    return _attn_out(x3, ng, w_in16, _attn(qkvs, bsz, seq), w_out)


def kernel(x, norm_gain, a_w_in, a_v_gain, a_w_s, a_b_s, a_w_out, b_w_in, b_q_gain, b_k_gain,
           b_w_out, c_w_in, c_w_grp, c_scale, c_w_out):
    bsz, seq, d_model = x.shape
    assert d_model == D_MODEL and seq % SUPER == 0 and SUPER % TOKEN_TILE == 0
    depth = norm_gain.shape[0]
    for i in range(depth):
        kind, j = i % 3, i // 3
        if kind == 0:
            x = _layer_a(x.reshape(bsz * seq, D_MODEL), norm_gain[i], a_w_in[j], a_v_gain[j],
                         a_w_s[j], a_b_s[j], a_w_out[j]).reshape(bsz, seq, D_MODEL)
        elif kind == 1:
            x = _layer_b(x, norm_gain[i], b_w_in[j], b_q_gain[j], b_k_gain[j], b_w_out[j])
        else:
            x = _layer_c(x, norm_gain[i], c_w_in[j], c_w_grp[j], c_scale[j], c_w_out[j])
    return x
```

```python
import functools
import math

import jax
import jax.numpy as jnp
import numpy as np
from jax import lax
from jax.experimental import pallas as pl
from jax.experimental.pallas import tpu as pltpu

D_MODEL = 1024
EPS = 1e-6
CHUNK = 128
A_WIDTH = 2 * D_MODEL
A_GROUPS = 8
A_GROUP_DIM = A_WIDTH // A_GROUPS
HEAD_DIM = 128
B_HEADS = D_MODEL // HEAD_DIM
B_PATTERNS = ((128, 1), (512, 4), (2048, 16))
N_B_GROUPS = len(B_PATTERNS)
B_WIDTH = B_HEADS * HEAD_DIM
ROPE_DIM = HEAD_DIM // 4
ROPE_HALF = ROPE_DIM // 2
ROPE_THETA = 500000.0
Q_PRESCALE = math.log2(math.e) / math.sqrt(HEAD_DIM)
SPAN = 128
SUPER = SPAN * max(d for _, d in B_PATTERNS)
HEADS_PER_STEP = 2
MERGE_DIL = 4
POOL_SIZES = (2, 4, 8, 16)
N_POOL = len(POOL_SIZES)
C_WIDTH = 2 * D_MODEL
C_GROUP = C_WIDTH // N_POOL
HALO = 16

LANES = 128
VMEM_LIMIT_BYTES = 56 * 1024 * 1024

TOKEN_TILE = 512
LAYER_TILE = 1024
MASK_VALUE = -1e30

F32 = jnp.float32
BF16 = jnp.bfloat16


def _resident(shape, index_map):
    return pl.BlockSpec(shape, index_map, pipeline_mode=pl.Buffered(1))


def _rms_norm(x, gain):
    return x * lax.rsqrt(jnp.mean(x * x, axis=-1, keepdims=True) + EPS) * gain


def _dot(a, b):
    return jnp.dot(a, b, preferred_element_type=F32)


def _dot_nt(a, b):
    return lax.dot_general(a, b, (((1,), (1,)), ((), ())), preferred_element_type=F32)


def _silu(z):
    return z * jax.nn.sigmoid(z)


def _cast_specs(jobs, n_steps, flat_step):
    in_specs, out_specs, out_shapes = [], [], []
    for arr, layer in jobs:
        _, rows, cols = arr.shape
        chunk = rows // n_steps
        assert chunk * n_steps == rows and chunk % 16 == 0, (arr.shape, n_steps)
        last = n_steps - 1
        in_specs.append(pl.BlockSpec(
            (None, chunk, cols), lambda *ids, layer=layer: (layer, jnp.minimum(flat_step(*ids), last), 0)))
        out_specs.append(pl.BlockSpec(
            (chunk, cols), lambda *ids: (jnp.minimum(flat_step(*ids), last), 0)))
        out_shapes.append(jax.ShapeDtypeStruct((rows, cols), BF16))
    return in_specs, out_specs, out_shapes


def _with_casts(body, n_in, n_out, n_jobs):
    if n_jobs == 0:
        return body

    def wrapped(*refs):
        ins, cast_in = refs[:n_in], refs[n_in:n_in + n_jobs]
        outs = refs[n_in + n_jobs:n_in + n_jobs + n_out]
        cast_out = refs[n_in + n_jobs + n_out:n_in + 2 * n_jobs + n_out]
        body(*ins, *outs, *refs[n_in + 2 * n_jobs + n_out:])
        for src, dst in zip(cast_in, cast_out):
            dst[...] = src[...].astype(BF16)

    return wrapped


def _layer_a_kernel(x_ref, ng_ref, w_in_ref, vg_ref, ws_ref, bs_ref, w_out_ref,
                    o_ref, vn_ref, y_ref):
    x = x_ref[...]
    h = _rms_norm(x, ng_ref[...]).astype(BF16)
    v = _dot(h, w_in_ref[:, A_WIDTH:2 * A_WIDTH])
    vn_ref[...] = _rms_norm(v, vg_ref[...]).astype(BF16)
    row = lax.broadcasted_iota(jnp.int32, (CHUNK, CHUNK), 0)
    col = lax.broadcasted_iota(jnp.int32, (CHUNK, CHUNK), 1)
    causal = col <= row
    n_chunks = x.shape[0] // CHUNK
    for g in range(A_GROUPS):
        cols = slice(g * A_GROUP_DIM, (g + 1) * A_GROUP_DIM)
        u = _dot(h, w_in_ref[:, cols])
        z = _dot(h, w_in_ref[:, 2 * A_WIDTH + g * A_GROUP_DIM:2 * A_WIDTH + (g + 1) * A_GROUP_DIM])
        ws = jnp.where(causal, ws_ref[g], 0.0).astype(BF16)
        bias = bs_ref[:, g:g + 1]
        mixed = jnp.concatenate(
            [_dot(ws, vn_ref[c * CHUNK:(c + 1) * CHUNK, cols]) + bias for c in range(n_chunks)],
            axis=0)
        y_ref[:, cols] = (u * mixed * _silu(z)).astype(BF16)
    o_ref[...] = x + _dot(y_ref[...], w_out_ref[...])


def _layer_a(x2, ng, w_in16, v_gain, w_s, b_s, w_out16, cast_jobs=()):
    tokens = x2.shape[0]
    tm = LAYER_TILE
    n_steps = tokens // tm
    cast_in, cast_out, cast_shapes = _cast_specs(cast_jobs, n_steps, lambda i: i)
    out, *casted = pl.pallas_call(
        _with_casts(_layer_a_kernel, 7, 1, len(cast_jobs)),
        grid=(n_steps,),
        in_specs=[
            pl.BlockSpec((tm, D_MODEL), lambda i: (i, 0)),
            _resident((1, D_MODEL), lambda i: (0, 0)),
            _resident((D_MODEL, 3 * A_WIDTH), lambda i: (0, 0)),
            _resident((1, A_WIDTH), lambda i: (0, 0)),
            _resident((A_GROUPS, CHUNK, CHUNK), lambda i: (0, 0, 0)),
            _resident((CHUNK, A_GROUPS), lambda i: (0, 0)),
            _resident((A_WIDTH, D_MODEL), lambda i: (0, 0)),
        ] + cast_in,
        out_specs=[pl.BlockSpec((tm, D_MODEL), lambda i: (i, 0))] + cast_out,
        out_shape=[jax.ShapeDtypeStruct((tokens, D_MODEL), F32)] + cast_shapes,
        scratch_shapes=[pltpu.VMEM((tm, A_WIDTH), BF16), pltpu.VMEM((tm, A_WIDTH), BF16)],
        compiler_params=pltpu.CompilerParams(
            dimension_semantics=("arbitrary",), vmem_limit_bytes=VMEM_LIMIT_BYTES),
        name="layer_a",
    )(x2, ng.reshape(1, D_MODEL), w_in16, v_gain.reshape(1, A_WIDTH), w_s, b_s.T, w_out16,
      *[arr for arr, _ in cast_jobs])
    return out, casted


def _layer_c_kernel(x_ref, ng_ref, w_in_ref, w_grp_ref, scale_ref, w_out_ref,
                    o_ref, ext_ref, y_ref):
    t = pl.program_id(1)
    tm = x_ref.shape[0]
    x = x_ref[...]
    h = _rms_norm(x, ng_ref[...]).astype(BF16)

    @pl.when(t == 0)
    def _():
        ext_ref[0:HALO, :] = jnp.zeros((HALO, C_WIDTH), F32)

    @pl.when(t > 0)
    def _():
        ext_ref[0:HALO, :] = ext_ref[tm:tm + HALO, :]

    ext_ref[HALO:HALO + tm, :] = _dot(h, w_in_ref[:, 0:C_WIDTH])
    pos = t * tm + lax.broadcasted_iota(jnp.int32, (tm, 1), 0)
    for g, window in enumerate(POOL_SIZES):
        cols = slice(g * C_GROUP, (g + 1) * C_GROUP)
        xc = ext_ref[HALO:HALO + tm, cols]
        acc = xc
        for j in range(1, window):
            acc = acc + ext_ref[HALO - j:HALO - j + tm, cols]
        cnt = jnp.minimum(pos + 1, window).astype(F32)
        diff = (acc / cnt - xc).astype(BF16)
        mixed = _dot(diff, w_grp_ref[g]) * scale_ref[:, cols]
        z = _dot(h, w_in_ref[:, C_WIDTH + g * C_GROUP:C_WIDTH + (g + 1) * C_GROUP])
        y_ref[:, cols] = (mixed * _silu(z)).astype(BF16)
    o_ref[...] = x + _dot(y_ref[...], w_out_ref[...])


def _layer_c(x3, ng, w_in16, w_grp16, scale, w_out16):
    bsz, seq, _ = x3.shape
    tm = LAYER_TILE
    return pl.pallas_call(
        _layer_c_kernel,
        grid=(bsz, seq // tm),
        in_specs=[
            pl.BlockSpec((None, tm, D_MODEL), lambda b, t: (b, t, 0)),
            _resident((1, D_MODEL), lambda b, t: (0, 0)),
            _resident((D_MODEL, 2 * C_WIDTH), lambda b, t: (0, 0)),
            _resident((N_POOL, C_GROUP, C_GROUP), lambda b, t: (0, 0, 0)),
            _resident((1, C_WIDTH), lambda b, t: (0, 0)),
            _resident((C_WIDTH, D_MODEL), lambda b, t: (0, 0)),
        ],
        out_specs=pl.BlockSpec((None, tm, D_MODEL), lambda b, t: (b, t, 0)),
        out_shape=jax.ShapeDtypeStruct((bsz, seq, D_MODEL), F32),
        scratch_shapes=[pltpu.VMEM((tm + HALO, C_WIDTH), F32), pltpu.VMEM((tm, C_WIDTH), BF16)],
        compiler_params=pltpu.CompilerParams(
            dimension_semantics=("arbitrary", "arbitrary"), vmem_limit_bytes=VMEM_LIMIT_BYTES),
        name="layer_c",
    )(x3, ng.reshape(1, D_MODEL), w_in16, w_grp16, scale.reshape(1, C_WIDTH), w_out16)


def _attn_proj_kernel(x_ref, ng_ref, freq_ref, wq_ref, wkt_ref, wv_ref, qg_ref, kg_ref,
                      q_ref, kt_ref, v_ref, h_scr, q_a, kt_a, v_a, q_b, kt_b, v_b,
                      *, dilation, n_steps, n_super):
    s = pl.program_id(0)
    rows = q_ref.shape[0]
    n_chunk = SUPER // rows
    per = SUPER // dilation
    n_cols = D_MODEL // LANES
    cur = jnp.minimum(s, n_steps - 1)
    c = cur % n_chunk
    prev = jnp.maximum(s - 1, 0)
    c_prev = prev % n_chunk
    n_prev = (prev // n_chunk) % n_super

    @pl.when(s == 0)
    def _():
        q_b[...] = jnp.zeros(q_b.shape, q_b.dtype)
        kt_b[...] = jnp.zeros(kt_b.shape, kt_b.dtype)
        v_b[...] = jnp.zeros(v_b.shape, v_b.dtype)

    @pl.when((c == 0) & (s < n_steps))
    def _():
        for i in range(SUPER // rows):
            rs = slice(i * rows, (i + 1) * rows)
            hh = _rms_norm(x_ref[rs, :], ng_ref[...])
            for j in range(n_cols):
                h_scr[j, rs, :] = hh[:, j * LANES:(j + 1) * LANES]

    def gather(j):
        if dilation == 1:
            return h_scr[j, pl.ds(pl.multiple_of(c * rows, rows), rows), :]
        run = min(per, rows)
        first = c * (rows // run)
        if per > rows:
            raise NotImplementedError("a residue class longer than the chunk needs a row offset")
        pieces = [h_scr[j, pl.ds(first + rr, run, stride=dilation), :] for rr in range(rows // run)]
        return pieces[0] if len(pieces) == 1 else jnp.concatenate(pieces, axis=0)

    def project(q_raw, kt_raw, v_raw):
        h = jnp.concatenate([gather(j) for j in range(n_cols)], axis=1).astype(BF16)
        q_raw[...] = _dot(h, wq_ref[...])
        kt_raw[...] = _dot_nt(wkt_ref[...], h)
        v_raw[...] = _dot(h, wv_ref[...]).astype(BF16)

    def finish(q_raw, kt_raw, v_raw):
        p = c_prev * rows + lax.broadcasted_iota(jnp.int32, (1, rows), 1)
        residue = lax.shift_right_logical(p, per.bit_length() - 1)
        strided = jnp.bitwise_and(p, per - 1)
        pos = n_prev * SUPER + strided * dilation + residue
        ang = pos.astype(F32) * freq_ref[...]
        cos_t, sin_t = jnp.cos(ang), jnp.sin(ang)

        v_ref[...] = v_raw[...]

        for hd in range(B_HEADS):
            slab = kt_raw[hd * HEAD_DIM:(hd + 1) * HEAD_DIM, :]
            kn = slab * lax.rsqrt(jnp.mean(slab * slab, axis=0, keepdims=True) + EPS) * kg_ref[...]
            x1, x2 = kn[0:ROPE_HALF, :], kn[ROPE_HALF:ROPE_DIM, :]
            rotated = jnp.concatenate(
                [x1 * cos_t - x2 * sin_t, x2 * cos_t + x1 * sin_t, kn[ROPE_DIM:, :]], axis=0)
            kt_ref[hd * HEAD_DIM:(hd + 1) * HEAD_DIM, :] = rotated.astype(BF16)

        rest = HEAD_DIM - ROPE_DIM
        cos_tab = jnp.concatenate([cos_t, cos_t, jnp.ones((rest, rows), F32)], axis=0).T
        sin_tab = jnp.concatenate([-sin_t, sin_t, jnp.zeros((rest, rows), F32)], axis=0).T
        lane = lax.broadcasted_iota(jnp.int32, (rows, HEAD_DIM), 1)
        for hd in range(B_HEADS):
            cols = slice(hd * HEAD_DIM, (hd + 1) * HEAD_DIM)
            qn = _rms_norm(q_raw[:, cols], qg_ref[...])
            partner = jnp.where(lane < ROPE_HALF, pltpu.roll(qn, HEAD_DIM - ROPE_HALF, 1),
                                pltpu.roll(qn, ROPE_HALF, 1))
            q_ref[:, cols] = ((qn * cos_tab + partner * sin_tab) * Q_PRESCALE).astype(BF16)

    @pl.when(s % 2 == 0)
    def _():
        project(q_a, kt_a, v_a)
        finish(q_b, kt_b, v_b)

    @pl.when(s % 2 == 1)
    def _():
        project(q_b, kt_b, v_b)
        finish(q_a, kt_a, v_a)


def _attn_proj(x3, ng, inv_freq, w_in16, w_kt16, q_gain, k_gain, group, dilation, cast_jobs=()):
    bsz, seq, _ = x3.shape
    rows = TOKEN_TILE
    n_super, n_chunk = seq // SUPER, SUPER // rows
    per_batch = n_super * n_chunk
    n_steps = bsz * per_batch
    cur = lambda s: jnp.minimum(s, n_steps - 1)
    prev = lambda s: jnp.maximum(s - 1, 0)
    raw = [pltpu.VMEM((rows, B_WIDTH), F32), pltpu.VMEM((B_WIDTH, rows), F32),
           pltpu.VMEM((rows, B_WIDTH), BF16)]
    cast_in, cast_out, cast_shapes = _cast_specs(cast_jobs, n_steps, lambda s: s)
    body = functools.partial(_attn_proj_kernel, dilation=dilation, n_steps=n_steps, n_super=n_super)
    q, kt, v, *casted = pl.pallas_call(
        _with_casts(body, 8, 3, len(cast_jobs)),
        grid=(n_steps + 1,),
        in_specs=[
            pl.BlockSpec((None, SUPER, D_MODEL),
                         lambda s: (cur(s) // per_batch, (cur(s) // n_chunk) % n_super, 0)),
            _resident((1, D_MODEL), lambda s: (0, 0)),
            _resident((ROPE_HALF, 1), lambda s: (0, 0)),
            _resident((D_MODEL, B_WIDTH), lambda s: (0, group)),
            _resident((B_WIDTH, D_MODEL), lambda s: (group, 0)),
            _resident((D_MODEL, B_WIDTH), lambda s: (0, 2 * N_B_GROUPS + group)),
            _resident((1, HEAD_DIM), lambda s: (0, 0)),
            _resident((HEAD_DIM, 1), lambda s: (0, 0)),
        ] + cast_in,
        out_specs=[
            pl.BlockSpec((None, rows, B_WIDTH), lambda s: (prev(s) // per_batch, prev(s) % per_batch, 0)),
            pl.BlockSpec((None, B_WIDTH, rows), lambda s: (prev(s) // per_batch, 0, prev(s) % per_batch)),
            pl.BlockSpec((None, rows, B_WIDTH), lambda s: (prev(s) // per_batch, prev(s) % per_batch, 0)),
        ] + cast_out,
        out_shape=[
            jax.ShapeDtypeStruct((bsz, seq, B_WIDTH), BF16),
            jax.ShapeDtypeStruct((bsz, B_WIDTH, seq), BF16),
            jax.ShapeDtypeStruct((bsz, seq, B_WIDTH), BF16),
        ] + cast_shapes,
        scratch_shapes=[pltpu.VMEM((D_MODEL // LANES, SUPER, LANES), F32)] + raw + raw,
        compiler_params=pltpu.CompilerParams(
            dimension_semantics=("arbitrary",), vmem_limit_bytes=VMEM_LIMIT_BYTES),
        name=f"attn_proj_d{dilation}",
    )(x3, ng.reshape(1, D_MODEL), inv_freq, w_in16, w_kt16, w_in16,
      q_gain.reshape(1, HEAD_DIM), k_gain.reshape(HEAD_DIM, 1), *[arr for arr, _ in cast_jobs])
    return (q, kt, v), casted


def _attn_kernel(*refs):
    qkv_refs = refs[:3 * N_B_GROUPS]
    y_ref = refs[3 * N_B_GROUPS]
    scratch = refs[3 * N_B_GROUPS + 1:]
    kt_scrs, v_scrs = scratch[0:N_B_GROUPS], scratch[N_B_GROUPS:2 * N_B_GROUPS]
    o_scr, lse_scr, bias_scr = scratch[2 * N_B_GROUPS:]
    n = pl.program_id(2)

    qi = lax.broadcasted_iota(jnp.int32, (SPAN, 2 * SPAN), 0)
    ki = lax.broadcasted_iota(jnp.int32, (SPAN, 2 * SPAN), 1)
    band = (ki >= qi) & (ki <= qi + SPAN)
    bias_scr[0] = jnp.where(band, 0.0, MASK_VALUE)
    bias_scr[1] = jnp.where(band & (ki >= jnp.where(n > 0, 0, SPAN)), 0.0, MASK_VALUE)

    for g, (_, d) in enumerate(B_PATTERNS):
        q_ref, kt_ref, v_ref = qkv_refs[3 * g:3 * g + 3]
        kt_scr, v_scr = kt_scrs[g], v_scrs[g]
        per = SUPER // d
        seg = SPAN + per

        @pl.when(n == 0)
        def _():
            v_scr[...] = jnp.ones(v_scr.shape, BF16)
            for r in range(d):
                kt_scr[:, r * seg:r * seg + SPAN] = jnp.zeros((kt_scr.shape[0], SPAN), BF16)
                for hh in range(HEADS_PER_STEP):
                    v_scr[r * seg:r * seg + SPAN, 2 * hh * HEAD_DIM:(2 * hh + 1) * HEAD_DIM] = (
                        jnp.zeros((SPAN, HEAD_DIM), BF16))

        @pl.when(n > 0)
        def _():
            for r in range(d):
                kt_scr[:, r * seg:r * seg + SPAN] = kt_scr[:, r * seg + per:r * seg + per + SPAN]
                v_scr[r * seg:r * seg + SPAN, :] = v_scr[r * seg + per:r * seg + per + SPAN, :]

        for r in range(d):
            kt_scr[:, r * seg + SPAN:(r + 1) * seg] = kt_ref[:, r * per:(r + 1) * per]
            for hh in range(HEADS_PER_STEP):
                v_scr[r * seg + SPAN:(r + 1) * seg, 2 * hh * HEAD_DIM:(2 * hh + 1) * HEAD_DIM] = (
                    v_ref[r * per:(r + 1) * per, hh * HEAD_DIM:(hh + 1) * HEAD_DIM])

        for r in range(d):
            for j in range(per // SPAN):
                q_rows = slice(r * per + j * SPAN, r * per + (j + 1) * SPAN)
                keys = slice(r * seg + j * SPAN, r * seg + (j + 2) * SPAN)
                if d < MERGE_DIL:
                    assert d == 1
                    nat = slice(j * SPAN, (j + 1) * SPAN)
                else:
                    fine = d // MERGE_DIL
                    start = (r % MERGE_DIL) * (SUPER // MERGE_DIL) + fine * j * SPAN + r // MERGE_DIL
                    nat = pl.ds(start, SPAN, stride=fine) if fine > 1 else slice(start, start + SPAN)
                for hh in range(HEADS_PER_STEP):
                    cols = slice(hh * HEAD_DIM, (hh + 1) * HEAD_DIM)
                    s = _dot(q_ref[q_rows, cols], kt_scr[cols, keys]) + bias_scr[1 if j == 0 else 0]
                    m = jnp.max(s, axis=-1, keepdims=True)
                    p = jnp.exp2(s - m)
                    pv = _dot(p.astype(BF16), v_scr[keys, 2 * hh * HEAD_DIM:(2 * hh + 2) * HEAD_DIM])
                    denom = pv[:, HEAD_DIM:]
                    o_scr[g, hh, nat, :] = pv[:, :HEAD_DIM] * (1.0 / denom)
                    lse_scr[g, hh, nat, :] = m + jnp.log2(denom)

    rows = 256
    for hh in range(HEADS_PER_STEP):
        for b in range(MERGE_DIL):
            for u0 in range(0, SUPER // MERGE_DIL, rows):
                natural = pl.ds(b + MERGE_DIL * u0, rows, stride=MERGE_DIL)
                ordered = slice(b * (SUPER // MERGE_DIL) + u0, b * (SUPER // MERGE_DIL) + u0 + rows)
                idx = [natural if d < MERGE_DIL else ordered for _, d in B_PATTERNS]
                lses = [lse_scr[g, hh, idx[g], :] for g in range(N_B_GROUPS)]
                m = jnp.maximum(jnp.maximum(lses[0], lses[1]), lses[2])
                es = [jnp.exp2(l - m) for l in lses]
                mixed = (es[0] * o_scr[0, hh, idx[0], :] + es[1] * o_scr[1, hh, idx[1], :]
                         + es[2] * o_scr[2, hh, idx[2], :])
                y_ref[hh, natural, :] = mixed * (1.0 / (es[0] + es[1] + es[2]))


def _attn(qkvs, bsz, seq, cast_jobs=()):
    width = HEADS_PER_STEP * HEAD_DIM
    n_super = seq // SUPER
    in_specs, operands, kt_scr, v_scr = [], [], [], []
    for (q, kt, v), (_, d) in zip(qkvs, B_PATTERNS):
        in_specs += [
            pl.BlockSpec((None, SUPER, width), lambda b, hp, n: (b, n, hp)),
            pl.BlockSpec((None, width, SUPER), lambda b, hp, n: (b, hp, n)),
            pl.BlockSpec((None, SUPER, width), lambda b, hp, n: (b, n, hp)),
        ]
        operands += [q, kt, v]
        kt_scr.append(pltpu.VMEM((width, SUPER + d * SPAN), BF16))
        v_scr.append(pltpu.VMEM((SUPER + d * SPAN, 2 * width), BF16))
    n_pairs = B_HEADS // HEADS_PER_STEP
    cast_in, cast_out, cast_shapes = _cast_specs(
        cast_jobs, bsz * n_pairs * n_super, lambda b, hp, n: (b * n_pairs + hp) * n_super + n)
    out, *casted = pl.pallas_call(
        _with_casts(_attn_kernel, len(operands), 1, len(cast_jobs)),
        grid=(bsz, n_pairs, n_super),
        in_specs=in_specs + cast_in,
        out_specs=[pl.BlockSpec((None, HEADS_PER_STEP, SUPER, HEAD_DIM),
                                lambda b, hp, n: (b, hp, n, 0))] + cast_out,
        out_shape=[jax.ShapeDtypeStruct((bsz, B_HEADS, seq, HEAD_DIM), F32)] + cast_shapes,
        scratch_shapes=kt_scr + v_scr + [
            pltpu.VMEM((N_B_GROUPS, HEADS_PER_STEP, SUPER, HEAD_DIM), F32),
            pltpu.VMEM((N_B_GROUPS, HEADS_PER_STEP, SUPER, HEAD_DIM), F32),
            pltpu.VMEM((2, SPAN, 2 * SPAN), F32),
        ],
        compiler_params=pltpu.CompilerParams(
            dimension_semantics=("arbitrary", "arbitrary", "arbitrary"),
            vmem_limit_bytes=VMEM_LIMIT_BYTES),
        name="attn",
    )(*operands, *[arr for arr, _ in cast_jobs])
    return out, casted


def _attn_out_kernel(x_ref, ng_ref, wz_ref, o_ref, w_out_ref, out_ref):
    x = x_ref[...]
    h = _rms_norm(x, ng_ref[...]).astype(BF16)
    z = _dot(h, wz_ref[...])
    o = jnp.concatenate([o_ref[hd] for hd in range(B_HEADS)], axis=1)
    y = (o * _silu(z)).astype(BF16)
    out_ref[...] = x + _dot(y, w_out_ref[...])


def _attn_out(x3, ng, w_in16, o, w_out16):
    bsz, seq, _ = x3.shape
    tm = TOKEN_TILE
    tile = pl.BlockSpec((None, tm, D_MODEL), lambda b, t: (b, t, 0))
    return pl.pallas_call(
        _attn_out_kernel,
        grid=(bsz, seq // tm),
        in_specs=[
            tile,
            _resident((1, D_MODEL), lambda b, t: (0, 0)),
            _resident((D_MODEL, B_WIDTH), lambda b, t: (0, 3 * N_B_GROUPS)),
            pl.BlockSpec((None, B_HEADS, tm, HEAD_DIM), lambda b, t: (b, 0, t, 0)),
            _resident((B_WIDTH, D_MODEL), lambda b, t: (0, 0)),
        ],
        out_specs=tile,
        out_shape=jax.ShapeDtypeStruct((bsz, seq, D_MODEL), F32),
        compiler_params=pltpu.CompilerParams(
            dimension_semantics=("arbitrary", "arbitrary"), vmem_limit_bytes=VMEM_LIMIT_BYTES),
        name="attn_out",
    )(x3, ng.reshape(1, D_MODEL), w_in16, o, w_out16)


def _layer_b(x3, ng, w_in16, q_gain, k_gain, w_out16, proj_cast_jobs, attn_cast_jobs):
    bsz, seq, _ = x3.shape
    w_kt16 = w_in16[:, N_B_GROUPS * B_WIDTH:2 * N_B_GROUPS * B_WIDTH].T
    inv_freq = jnp.power(jnp.float32(ROPE_THETA), -jnp.arange(ROPE_HALF, dtype=F32) / ROPE_HALF)
    inv_freq = inv_freq.reshape(ROPE_HALF, 1)
    qkvs, casted = [], []
    for g, (window, dilation) in enumerate(B_PATTERNS):
        assert window // dilation == SPAN
        qkv, done = _attn_proj(x3, ng, inv_freq, w_in16, w_kt16, q_gain[g], k_gain[g], g, dilation,
                               proj_cast_jobs[g])
        qkvs.append(qkv)
        casted += done
    o, done = _attn(qkvs, bsz, seq, attn_cast_jobs)
    return _attn_out(x3, ng, w_in16, o, w_out16), casted + done


def kernel(x, norm_gain, a_w_in, a_v_gain, a_w_s, a_b_s, a_w_out, b_w_in, b_q_gain, b_k_gain,
           b_w_out, c_w_in, c_w_grp, c_scale, c_w_out):
    bsz, seq, d_model = x.shape
    assert d_model == D_MODEL and seq % SUPER == 0 and SUPER % TOKEN_TILE == 0
    assert norm_gain.shape[0] == 4 and a_w_in.shape[0] == 2
    c_w_grp2 = c_w_grp.reshape(c_w_grp.shape[0], N_POOL * C_GROUP, C_GROUP)

    x2, (b_w_in16, b_w_out16) = _layer_a(
        x.reshape(bsz * seq, D_MODEL), norm_gain[0], a_w_in[0].astype(BF16), a_v_gain[0], a_w_s[0],
        a_b_s[0], a_w_out[0].astype(BF16), cast_jobs=[(b_w_in, 0), (b_w_out, 0)])
    x, (c_w_in16, c_w_grp16, c_w_out16, a_w_in16, a_w_out16) = _layer_b(
        x2.reshape(bsz, seq, D_MODEL), norm_gain[1], b_w_in16, b_q_gain[0], b_k_gain[0], b_w_out16,
        proj_cast_jobs=[[(c_w_in, 0)], [(c_w_grp2, 0)], [(c_w_out, 0)]],
        attn_cast_jobs=[(a_w_in, 1), (a_w_out, 1)])
    x = _layer_c(x, norm_gain[2], c_w_in16, c_w_grp16.reshape(N_POOL, C_GROUP, C_GROUP), c_scale[0],
                 c_w_out16)
    x2, _ = _layer_a(x.reshape(bsz * seq, D_MODEL), norm_gain[3], a_w_in16, a_v_gain[1], a_w_s[1],
                     a_b_s[1], a_w_out16)
    return x2.reshape(bsz, seq, D_MODEL)
```

```python
import functools
import math

import jax
import jax.numpy as jnp
import numpy as np
from jax import lax
from jax.experimental import pallas as pl
from jax.experimental.pallas import tpu as pltpu

D_MODEL = 1024
EPS = 1e-6
CHUNK = 128
A_WIDTH = 2 * D_MODEL
A_GROUPS = 8
A_GROUP_DIM = A_WIDTH // A_GROUPS
HEAD_DIM = 128
B_HEADS = D_MODEL // HEAD_DIM
B_PATTERNS = ((128, 1), (512, 4), (2048, 16))
N_B_GROUPS = len(B_PATTERNS)
B_WIDTH = B_HEADS * HEAD_DIM
ROPE_DIM = HEAD_DIM // 4
ROPE_HALF = ROPE_DIM // 2
ROPE_THETA = 500000.0
Q_PRESCALE = math.log2(math.e) / math.sqrt(HEAD_DIM)
SPAN = 128
SUPER = SPAN * max(d for _, d in B_PATTERNS)
HEADS_PER_STEP = 2
MERGE_DIL = 4
POOL_SIZES = (2, 4, 8, 16)
N_POOL = len(POOL_SIZES)
C_WIDTH = 2 * D_MODEL
C_GROUP = C_WIDTH // N_POOL
HALO = 16

LANES = 128
VMEM_LIMIT_BYTES = 60 * 1024 * 1024

TOKEN_TILE = 512
LAYER_TILE = 1024
MASK_VALUE = -1e30

F32 = jnp.float32
BF16 = jnp.bfloat16


def _resident(shape, index_map):
    return pl.BlockSpec(shape, index_map, pipeline_mode=pl.Buffered(1))


def _rms_norm(x, gain):
    return x * lax.rsqrt(jnp.mean(x * x, axis=-1, keepdims=True) + EPS) * gain


def _dot(a, b):
    return jnp.dot(a, b, preferred_element_type=F32)


def _dot_nt(a, b):
    return lax.dot_general(a, b, (((1,), (1,)), ((), ())), preferred_element_type=F32)


def _silu(z):
    return z * jax.nn.sigmoid(z)


def _cast_specs(jobs, n_steps, flat_step):
    in_specs, out_specs, out_shapes = [], [], []
    for arr, layer, col_range in jobs:
        _, rows, cols = arr.shape
        if col_range is None:
            n_chunks = n_steps
            chunk = rows // n_chunks
            assert chunk * n_chunks == rows and chunk % 16 == 0, (arr.shape, n_steps)
            last = n_chunks - 1
            in_specs.append(pl.BlockSpec(
                (None, chunk, cols),
                lambda *ids, layer=layer, last=last: (layer, jnp.minimum(flat_step(*ids), last), 0)))
            out_specs.append(pl.BlockSpec(
                (chunk, cols), lambda *ids, last=last: (jnp.minimum(flat_step(*ids), last), 0)))
            out_shapes.append(jax.ShapeDtypeStruct((rows, cols), BF16))
        else:
            start, width = col_range
            n_chunks = rows // LANES
            assert n_chunks <= n_steps and start % width == 0, (arr.shape, n_steps, col_range)
            last = n_chunks - 1
            in_specs.append(pl.BlockSpec(
                (None, LANES, width),
                lambda *ids, layer=layer, last=last, blk=start // width: (
                    layer, jnp.minimum(flat_step(*ids), last), blk)))
            out_specs.append(pl.BlockSpec(
                (width, LANES), lambda *ids, last=last: (0, jnp.minimum(flat_step(*ids), last))))
            out_shapes.append(jax.ShapeDtypeStruct((width, rows), BF16))
    return in_specs, out_specs, out_shapes


def _with_casts(body, n_in, n_out, n_jobs):
    if n_jobs == 0:
        return body

    def wrapped(*refs):
        ins, cast_in = refs[:n_in], refs[n_in:n_in + n_jobs]
        outs = refs[n_in + n_jobs:n_in + n_jobs + n_out]
        cast_out = refs[n_in + n_jobs + n_out:n_in + 2 * n_jobs + n_out]
        body(*ins, *outs, *refs[n_in + 2 * n_jobs + n_out:])
        for src, dst in zip(cast_in, cast_out):
            transposed = src.shape != dst.shape
            dst[...] = (src[...].T if transposed else src[...]).astype(BF16)

    return wrapped


def _layer_a_kernel(x_ref, ng_ref, w_in_ref, vg_ref, ws_ref, bs_ref, w_out_ref,
                    o_ref, vn_ref, y_ref):
    x = x_ref[...]
    h = _rms_norm(x, ng_ref[...]).astype(BF16)
    v = _dot(h, w_in_ref[:, A_WIDTH:2 * A_WIDTH])
    vn_ref[...] = _rms_norm(v, vg_ref[...]).astype(BF16)
    row = lax.broadcasted_iota(jnp.int32, (CHUNK, CHUNK), 0)
    col = lax.broadcasted_iota(jnp.int32, (CHUNK, CHUNK), 1)
    causal = col <= row
    n_chunks = x.shape[0] // CHUNK
    for g in range(A_GROUPS):
        cols = slice(g * A_GROUP_DIM, (g + 1) * A_GROUP_DIM)
        u = _dot(h, w_in_ref[:, cols])
        z = _dot(h, w_in_ref[:, 2 * A_WIDTH + g * A_GROUP_DIM:2 * A_WIDTH + (g + 1) * A_GROUP_DIM])
        ws = jnp.where(causal, ws_ref[g], 0.0).astype(BF16)
        bias = bs_ref[:, g:g + 1]
        mixed = jnp.concatenate(
            [_dot(ws, vn_ref[c * CHUNK:(c + 1) * CHUNK, cols]) + bias for c in range(n_chunks)],
            axis=0)
        y_ref[:, cols] = (u * mixed * _silu(z)).astype(BF16)
    o_ref[...] = x + _dot(y_ref[...], w_out_ref[...])


def _layer_a(x2, ng, w_in16, v_gain, w_s, b_s, w_out16, cast_jobs=()):
    tokens = x2.shape[0]
    tm = LAYER_TILE
    n_steps = tokens // tm
    cast_in, cast_out, cast_shapes = _cast_specs(cast_jobs, n_steps, lambda i: i)
    out, *casted = pl.pallas_call(
        _with_casts(_layer_a_kernel, 7, 1, len(cast_jobs)),
        grid=(n_steps,),
        in_specs=[
            pl.BlockSpec((tm, D_MODEL), lambda i: (i, 0)),
            _resident((1, D_MODEL), lambda i: (0, 0)),
            _resident((D_MODEL, 3 * A_WIDTH), lambda i: (0, 0)),
            _resident((1, A_WIDTH), lambda i: (0, 0)),
            _resident((A_GROUPS, CHUNK, CHUNK), lambda i: (0, 0, 0)),
            _resident((CHUNK, A_GROUPS), lambda i: (0, 0)),
            _resident((A_WIDTH, D_MODEL), lambda i: (0, 0)),
        ] + cast_in,
        out_specs=[pl.BlockSpec((tm, D_MODEL), lambda i: (i, 0))] + cast_out,
        out_shape=[jax.ShapeDtypeStruct((tokens, D_MODEL), F32)] + cast_shapes,
        scratch_shapes=[pltpu.VMEM((tm, A_WIDTH), BF16), pltpu.VMEM((tm, A_WIDTH), BF16)],
        compiler_params=pltpu.CompilerParams(
            dimension_semantics=("arbitrary",), vmem_limit_bytes=VMEM_LIMIT_BYTES),
        name="layer_a",
    )(x2, ng.reshape(1, D_MODEL), w_in16, v_gain.reshape(1, A_WIDTH), w_s, b_s.T, w_out16,
      *[job[0] for job in cast_jobs])
    return out, casted


def _layer_c_kernel(x_ref, ng_ref, w_in_ref, w_grp_ref, scale_ref, w_out_ref,
                    o_ref, ext_ref, y_ref):
    t = pl.program_id(1)
    tm = x_ref.shape[0]
    x = x_ref[...]
    h = _rms_norm(x, ng_ref[...]).astype(BF16)

    @pl.when(t == 0)
    def _():
        ext_ref[0:HALO, :] = jnp.zeros((HALO, C_WIDTH), F32)

    @pl.when(t > 0)
    def _():
        ext_ref[0:HALO, :] = ext_ref[tm:tm + HALO, :]

    ext_ref[HALO:HALO + tm, :] = _dot(h, w_in_ref[:, 0:C_WIDTH])
    pos = t * tm + lax.broadcasted_iota(jnp.int32, (tm, 1), 0)
    for g, window in enumerate(POOL_SIZES):
        cols = slice(g * C_GROUP, (g + 1) * C_GROUP)
        xc = ext_ref[HALO:HALO + tm, cols]
        acc = xc
        for j in range(1, window):
            acc = acc + ext_ref[HALO - j:HALO - j + tm, cols]
        cnt = jnp.minimum(pos + 1, window).astype(F32)
        diff = (acc / cnt - xc).astype(BF16)
        mixed = _dot(diff, w_grp_ref[g]) * scale_ref[:, cols]
        z = _dot(h, w_in_ref[:, C_WIDTH + g * C_GROUP:C_WIDTH + (g + 1) * C_GROUP])
        y_ref[:, cols] = (mixed * _silu(z)).astype(BF16)
    o_ref[...] = x + _dot(y_ref[...], w_out_ref[...])


def _layer_c(x3, ng, w_in16, w_grp16, scale, w_out16):
    bsz, seq, _ = x3.shape
    tm = LAYER_TILE
    return pl.pallas_call(
        _layer_c_kernel,
        grid=(bsz, seq // tm),
        in_specs=[
            pl.BlockSpec((None, tm, D_MODEL), lambda b, t: (b, t, 0)),
            _resident((1, D_MODEL), lambda b, t: (0, 0)),
            _resident((D_MODEL, 2 * C_WIDTH), lambda b, t: (0, 0)),
            _resident((N_POOL, C_GROUP, C_GROUP), lambda b, t: (0, 0, 0)),
            _resident((1, C_WIDTH), lambda b, t: (0, 0)),
            _resident((C_WIDTH, D_MODEL), lambda b, t: (0, 0)),
        ],
        out_specs=pl.BlockSpec((None, tm, D_MODEL), lambda b, t: (b, t, 0)),
        out_shape=jax.ShapeDtypeStruct((bsz, seq, D_MODEL), F32),
        scratch_shapes=[pltpu.VMEM((tm + HALO, C_WIDTH), F32), pltpu.VMEM((tm, C_WIDTH), BF16)],
        compiler_params=pltpu.CompilerParams(
            dimension_semantics=("arbitrary", "arbitrary"), vmem_limit_bytes=VMEM_LIMIT_BYTES),
        name="layer_c",
    )(x3, ng.reshape(1, D_MODEL), w_in16, w_grp16, scale.reshape(1, C_WIDTH), w_out16)


def _attn_proj_kernel(x_ref, ng_ref, freq_ref, wq_ref, wkt_ref, wv_ref, qg_ref, kg_ref,
                      q_ref, kt_ref, v_ref, h_scr, q_a, kt_a, v_a, q_b, kt_b, v_b,
                      *, dilation, n_steps, n_super):
    s = pl.program_id(0)
    rows = q_ref.shape[0]
    n_chunk = SUPER // rows
    per = SUPER // dilation
    n_cols = D_MODEL // LANES
    cur = jnp.minimum(s, n_steps - 1)
    c = cur % n_chunk
    prev = jnp.maximum(s - 1, 0)
    c_prev = prev % n_chunk
    n_prev = (prev // n_chunk) % n_super

    @pl.when(s == 0)
    def _():
        q_b[...] = jnp.zeros(q_b.shape, q_b.dtype)
        kt_b[...] = jnp.zeros(kt_b.shape, kt_b.dtype)
        v_b[...] = jnp.zeros(v_b.shape, v_b.dtype)

    @pl.when((c == 0) & (s < n_steps))
    def _():
        for i in range(SUPER // rows):
            rs = slice(i * rows, (i + 1) * rows)
            hh = _rms_norm(x_ref[rs, :], ng_ref[...])
            for j in range(n_cols):
                h_scr[j, rs, :] = hh[:, j * LANES:(j + 1) * LANES]

    def gather(j):
        if dilation == 1:
            return h_scr[j, pl.ds(pl.multiple_of(c * rows, rows), rows), :]
        run = min(per, rows)
        first = c * (rows // run)
        if per > rows:
            raise NotImplementedError("a residue class longer than the chunk needs a row offset")
        pieces = [h_scr[j, pl.ds(first + rr, run, stride=dilation), :] for rr in range(rows // run)]
        return pieces[0] if len(pieces) == 1 else jnp.concatenate(pieces, axis=0)

    def project(q_raw, kt_raw, v_raw):
        h = jnp.concatenate([gather(j) for j in range(n_cols)], axis=1).astype(BF16)
        q_raw[...] = _dot(h, wq_ref[...])
        kt_raw[...] = _dot_nt(wkt_ref[...], h)
        v_raw[...] = _dot(h, wv_ref[...]).astype(BF16)

    def finish(q_raw, kt_raw, v_raw):
        p = c_prev * rows + lax.broadcasted_iota(jnp.int32, (1, rows), 1)
        residue = lax.shift_right_logical(p, per.bit_length() - 1)
        strided = jnp.bitwise_and(p, per - 1)
        pos = n_prev * SUPER + strided * dilation + residue
        ang = pos.astype(F32) * freq_ref[...]
        cos_t, sin_t = jnp.cos(ang), jnp.sin(ang)

        v_ref[...] = v_raw[...]

        for hd in range(B_HEADS):
            slab = kt_raw[hd * HEAD_DIM:(hd + 1) * HEAD_DIM, :]
            kn = slab * lax.rsqrt(jnp.mean(slab * slab, axis=0, keepdims=True) + EPS) * kg_ref[...]
            x1, x2 = kn[0:ROPE_HALF, :], kn[ROPE_HALF:ROPE_DIM, :]
            rotated = jnp.concatenate(
                [x1 * cos_t - x2 * sin_t, x2 * cos_t + x1 * sin_t, kn[ROPE_DIM:, :]], axis=0)
            kt_ref[hd * HEAD_DIM:(hd + 1) * HEAD_DIM, :] = rotated.astype(BF16)

        rest = HEAD_DIM - ROPE_DIM
        cos_tab = jnp.concatenate([cos_t, cos_t, jnp.ones((rest, rows), F32)], axis=0).T
        sin_tab = jnp.concatenate([-sin_t, sin_t, jnp.zeros((rest, rows), F32)], axis=0).T
        lane = lax.broadcasted_iota(jnp.int32, (rows, HEAD_DIM), 1)
        for hd in range(B_HEADS):
            cols = slice(hd * HEAD_DIM, (hd + 1) * HEAD_DIM)
            qn = _rms_norm(q_raw[:, cols], qg_ref[...])
            partner = jnp.where(lane < ROPE_HALF, pltpu.roll(qn, HEAD_DIM - ROPE_HALF, 1),
                                pltpu.roll(qn, ROPE_HALF, 1))
            q_ref[:, cols] = ((qn * cos_tab + partner * sin_tab) * Q_PRESCALE).astype(BF16)

    @pl.when(s % 2 == 0)
    def _():
        project(q_a, kt_a, v_a)
        finish(q_b, kt_b, v_b)

    @pl.when(s % 2 == 1)
    def _():
        project(q_b, kt_b, v_b)
        finish(q_a, kt_a, v_a)


def _attn_proj(x3, ng, inv_freq, w_in16, w_kt16, q_gain, k_gain, group, dilation, cast_jobs=()):
    bsz, seq, _ = x3.shape
    rows = TOKEN_TILE
    n_super, n_chunk = seq // SUPER, SUPER // rows
    per_batch = n_super * n_chunk
    n_steps = bsz * per_batch
    cur = lambda s: jnp.minimum(s, n_steps - 1)
    prev = lambda s: jnp.maximum(s - 1, 0)
    raw = [pltpu.VMEM((rows, B_WIDTH), F32), pltpu.VMEM((B_WIDTH, rows), F32),
           pltpu.VMEM((rows, B_WIDTH), BF16)]
    cast_in, cast_out, cast_shapes = _cast_specs(cast_jobs, n_steps, lambda s: s)
    body = functools.partial(_attn_proj_kernel, dilation=dilation, n_steps=n_steps, n_super=n_super)
    q, kt, v, *casted = pl.pallas_call(
        _with_casts(body, 8, 3, len(cast_jobs)),
        grid=(n_steps + 1,),
        in_specs=[
            pl.BlockSpec((None, SUPER, D_MODEL),
                         lambda s: (cur(s) // per_batch, (cur(s) // n_chunk) % n_super, 0)),
            _resident((1, D_MODEL), lambda s: (0, 0)),
            _resident((ROPE_HALF, 1), lambda s: (0, 0)),
            _resident((D_MODEL, B_WIDTH), lambda s: (0, group)),
            _resident((B_WIDTH, D_MODEL), lambda s: (group, 0)),
            _resident((D_MODEL, B_WIDTH), lambda s: (0, 2 * N_B_GROUPS + group)),
            _resident((1, HEAD_DIM), lambda s: (0, 0)),
            _resident((HEAD_DIM, 1), lambda s: (0, 0)),
        ] + cast_in,
        out_specs=[
            pl.BlockSpec((None, rows, B_WIDTH), lambda s: (prev(s) // per_batch, prev(s) % per_batch, 0)),
            pl.BlockSpec((None, B_WIDTH, rows), lambda s: (prev(s) // per_batch, 0, prev(s) % per_batch)),
            pl.BlockSpec((None, rows, B_WIDTH), lambda s: (prev(s) // per_batch, prev(s) % per_batch, 0)),
        ] + cast_out,
        out_shape=[
            jax.ShapeDtypeStruct((bsz, seq, B_WIDTH), BF16),
            jax.ShapeDtypeStruct((bsz, B_WIDTH, seq), BF16),
            jax.ShapeDtypeStruct((bsz, seq, B_WIDTH), BF16),
        ] + cast_shapes,
        scratch_shapes=[pltpu.VMEM((D_MODEL // LANES, SUPER, LANES), F32)] + raw + raw,
        compiler_params=pltpu.CompilerParams(
            dimension_semantics=("arbitrary",), vmem_limit_bytes=VMEM_LIMIT_BYTES),
        name=f"attn_proj_d{dilation}",
    )(x3, ng.reshape(1, D_MODEL), inv_freq, w_in16, w_kt16, w_in16,
      q_gain.reshape(1, HEAD_DIM), k_gain.reshape(HEAD_DIM, 1), *[job[0] for job in cast_jobs])
    return (q, kt, v), casted


def _attn_kernel(*refs):
    qkv_refs = refs[:3 * N_B_GROUPS]
    y_ref = refs[3 * N_B_GROUPS]
    scratch = refs[3 * N_B_GROUPS + 1:]
    kt_scrs, v_scrs = scratch[0:N_B_GROUPS], scratch[N_B_GROUPS:2 * N_B_GROUPS]
    o_scr, lse_scr, bias_scr = scratch[2 * N_B_GROUPS:]
    n = pl.program_id(2)

    qi = lax.broadcasted_iota(jnp.int32, (SPAN, 2 * SPAN), 0)
    ki = lax.broadcasted_iota(jnp.int32, (SPAN, 2 * SPAN), 1)
    band = (ki >= qi) & (ki <= qi + SPAN)
    bias_scr[0] = jnp.where(band, 0.0, MASK_VALUE)
    bias_scr[1] = jnp.where(band & (ki >= jnp.where(n > 0, 0, SPAN)), 0.0, MASK_VALUE)

    for g, (_, d) in enumerate(B_PATTERNS):
        q_ref, kt_ref, v_ref = qkv_refs[3 * g:3 * g + 3]
        kt_scr, v_scr = kt_scrs[g], v_scrs[g]
        per = SUPER // d
        seg = SPAN + per

        @pl.when(n == 0)
        def _():
            v_scr[...] = jnp.ones(v_scr.shape, BF16)
            for r in range(d):
                kt_scr[:, r * seg:r * seg + SPAN] = jnp.zeros((kt_scr.shape[0], SPAN), BF16)
                for hh in range(HEADS_PER_STEP):
                    v_scr[r * seg:r * seg + SPAN, 2 * hh * HEAD_DIM:(2 * hh + 1) * HEAD_DIM] = (
                        jnp.zeros((SPAN, HEAD_DIM), BF16))

        @pl.when(n > 0)
        def _():
            for r in range(d):
                kt_scr[:, r * seg:r * seg + SPAN] = kt_scr[:, r * seg + per:r * seg + per + SPAN]
                v_scr[r * seg:r * seg + SPAN, :] = v_scr[r * seg + per:r * seg + per + SPAN, :]

        for r in range(d):
            kt_scr[:, r * seg + SPAN:(r + 1) * seg] = kt_ref[:, r * per:(r + 1) * per]
            for hh in range(HEADS_PER_STEP):
                v_scr[r * seg + SPAN:(r + 1) * seg, 2 * hh * HEAD_DIM:(2 * hh + 1) * HEAD_DIM] = (
                    v_ref[r * per:(r + 1) * per, hh * HEAD_DIM:(hh + 1) * HEAD_DIM])

        for r in range(d):
            for j in range(per // SPAN):
                q_rows = slice(r * per + j * SPAN, r * per + (j + 1) * SPAN)
                keys = slice(r * seg + j * SPAN, r * seg + (j + 2) * SPAN)
                if d < MERGE_DIL:
                    assert d == 1
                    nat = slice(j * SPAN, (j + 1) * SPAN)
                else:
                    fine = d // MERGE_DIL
                    start = (r % MERGE_DIL) * (SUPER // MERGE_DIL) + fine * j * SPAN + r // MERGE_DIL
                    nat = pl.ds(start, SPAN, stride=fine) if fine > 1 else slice(start, start + SPAN)
                for hh in range(HEADS_PER_STEP):
                    cols = slice(hh * HEAD_DIM, (hh + 1) * HEAD_DIM)
                    s = _dot(q_ref[q_rows, cols], kt_scr[cols, keys]) + bias_scr[1 if j == 0 else 0]
                    m = jnp.max(s, axis=-1, keepdims=True)
                    p = jnp.exp2(s - m)
                    pv = _dot(p.astype(BF16), v_scr[keys, 2 * hh * HEAD_DIM:(2 * hh + 2) * HEAD_DIM])
                    denom = pv[:, HEAD_DIM:]
                    o_scr[g, hh, nat, :] = pv[:, :HEAD_DIM] * (1.0 / denom)
                    lse_scr[g, hh, nat, :] = m + jnp.log2(denom)

    rows = 256
    for hh in range(HEADS_PER_STEP):
        for b in range(MERGE_DIL):
            for u0 in range(0, SUPER // MERGE_DIL, rows):
                natural = pl.ds(b + MERGE_DIL * u0, rows, stride=MERGE_DIL)
                ordered = slice(b * (SUPER // MERGE_DIL) + u0, b * (SUPER // MERGE_DIL) + u0 + rows)
                idx = [natural if d < MERGE_DIL else ordered for _, d in B_PATTERNS]
                lses = [lse_scr[g, hh, idx[g], :] for g in range(N_B_GROUPS)]
                m = jnp.maximum(jnp.maximum(lses[0], lses[1]), lses[2])
                es = [jnp.exp2(l - m) for l in lses]
                mixed = (es[0] * o_scr[0, hh, idx[0], :] + es[1] * o_scr[1, hh, idx[1], :]
                         + es[2] * o_scr[2, hh, idx[2], :])
                y_ref[hh, natural, :] = mixed * (1.0 / (es[0] + es[1] + es[2]))


def _attn(qkvs, bsz, seq, cast_jobs=()):
    width = HEADS_PER_STEP * HEAD_DIM
    n_super = seq // SUPER
    in_specs, operands, kt_scr, v_scr = [], [], [], []
    for (q, kt, v), (_, d) in zip(qkvs, B_PATTERNS):
        in_specs += [
            pl.BlockSpec((None, SUPER, width), lambda b, hp, n: (b, n, hp)),
            pl.BlockSpec((None, width, SUPER), lambda b, hp, n: (b, hp, n)),
            pl.BlockSpec((None, SUPER, width), lambda b, hp, n: (b, n, hp)),
        ]
        operands += [q, kt, v]
        kt_scr.append(pltpu.VMEM((width, SUPER + d * SPAN), BF16))
        v_scr.append(pltpu.VMEM((SUPER + d * SPAN, 2 * width), BF16))
    n_pairs = B_HEADS // HEADS_PER_STEP
    cast_in, cast_out, cast_shapes = _cast_specs(
        cast_jobs, bsz * n_pairs * n_super, lambda b, hp, n: (b * n_pairs + hp) * n_super + n)
    out, *casted = pl.pallas_call(
        _with_casts(_attn_kernel, len(operands), 1, len(cast_jobs)),
        grid=(bsz, n_pairs, n_super),
        in_specs=in_specs + cast_in,
        out_specs=[pl.BlockSpec((None, HEADS_PER_STEP, SUPER, HEAD_DIM),
                                lambda b, hp, n: (b, hp, n, 0))] + cast_out,
        out_shape=[jax.ShapeDtypeStruct((bsz, B_HEADS, seq, HEAD_DIM), F32)] + cast_shapes,
        scratch_shapes=kt_scr + v_scr + [
            pltpu.VMEM((N_B_GROUPS, HEADS_PER_STEP, SUPER, HEAD_DIM), F32),
            pltpu.VMEM((N_B_GROUPS, HEADS_PER_STEP, SUPER, HEAD_DIM), F32),
            pltpu.VMEM((2, SPAN, 2 * SPAN), F32),
        ],
        compiler_params=pltpu.CompilerParams(
            dimension_semantics=("arbitrary", "arbitrary", "arbitrary"),
            vmem_limit_bytes=VMEM_LIMIT_BYTES),
        name="attn",
    )(*operands, *[job[0] for job in cast_jobs])
    return out, casted


def _attn_out_kernel(x_ref, ng_ref, wz_ref, o_ref, w_out_ref, out_ref):
    x = x_ref[...]
    h = _rms_norm(x, ng_ref[...]).astype(BF16)
    z = _dot(h, wz_ref[...])
    o = jnp.concatenate([o_ref[hd] for hd in range(B_HEADS)], axis=1)
    y = (o * _silu(z)).astype(BF16)
    out_ref[...] = x + _dot(y, w_out_ref[...])


def _attn_out(x3, ng, w_in16, o, w_out16):
    bsz, seq, _ = x3.shape
    tm = LAYER_TILE
    tile = pl.BlockSpec((None, tm, D_MODEL), lambda b, t: (b, t, 0))
    return pl.pallas_call(
        _attn_out_kernel,
        grid=(bsz, seq // tm),
        in_specs=[
            tile,
            _resident((1, D_MODEL), lambda b, t: (0, 0)),
            _resident((D_MODEL, B_WIDTH), lambda b, t: (0, 3 * N_B_GROUPS)),
            pl.BlockSpec((None, B_HEADS, tm, HEAD_DIM), lambda b, t: (b, 0, t, 0)),
            _resident((B_WIDTH, D_MODEL), lambda b, t: (0, 0)),
        ],
        out_specs=tile,
        out_shape=jax.ShapeDtypeStruct((bsz, seq, D_MODEL), F32),
        compiler_params=pltpu.CompilerParams(
            dimension_semantics=("arbitrary", "arbitrary"), vmem_limit_bytes=VMEM_LIMIT_BYTES),
        name="attn_out",
    )(x3, ng.reshape(1, D_MODEL), w_in16, o, w_out16)


def _layer_b(x3, ng, w_in16, w_kt16, q_gain, k_gain, w_out16, proj_cast_jobs, attn_cast_jobs):
    bsz, seq, _ = x3.shape
    inv_freq = jnp.power(jnp.float32(ROPE_THETA), -jnp.arange(ROPE_HALF, dtype=F32) / ROPE_HALF)
    inv_freq = inv_freq.reshape(ROPE_HALF, 1)
    qkvs, casted = [], []
    for g, (window, dilation) in enumerate(B_PATTERNS):
        assert window // dilation == SPAN
        qkv, done = _attn_proj(x3, ng, inv_freq, w_in16, w_kt16, q_gain[g], k_gain[g], g, dilation,
                               proj_cast_jobs[g])
        qkvs.append(qkv)
        casted += done
    o, done = _attn(qkvs, bsz, seq, attn_cast_jobs)
    return _attn_out(x3, ng, w_in16, o, w_out16), casted + done


def kernel(x, norm_gain, a_w_in, a_v_gain, a_w_s, a_b_s, a_w_out, b_w_in, b_q_gain, b_k_gain,
           b_w_out, c_w_in, c_w_grp, c_scale, c_w_out):
    bsz, seq, d_model = x.shape
    assert d_model == D_MODEL and seq % SUPER == 0 and SUPER % TOKEN_TILE == 0
    assert norm_gain.shape[0] == 4 and a_w_in.shape[0] == 2
    c_w_grp2 = c_w_grp.reshape(c_w_grp.shape[0], N_POOL * C_GROUP, C_GROUP)
    k_cols = (N_B_GROUPS * B_WIDTH, N_B_GROUPS * B_WIDTH)

    x2, (b_w_in16, b_w_out16, b_w_kt16) = _layer_a(
        x.reshape(bsz * seq, D_MODEL), norm_gain[0], a_w_in[0].astype(BF16), a_v_gain[0], a_w_s[0],
        a_b_s[0], a_w_out[0].astype(BF16),
        cast_jobs=[(b_w_in, 0, None), (b_w_out, 0, None), (b_w_in, 0, k_cols)])
    x, (c_w_in16, c_w_grp16, c_w_out16, a_w_in16, a_w_out16) = _layer_b(
        x2.reshape(bsz, seq, D_MODEL), norm_gain[1], b_w_in16, b_w_kt16, b_q_gain[0], b_k_gain[0],
        b_w_out16,
        proj_cast_jobs=[[(c_w_in, 0, None)], [(c_w_grp2, 0, None)], [(c_w_out, 0, None)]],
        attn_cast_jobs=[(a_w_in, 1, None), (a_w_out, 1, None)])
    x = _layer_c(x, norm_gain[2], c_w_in16, c_w_grp16.reshape(N_POOL, C_GROUP, C_GROUP), c_scale[0],
                 c_w_out16)
    x2, _ = _layer_a(x.reshape(bsz * seq, D_MODEL), norm_gain[3], a_w_in16, a_v_gain[1], a_w_s[1],
                     a_b_s[1], a_w_out16)
    return x2.reshape(bsz, seq, D_MODEL)
```

```python
import functools
import math

import jax
import jax.numpy as jnp
import numpy as np
from jax import lax
from jax.experimental import pallas as pl
from jax.experimental.pallas import tpu as pltpu

D_MODEL = 1024
EPS = 1e-6
CHUNK = 128
A_WIDTH = 2 * D_MODEL
A_GROUPS = 8
A_GROUP_DIM = A_WIDTH // A_GROUPS
HEAD_DIM = 128
B_HEADS = D_MODEL // HEAD_DIM
B_PATTERNS = ((128, 1), (512, 4), (2048, 16))
N_B_GROUPS = len(B_PATTERNS)
B_WIDTH = B_HEADS * HEAD_DIM
ROPE_DIM = HEAD_DIM // 4
ROPE_HALF = ROPE_DIM // 2
ROPE_THETA = 500000.0
Q_PRESCALE = math.log2(math.e) / math.sqrt(HEAD_DIM)
SPAN = 128
SUPER = SPAN * max(d for _, d in B_PATTERNS)
HEADS_PER_STEP = 2
MERGE_DIL = 4
POOL_SIZES = (2, 4, 8, 16)
N_POOL = len(POOL_SIZES)
C_WIDTH = 2 * D_MODEL
C_GROUP = C_WIDTH // N_POOL
HALO = 16

LANES = 128
BF16_ROWS = 16
MERGE_ROWS = 256
VMEM_LIMIT_BYTES = 60 * 1024 * 1024

TOKEN_TILE = 512
LAYER_TILE = 1024
MASK_VALUE = -1e30

F32 = jnp.float32
BF16 = jnp.bfloat16


def _resident(shape, index_map):
    return pl.BlockSpec(shape, index_map, pipeline_mode=pl.Buffered(1))


def _rms_norm(x, gain):
    return x * lax.rsqrt(jnp.mean(x * x, axis=-1, keepdims=True) + EPS) * gain


def _dot(a, b):
    return jnp.dot(a, b, preferred_element_type=F32)


def _dot_nt(a, b):
    return lax.dot_general(a, b, (((1,), (1,)), ((), ())), preferred_element_type=F32)


def _silu(z):
    return z * jax.nn.sigmoid(z)


def _cast_specs(jobs, n_steps, flat_step):
    in_specs, out_specs, out_shapes = [], [], []
    for arr, layer, col_range in jobs:
        _, rows, cols = arr.shape
        if col_range is None:
            n_chunks = n_steps
            chunk = rows // n_chunks
            assert chunk * n_chunks == rows and chunk % BF16_ROWS == 0, (arr.shape, n_steps)
            last = n_chunks - 1
            in_specs.append(pl.BlockSpec(
                (None, chunk, cols),
                lambda *ids, layer=layer, last=last: (layer, jnp.minimum(flat_step(*ids), last), 0)))
            out_specs.append(pl.BlockSpec(
                (chunk, cols), lambda *ids, last=last: (jnp.minimum(flat_step(*ids), last), 0)))
            out_shapes.append(jax.ShapeDtypeStruct((rows, cols), BF16))
        else:
            start, width = col_range
            n_chunks = rows // LANES
            assert n_chunks <= n_steps and start % width == 0, (arr.shape, n_steps, col_range)
            last = n_chunks - 1
            in_specs.append(pl.BlockSpec(
                (None, LANES, width),
                lambda *ids, layer=layer, last=last, blk=start // width: (
                    layer, jnp.minimum(flat_step(*ids), last), blk)))
            out_specs.append(pl.BlockSpec(
                (width, LANES), lambda *ids, last=last: (0, jnp.minimum(flat_step(*ids), last))))
            out_shapes.append(jax.ShapeDtypeStruct((width, rows), BF16))
    return in_specs, out_specs, out_shapes


def _with_casts(body, n_in, n_out, n_jobs):
    if n_jobs == 0:
        return body

    def wrapped(*refs):
        ins, cast_in = refs[:n_in], refs[n_in:n_in + n_jobs]
        outs = refs[n_in + n_jobs:n_in + n_jobs + n_out]
        cast_out = refs[n_in + n_jobs + n_out:n_in + 2 * n_jobs + n_out]
        body(*ins, *outs, *refs[n_in + 2 * n_jobs + n_out:])
        for src, dst in zip(cast_in, cast_out):
            transposed = src.shape != dst.shape
            dst[...] = (src[...].T if transposed else src[...]).astype(BF16)

    return wrapped


def _layer_a_kernel(x_ref, ng_ref, w_in_ref, vg_ref, ws_ref, bs_ref, w_out_ref,
                    o_ref, vn_ref, y_ref):
    x = x_ref[...]
    h = _rms_norm(x, ng_ref[...]).astype(BF16)
    v = _dot(h, w_in_ref[:, A_WIDTH:2 * A_WIDTH])
    vn_ref[...] = _rms_norm(v, vg_ref[...]).astype(BF16)
    row = lax.broadcasted_iota(jnp.int32, (CHUNK, CHUNK), 0)
    col = lax.broadcasted_iota(jnp.int32, (CHUNK, CHUNK), 1)
    causal = col <= row
    n_chunks = x.shape[0] // CHUNK
    for g in range(A_GROUPS):
        cols = slice(g * A_GROUP_DIM, (g + 1) * A_GROUP_DIM)
        u = _dot(h, w_in_ref[:, cols])
        z = _dot(h, w_in_ref[:, 2 * A_WIDTH + g * A_GROUP_DIM:2 * A_WIDTH + (g + 1) * A_GROUP_DIM])
        ws = jnp.where(causal, ws_ref[g], 0.0).astype(BF16)
        bias = bs_ref[:, g:g + 1]
        mixed = jnp.concatenate(
            [_dot(ws, vn_ref[c * CHUNK:(c + 1) * CHUNK, cols]) + bias for c in range(n_chunks)],
            axis=0)
        y_ref[:, cols] = (u * mixed * _silu(z)).astype(BF16)
    o_ref[...] = x + _dot(y_ref[...], w_out_ref[...])


def _layer_a(x2, ng, w_in16, v_gain, w_s, b_s, w_out16, cast_jobs=()):
    tokens = x2.shape[0]
    tm = LAYER_TILE
    n_steps = tokens // tm
    cast_in, cast_out, cast_shapes = _cast_specs(cast_jobs, n_steps, lambda i: i)
    out, *casted = pl.pallas_call(
        _with_casts(_layer_a_kernel, 7, 1, len(cast_jobs)),
        grid=(n_steps,),
        in_specs=[
            pl.BlockSpec((tm, D_MODEL), lambda i: (i, 0)),
            _resident((1, D_MODEL), lambda i: (0, 0)),
            _resident((D_MODEL, 3 * A_WIDTH), lambda i: (0, 0)),
            _resident((1, A_WIDTH), lambda i: (0, 0)),
            _resident((A_GROUPS, CHUNK, CHUNK), lambda i: (0, 0, 0)),
            _resident((CHUNK, A_GROUPS), lambda i: (0, 0)),
            _resident((A_WIDTH, D_MODEL), lambda i: (0, 0)),
        ] + cast_in,
        out_specs=[pl.BlockSpec((tm, D_MODEL), lambda i: (i, 0))] + cast_out,
        out_shape=[jax.ShapeDtypeStruct((tokens, D_MODEL), F32)] + cast_shapes,
        scratch_shapes=[pltpu.VMEM((tm, A_WIDTH), BF16), pltpu.VMEM((tm, A_WIDTH), BF16)],
        compiler_params=pltpu.CompilerParams(
            dimension_semantics=("arbitrary",), vmem_limit_bytes=VMEM_LIMIT_BYTES),
        name="layer_a",
    )(x2, ng.reshape(1, D_MODEL), w_in16, v_gain.reshape(1, A_WIDTH), w_s, b_s.T, w_out16,
      *[job[0] for job in cast_jobs])
    return out, casted


def _layer_c_kernel(x_ref, ng_ref, w_in_ref, w_grp_ref, scale_ref, w_out_ref,
                    o_ref, ext_ref, y_ref):
    t = pl.program_id(1)
    tm = x_ref.shape[0]
    x = x_ref[...]
    h = _rms_norm(x, ng_ref[...]).astype(BF16)

    @pl.when(t == 0)
    def _():
        ext_ref[0:HALO, :] = jnp.zeros((HALO, C_WIDTH), F32)

    @pl.when(t > 0)
    def _():
        ext_ref[0:HALO, :] = ext_ref[tm:tm + HALO, :]

    ext_ref[HALO:HALO + tm, :] = _dot(h, w_in_ref[:, 0:C_WIDTH])
    pos = t * tm + lax.broadcasted_iota(jnp.int32, (tm, 1), 0)
    for g, window in enumerate(POOL_SIZES):
        cols = slice(g * C_GROUP, (g + 1) * C_GROUP)
        xc = ext_ref[HALO:HALO + tm, cols]
        acc = xc
        for j in range(1, window):
            acc = acc + ext_ref[HALO - j:HALO - j + tm, cols]
        cnt = jnp.minimum(pos + 1, window).astype(F32)
        diff = (acc / cnt - xc).astype(BF16)
        mixed = _dot(diff, w_grp_ref[g]) * scale_ref[:, cols]
        z = _dot(h, w_in_ref[:, C_WIDTH + g * C_GROUP:C_WIDTH + (g + 1) * C_GROUP])
        y_ref[:, cols] = (mixed * _silu(z)).astype(BF16)
    o_ref[...] = x + _dot(y_ref[...], w_out_ref[...])


def _layer_c(x3, ng, w_in16, w_grp16, scale, w_out16):
    bsz, seq, _ = x3.shape
    tm = LAYER_TILE
    return pl.pallas_call(
        _layer_c_kernel,
        grid=(bsz, seq // tm),
        in_specs=[
            pl.BlockSpec((None, tm, D_MODEL), lambda b, t: (b, t, 0)),
            _resident((1, D_MODEL), lambda b, t: (0, 0)),
            _resident((D_MODEL, 2 * C_WIDTH), lambda b, t: (0, 0)),
            _resident((N_POOL, C_GROUP, C_GROUP), lambda b, t: (0, 0, 0)),
            _resident((1, C_WIDTH), lambda b, t: (0, 0)),
            _resident((C_WIDTH, D_MODEL), lambda b, t: (0, 0)),
        ],
        out_specs=pl.BlockSpec((None, tm, D_MODEL), lambda b, t: (b, t, 0)),
        out_shape=jax.ShapeDtypeStruct((bsz, seq, D_MODEL), F32),
        scratch_shapes=[pltpu.VMEM((tm + HALO, C_WIDTH), F32), pltpu.VMEM((tm, C_WIDTH), BF16)],
        compiler_params=pltpu.CompilerParams(
            dimension_semantics=("arbitrary", "arbitrary"), vmem_limit_bytes=VMEM_LIMIT_BYTES),
        name="layer_c",
    )(x3, ng.reshape(1, D_MODEL), w_in16, w_grp16, scale.reshape(1, C_WIDTH), w_out16)


def _attn_proj_kernel(x_ref, ng_ref, freq_ref, wq_ref, wkt_ref, wv_ref, qg_ref, kg_ref,
                      q_ref, kt_ref, v_ref, h_scr, q_a, kt_a, q_b, kt_b,
                      *, dilation, n_steps, n_super):
    s = pl.program_id(0)
    rows = q_ref.shape[0]
    n_chunk = SUPER // rows
    per = SUPER // dilation
    n_cols = D_MODEL // LANES
    cur = jnp.minimum(s, n_steps - 1)
    c = cur % n_chunk
    prev = jnp.maximum(s - 1, 0)
    c_prev = prev % n_chunk
    n_prev = (prev // n_chunk) % n_super

    @pl.when(s == 0)
    def _():
        q_b[...] = jnp.zeros(q_b.shape, q_b.dtype)
        kt_b[...] = jnp.zeros(kt_b.shape, kt_b.dtype)

    @pl.when((c == 0) & (s < n_steps))
    def _():
        for i in range(SUPER // rows):
            rs = slice(i * rows, (i + 1) * rows)
            hh = _rms_norm(x_ref[rs, :], ng_ref[...])
            for j in range(n_cols):
                h_scr[j, rs, :] = hh[:, j * LANES:(j + 1) * LANES]

    def gather(j):
        if dilation == 1:
            return h_scr[j, pl.ds(pl.multiple_of(c * rows, rows), rows), :]
        run = min(per, rows)
        first = c * (rows // run)
        if per > rows:
            raise NotImplementedError("a residue class longer than the chunk needs a row offset")
        pieces = [h_scr[j, pl.ds(first + rr, run, stride=dilation), :] for rr in range(rows // run)]
        return pieces[0] if len(pieces) == 1 else jnp.concatenate(pieces, axis=0)

    def project(q_raw, kt_raw):
        h = jnp.concatenate([gather(j) for j in range(n_cols)], axis=1).astype(BF16)
        q_raw[...] = _dot(h, wq_ref[...])
        kt_raw[...] = _dot_nt(wkt_ref[...], h)
        v_ref[...] = _dot(h, wv_ref[...]).astype(BF16)

    def finish(q_raw, kt_raw):
        p = c_prev * rows + lax.broadcasted_iota(jnp.int32, (1, rows), 1)
        residue = lax.shift_right_logical(p, per.bit_length() - 1)
        strided = jnp.bitwise_and(p, per - 1)
        pos = n_prev * SUPER + strided * dilation + residue
        ang = pos.astype(F32) * freq_ref[...]
        cos_t, sin_t = jnp.cos(ang), jnp.sin(ang)

        for hd in range(B_HEADS):
            slab = kt_raw[hd * HEAD_DIM:(hd + 1) * HEAD_DIM, :]
            kn = slab * lax.rsqrt(jnp.mean(slab * slab, axis=0, keepdims=True) + EPS) * kg_ref[...]
            x1, x2 = kn[0:ROPE_HALF, :], kn[ROPE_HALF:ROPE_DIM, :]
            rotated = jnp.concatenate(
                [x1 * cos_t - x2 * sin_t, x2 * cos_t + x1 * sin_t, kn[ROPE_DIM:, :]], axis=0)
            kt_ref[hd * HEAD_DIM:(hd + 1) * HEAD_DIM, :] = rotated.astype(BF16)

        rest = HEAD_DIM - ROPE_DIM
        cos_tab = jnp.concatenate([cos_t, cos_t, jnp.ones((rest, rows), F32)], axis=0).T * Q_PRESCALE
        sin_tab = jnp.concatenate([-sin_t, sin_t, jnp.zeros((rest, rows), F32)], axis=0).T * Q_PRESCALE
        lane = lax.broadcasted_iota(jnp.int32, (rows, HEAD_DIM), 1)
        for hd in range(B_HEADS):
            cols = slice(hd * HEAD_DIM, (hd + 1) * HEAD_DIM)
            qn = _rms_norm(q_raw[:, cols], qg_ref[...])
            partner = jnp.where(lane < ROPE_HALF, pltpu.roll(qn, HEAD_DIM - ROPE_HALF, 1),
                                pltpu.roll(qn, ROPE_HALF, 1))
            q_ref[:, cols] = (qn * cos_tab + partner * sin_tab).astype(BF16)

    @pl.when(s % 2 == 0)
    def _():
        project(q_a, kt_a)
        finish(q_b, kt_b)

    @pl.when(s % 2 == 1)
    def _():
        project(q_b, kt_b)
        finish(q_a, kt_a)


def _attn_proj(x3, ng, inv_freq, w_in16, w_kt16, q_gain, k_gain, group, dilation, cast_jobs=()):
    bsz, seq, _ = x3.shape
    rows = TOKEN_TILE
    n_super, n_chunk = seq // SUPER, SUPER // rows
    per_batch = n_super * n_chunk
    n_steps = bsz * per_batch
    cur = lambda s: jnp.minimum(s, n_steps - 1)
    prev = lambda s: jnp.maximum(s - 1, 0)
    raw = [pltpu.VMEM((rows, B_WIDTH), F32), pltpu.VMEM((B_WIDTH, rows), F32)]
    cast_in, cast_out, cast_shapes = _cast_specs(cast_jobs, n_steps, lambda s: s)
    body = functools.partial(_attn_proj_kernel, dilation=dilation, n_steps=n_steps, n_super=n_super)
    q, kt, v, *casted = pl.pallas_call(
        _with_casts(body, 8, 3, len(cast_jobs)),
        grid=(n_steps + 1,),
        in_specs=[
            pl.BlockSpec((None, SUPER, D_MODEL),
                         lambda s: (cur(s) // per_batch, (cur(s) // n_chunk) % n_super, 0)),
            _resident((1, D_MODEL), lambda s: (0, 0)),
            _resident((ROPE_HALF, 1), lambda s: (0, 0)),
            _resident((D_MODEL, B_WIDTH), lambda s: (0, group)),
            _resident((B_WIDTH, D_MODEL), lambda s: (group, 0)),
            _resident((D_MODEL, B_WIDTH), lambda s: (0, 2 * N_B_GROUPS + group)),
            _resident((1, HEAD_DIM), lambda s: (0, 0)),
            _resident((HEAD_DIM, 1), lambda s: (0, 0)),
        ] + cast_in,
        out_specs=[
            pl.BlockSpec((None, rows, B_WIDTH), lambda s: (prev(s) // per_batch, prev(s) % per_batch, 0)),
            pl.BlockSpec((None, B_WIDTH, rows), lambda s: (prev(s) // per_batch, 0, prev(s) % per_batch)),
            pl.BlockSpec((None, rows, B_WIDTH), lambda s: (cur(s) // per_batch, cur(s) % per_batch, 0)),
        ] + cast_out,
        out_shape=[
            jax.ShapeDtypeStruct((bsz, seq, B_WIDTH), BF16),
            jax.ShapeDtypeStruct((bsz, B_WIDTH, seq), BF16),
            jax.ShapeDtypeStruct((bsz, seq, B_WIDTH), BF16),
        ] + cast_shapes,
        scratch_shapes=[pltpu.VMEM((D_MODEL // LANES, SUPER, LANES), F32)] + raw + raw,
        compiler_params=pltpu.CompilerParams(
            dimension_semantics=("arbitrary",), vmem_limit_bytes=VMEM_LIMIT_BYTES),
        name=f"attn_proj_d{dilation}",
    )(x3, ng.reshape(1, D_MODEL), inv_freq, w_in16, w_kt16, w_in16,
      q_gain.reshape(1, HEAD_DIM), k_gain.reshape(HEAD_DIM, 1), *[job[0] for job in cast_jobs])
    return (q, kt, v), casted


def _attn_kernel(*refs):
    qkv_refs = refs[:3 * N_B_GROUPS]
    y_ref = refs[3 * N_B_GROUPS]
    scratch = refs[3 * N_B_GROUPS + 1:]
    kt_scrs, v_scrs = scratch[0:N_B_GROUPS], scratch[N_B_GROUPS:2 * N_B_GROUPS]
    o_scr, lse_scr, bias_scr = scratch[2 * N_B_GROUPS:]
    n = pl.program_id(2)

    qi = lax.broadcasted_iota(jnp.int32, (SPAN, 2 * SPAN), 0)
    ki = lax.broadcasted_iota(jnp.int32, (SPAN, 2 * SPAN), 1)
    band = (ki >= qi) & (ki <= qi + SPAN)
    bias_scr[0] = jnp.where(band, 0.0, MASK_VALUE)
    bias_scr[1] = jnp.where(band & (ki >= jnp.where(n > 0, 0, SPAN)), 0.0, MASK_VALUE)

    for g, (_, d) in enumerate(B_PATTERNS):
        q_ref, kt_ref, v_ref = qkv_refs[3 * g:3 * g + 3]
        kt_scr, v_scr = kt_scrs[g], v_scrs[g]
        per = SUPER // d
        seg = SPAN + per

        @pl.when(n == 0)
        def _():
            v_scr[...] = jnp.ones(v_scr.shape, BF16)
            for r in range(d):
                kt_scr[:, r * seg:r * seg + SPAN] = jnp.zeros((kt_scr.shape[0], SPAN), BF16)
                for hh in range(HEADS_PER_STEP):
                    v_scr[r * seg:r * seg + SPAN, 2 * hh * HEAD_DIM:(2 * hh + 1) * HEAD_DIM] = (
                        jnp.zeros((SPAN, HEAD_DIM), BF16))

        @pl.when(n > 0)
        def _():
            for r in range(d):
                kt_scr[:, r * seg:r * seg + SPAN] = kt_scr[:, r * seg + per:r * seg + per + SPAN]
                v_scr[r * seg:r * seg + SPAN, :] = v_scr[r * seg + per:r * seg + per + SPAN, :]

        for r in range(d):
            kt_scr[:, r * seg + SPAN:(r + 1) * seg] = kt_ref[:, r * per:(r + 1) * per]
            for hh in range(HEADS_PER_STEP):
                v_scr[r * seg + SPAN:(r + 1) * seg, 2 * hh * HEAD_DIM:(2 * hh + 1) * HEAD_DIM] = (
                    v_ref[r * per:(r + 1) * per, hh * HEAD_DIM:(hh + 1) * HEAD_DIM])

        for r in range(d):
            for j in range(per // SPAN):
                q_rows = slice(r * per + j * SPAN, r * per + (j + 1) * SPAN)
                keys = slice(r * seg + j * SPAN, r * seg + (j + 2) * SPAN)
                if d < MERGE_DIL:
                    assert d == 1
                    nat = slice(j * SPAN, (j + 1) * SPAN)
                else:
                    fine = d // MERGE_DIL
                    start = (r % MERGE_DIL) * (SUPER // MERGE_DIL) + fine * j * SPAN + r // MERGE_DIL
                    nat = pl.ds(start, SPAN, stride=fine) if fine > 1 else slice(start, start + SPAN)
                for hh in range(HEADS_PER_STEP):
                    cols = slice(hh * HEAD_DIM, (hh + 1) * HEAD_DIM)
                    s = _dot(q_ref[q_rows, cols], kt_scr[cols, keys]) + bias_scr[1 if j == 0 else 0]
                    m = jnp.max(s, axis=-1, keepdims=True)
                    p = jnp.exp2(s - m)
                    pv = _dot(p.astype(BF16), v_scr[keys, 2 * hh * HEAD_DIM:(2 * hh + 2) * HEAD_DIM])
                    denom = pv[:, HEAD_DIM:]
                    o_scr[g, hh, nat, :] = pv[:, :HEAD_DIM] * (1.0 / denom)
                    lse_scr[g, hh, nat, :] = m + jnp.log2(denom)

    rows = MERGE_ROWS
    for hh in range(HEADS_PER_STEP):
        for b in range(MERGE_DIL):
            for u0 in range(0, SUPER // MERGE_DIL, rows):
                natural = pl.ds(b + MERGE_DIL * u0, rows, stride=MERGE_DIL)
                ordered = slice(b * (SUPER // MERGE_DIL) + u0, b * (SUPER // MERGE_DIL) + u0 + rows)
                idx = [natural if d < MERGE_DIL else ordered for _, d in B_PATTERNS]
                lses = [lse_scr[g, hh, idx[g], :] for g in range(N_B_GROUPS)]
                m = jnp.maximum(jnp.maximum(lses[0], lses[1]), lses[2])
                es = [jnp.exp2(l - m) for l in lses]
                mixed = (es[0] * o_scr[0, hh, idx[0], :] + es[1] * o_scr[1, hh, idx[1], :]
                         + es[2] * o_scr[2, hh, idx[2], :])
                y_ref[hh, natural, :] = mixed * (1.0 / (es[0] + es[1] + es[2]))


def _attn(qkvs, bsz, seq, cast_jobs=()):
    width = HEADS_PER_STEP * HEAD_DIM
    n_super = seq // SUPER
    in_specs, operands, kt_scr, v_scr = [], [], [], []
    for (q, kt, v), (_, d) in zip(qkvs, B_PATTERNS):
        in_specs += [
            pl.BlockSpec((None, SUPER, width), lambda b, hp, n: (b, n, hp)),
            pl.BlockSpec((None, width, SUPER), lambda b, hp, n: (b, hp, n)),
            pl.BlockSpec((None, SUPER, width), lambda b, hp, n: (b, n, hp)),
        ]
        operands += [q, kt, v]
        kt_scr.append(pltpu.VMEM((width, SUPER + d * SPAN), BF16))
        v_scr.append(pltpu.VMEM((SUPER + d * SPAN, 2 * width), BF16))
    n_pairs = B_HEADS // HEADS_PER_STEP
    cast_in, cast_out, cast_shapes = _cast_specs(
        cast_jobs, bsz * n_pairs * n_super, lambda b, hp, n: (b * n_pairs + hp) * n_super + n)
    out, *casted = pl.pallas_call(
        _with_casts(_attn_kernel, len(operands), 1, len(cast_jobs)),
        grid=(bsz, n_pairs, n_super),
        in_specs=in_specs + cast_in,
        out_specs=[pl.BlockSpec((None, HEADS_PER_STEP, SUPER, HEAD_DIM),
                                lambda b, hp, n: (b, hp, n, 0))] + cast_out,
        out_shape=[jax.ShapeDtypeStruct((bsz, B_HEADS, seq, HEAD_DIM), F32)] + cast_shapes,
        scratch_shapes=kt_scr + v_scr + [
            pltpu.VMEM((N_B_GROUPS, HEADS_PER_STEP, SUPER, HEAD_DIM), F32),
            pltpu.VMEM((N_B_GROUPS, HEADS_PER_STEP, SUPER, HEAD_DIM), F32),
            pltpu.VMEM((2, SPAN, 2 * SPAN), F32),
        ],
        compiler_params=pltpu.CompilerParams(
            dimension_semantics=("arbitrary", "arbitrary", "arbitrary"),
            vmem_limit_bytes=VMEM_LIMIT_BYTES),
        name="attn",
    )(*operands, *[job[0] for job in cast_jobs])
    return out, casted


def _attn_out_kernel(x_ref, ng_ref, wz_ref, o_ref, w_out_ref, out_ref):
    x = x_ref[...]
    h = _rms_norm(x, ng_ref[...]).astype(BF16)
    z = _dot(h, wz_ref[...])
    o = jnp.concatenate([o_ref[hd] for hd in range(B_HEADS)], axis=1)
    y = (o * _silu(z)).astype(BF16)
    out_ref[...] = x + _dot(y, w_out_ref[...])


def _attn_out(x3, ng, w_in16, o, w_out16):
    bsz, seq, _ = x3.shape
    tm = LAYER_TILE
    tile = pl.BlockSpec((None, tm, D_MODEL), lambda b, t: (b, t, 0))
    return pl.pallas_call(
        _attn_out_kernel,
        grid=(bsz, seq // tm),
        in_specs=[
            tile,
            _resident((1, D_MODEL), lambda b, t: (0, 0)),
            _resident((D_MODEL, B_WIDTH), lambda b, t: (0, 3 * N_B_GROUPS)),
            pl.BlockSpec((None, B_HEADS, tm, HEAD_DIM), lambda b, t: (b, 0, t, 0)),
            _resident((B_WIDTH, D_MODEL), lambda b, t: (0, 0)),
        ],
        out_specs=tile,
        out_shape=jax.ShapeDtypeStruct((bsz, seq, D_MODEL), F32),
        compiler_params=pltpu.CompilerParams(
            dimension_semantics=("arbitrary", "arbitrary"), vmem_limit_bytes=VMEM_LIMIT_BYTES),
        name="attn_out",
    )(x3, ng.reshape(1, D_MODEL), w_in16, o, w_out16)


def _layer_b(x3, ng, w_in16, w_kt16, q_gain, k_gain, w_out16, proj_cast_jobs, attn_cast_jobs):
    bsz, seq, _ = x3.shape
    inv_freq = jnp.power(jnp.float32(ROPE_THETA), -jnp.arange(ROPE_HALF, dtype=F32) / ROPE_HALF)
    inv_freq = inv_freq.reshape(ROPE_HALF, 1)
    qkvs, casted = [], []
    for g, (window, dilation) in enumerate(B_PATTERNS):
        assert window // dilation == SPAN
        qkv, done = _attn_proj(x3, ng, inv_freq, w_in16, w_kt16, q_gain[g], k_gain[g], g, dilation,
                               proj_cast_jobs[g])
        qkvs.append(qkv)
        casted += done
    o, done = _attn(qkvs, bsz, seq, attn_cast_jobs)
    return _attn_out(x3, ng, w_in16, o, w_out16), casted + done


def kernel(x, norm_gain, a_w_in, a_v_gain, a_w_s, a_b_s, a_w_out, b_w_in, b_q_gain, b_k_gain,
           b_w_out, c_w_in, c_w_grp, c_scale, c_w_out):
    bsz, seq, d_model = x.shape
    assert d_model == D_MODEL and seq % SUPER == 0 and SUPER % TOKEN_TILE == 0
    assert norm_gain.shape[0] == 4 and a_w_in.shape[0] == 2
    c_w_grp2 = c_w_grp.reshape(c_w_grp.shape[0], N_POOL * C_GROUP, C_GROUP)
    k_cols = (N_B_GROUPS * B_WIDTH, N_B_GROUPS * B_WIDTH)

    x2, (b_w_in16, b_w_out16, b_w_kt16) = _layer_a(
        x.reshape(bsz * seq, D_MODEL), norm_gain[0], a_w_in[0].astype(BF16), a_v_gain[0], a_w_s[0],
        a_b_s[0], a_w_out[0].astype(BF16),
        cast_jobs=[(b_w_in, 0, None), (b_w_out, 0, None), (b_w_in, 0, k_cols)])
    x, (c_w_in16, c_w_grp16, c_w_out16, a_w_in16, a_w_out16) = _layer_b(
        x2.reshape(bsz, seq, D_MODEL), norm_gain[1], b_w_in16, b_w_kt16, b_q_gain[0], b_k_gain[0],
        b_w_out16,
        proj_cast_jobs=[[(c_w_in, 0, None)], [(c_w_grp2, 0, None)], [(c_w_out, 0, None)]],
        attn_cast_jobs=[(a_w_in, 1, None), (a_w_out, 1, None)])
    x = _layer_c(x, norm_gain[2], c_w_in16, c_w_grp16.reshape(N_POOL, C_GROUP, C_GROUP), c_scale[0],
                 c_w_out16)
    x2, _ = _layer_a(x.reshape(bsz * seq, D_MODEL), norm_gain[3], a_w_in16, a_v_gain[1], a_w_s[1],
                     a_b_s[1], a_w_out16)
    return x2.reshape(bsz, seq, D_MODEL)
```

```python
import functools
import math

import jax
import jax.numpy as jnp
import numpy as np
from jax import lax
from jax.experimental import pallas as pl
from jax.experimental.pallas import tpu as pltpu

D_MODEL = 1024
EPS = 1e-6
CHUNK = 128
A_WIDTH = 2 * D_MODEL
A_GROUPS = 8
A_GROUP_DIM = A_WIDTH // A_GROUPS
HEAD_DIM = 128
B_HEADS = D_MODEL // HEAD_DIM
B_PATTERNS = ((128, 1), (512, 4), (2048, 16))
N_B_GROUPS = len(B_PATTERNS)
B_WIDTH = B_HEADS * HEAD_DIM
ROPE_DIM = HEAD_DIM // 4
ROPE_HALF = ROPE_DIM // 2
ROPE_THETA = 500000.0
Q_PRESCALE = math.log2(math.e) / math.sqrt(HEAD_DIM)
SPAN = 128
SUPER = SPAN * max(d for _, d in B_PATTERNS)
HEADS_PER_STEP = 2
MERGE_DIL = 4
POOL_SIZES = (2, 4, 8, 16)
N_POOL = len(POOL_SIZES)
C_WIDTH = 2 * D_MODEL
C_GROUP = C_WIDTH // N_POOL
HALO = 16

LANES = 128
BF16_ROWS = 16
MERGE_ROWS = 256
VMEM_LIMIT_BYTES = 60 * 1024 * 1024

TOKEN_TILE = 512
LAYER_TILE = 1024
MASK_VALUE = -1e30

F32 = jnp.float32
BF16 = jnp.bfloat16


def _resident(shape, index_map):
    return pl.BlockSpec(shape, index_map, pipeline_mode=pl.Buffered(1))


def _rms_norm(x, gain):
    return x * lax.rsqrt(jnp.mean(x * x, axis=-1, keepdims=True) + EPS) * gain


def _dot(a, b):
    return jnp.dot(a, b, preferred_element_type=F32)


def _dot_nt(a, b):
    return lax.dot_general(a, b, (((1,), (1,)), ((), ())), preferred_element_type=F32)


def _silu(z):
    return z * jax.nn.sigmoid(z)


def _cast_specs(jobs, n_steps, flat_step):
    in_specs, out_specs, out_shapes = [], [], []
    for arr, layer, col_range in jobs:
        _, rows, cols = arr.shape
        if col_range is None:
            n_chunks = n_steps
            chunk = rows // n_chunks
            assert chunk * n_chunks == rows and chunk % BF16_ROWS == 0, (arr.shape, n_steps)
            last = n_chunks - 1
            in_specs.append(pl.BlockSpec(
                (None, chunk, cols),
                lambda *ids, layer=layer, last=last: (layer, jnp.minimum(flat_step(*ids), last), 0)))
            out_specs.append(pl.BlockSpec(
                (chunk, cols), lambda *ids, last=last: (jnp.minimum(flat_step(*ids), last), 0)))
            out_shapes.append(jax.ShapeDtypeStruct((rows, cols), BF16))
        else:
            start, width = col_range
            n_chunks = rows // LANES
            assert n_chunks <= n_steps and start % width == 0, (arr.shape, n_steps, col_range)
            last = n_chunks - 1
            in_specs.append(pl.BlockSpec(
                (None, LANES, width),
                lambda *ids, layer=layer, last=last, blk=start // width: (
                    layer, jnp.minimum(flat_step(*ids), last), blk)))
            out_specs.append(pl.BlockSpec(
                (width, LANES), lambda *ids, last=last: (0, jnp.minimum(flat_step(*ids), last))))
            out_shapes.append(jax.ShapeDtypeStruct((width, rows), BF16))
    return in_specs, out_specs, out_shapes


def _with_casts(body, n_in, n_out, n_jobs):
    if n_jobs == 0:
        return body

    def wrapped(*refs):
        ins, cast_in = refs[:n_in], refs[n_in:n_in + n_jobs]
        outs = refs[n_in + n_jobs:n_in + n_jobs + n_out]
        cast_out = refs[n_in + n_jobs + n_out:n_in + 2 * n_jobs + n_out]
        body(*ins, *outs, *refs[n_in + 2 * n_jobs + n_out:])
        for src, dst in zip(cast_in, cast_out):
            transposed = src.shape != dst.shape
            dst[...] = (src[...].T if transposed else src[...]).astype(BF16)

    return wrapped


def _layer_a_kernel(x_ref, ng_ref, w_in_ref, vg_ref, ws_ref, bs_ref, w_out_ref,
                    o_ref, vn_ref, y_ref):
    x = x_ref[...]
    h = _rms_norm(x, ng_ref[...]).astype(BF16)
    v = _dot(h, w_in_ref[:, A_WIDTH:2 * A_WIDTH])
    vn_ref[...] = _rms_norm(v, vg_ref[...]).astype(BF16)
    row = lax.broadcasted_iota(jnp.int32, (CHUNK, CHUNK), 0)
    col = lax.broadcasted_iota(jnp.int32, (CHUNK, CHUNK), 1)
    causal = col <= row
    n_chunks = x.shape[0] // CHUNK
    for g in range(A_GROUPS):
        cols = slice(g * A_GROUP_DIM, (g + 1) * A_GROUP_DIM)
        u = _dot(h, w_in_ref[:, cols])
        z = _dot(h, w_in_ref[:, 2 * A_WIDTH + g * A_GROUP_DIM:2 * A_WIDTH + (g + 1) * A_GROUP_DIM])
        ws = jnp.where(causal, ws_ref[g], 0.0).astype(BF16)
        bias = bs_ref[:, g:g + 1]
        mixed = jnp.concatenate(
            [_dot(ws, vn_ref[c * CHUNK:(c + 1) * CHUNK, cols]) + bias for c in range(n_chunks)],
            axis=0)
        y_ref[:, cols] = (u * mixed * _silu(z)).astype(BF16)
    o_ref[...] = x + _dot(y_ref[...], w_out_ref[...])


def _layer_a(x2, ng, w_in16, v_gain, w_s, b_s, w_out16, cast_jobs=()):
    tokens = x2.shape[0]
    tm = LAYER_TILE
    n_steps = tokens // tm
    cast_in, cast_out, cast_shapes = _cast_specs(cast_jobs, n_steps, lambda i: i)
    out, *casted = pl.pallas_call(
        _with_casts(_layer_a_kernel, 7, 1, len(cast_jobs)),
        grid=(n_steps,),
        in_specs=[
            pl.BlockSpec((tm, D_MODEL), lambda i: (i, 0)),
            _resident((1, D_MODEL), lambda i: (0, 0)),
            _resident((D_MODEL, 3 * A_WIDTH), lambda i: (0, 0)),
            _resident((1, A_WIDTH), lambda i: (0, 0)),
            _resident((A_GROUPS, CHUNK, CHUNK), lambda i: (0, 0, 0)),
            _resident((CHUNK, A_GROUPS), lambda i: (0, 0)),
            _resident((A_WIDTH, D_MODEL), lambda i: (0, 0)),
        ] + cast_in,
        out_specs=[pl.BlockSpec((tm, D_MODEL), lambda i: (i, 0))] + cast_out,
        out_shape=[jax.ShapeDtypeStruct((tokens, D_MODEL), F32)] + cast_shapes,
        scratch_shapes=[pltpu.VMEM((tm, A_WIDTH), BF16), pltpu.VMEM((tm, A_WIDTH), BF16)],
        compiler_params=pltpu.CompilerParams(
            dimension_semantics=("arbitrary",), vmem_limit_bytes=VMEM_LIMIT_BYTES),
        name="layer_a",
    )(x2, ng.reshape(1, D_MODEL), w_in16, v_gain.reshape(1, A_WIDTH), w_s, b_s.T, w_out16,
      *[job[0] for job in cast_jobs])
    return out, casted


def _layer_c_kernel(x_ref, ng_ref, w_in_ref, w_grp_ref, scale_ref, w_out_ref,
                    o_ref, ext_ref, y_ref):
    t = pl.program_id(1)
    tm = x_ref.shape[0]
    x = x_ref[...]
    h = _rms_norm(x, ng_ref[...]).astype(BF16)

    @pl.when(t == 0)
    def _():
        ext_ref[0:HALO, :] = jnp.zeros((HALO, C_WIDTH), F32)

    @pl.when(t > 0)
    def _():
        ext_ref[0:HALO, :] = ext_ref[tm:tm + HALO, :]

    ext_ref[HALO:HALO + tm, :] = _dot(h, w_in_ref[:, 0:C_WIDTH])
    pos = t * tm + lax.broadcasted_iota(jnp.int32, (tm, 1), 0)
    for g, window in enumerate(POOL_SIZES):
        cols = slice(g * C_GROUP, (g + 1) * C_GROUP)
        xc = ext_ref[HALO:HALO + tm, cols]
        acc = xc
        for j in range(1, window):
            acc = acc + ext_ref[HALO - j:HALO - j + tm, cols]
        cnt = jnp.minimum(pos + 1, window).astype(F32)
        diff = (acc / cnt - xc).astype(BF16)
        mixed = _dot(diff, w_grp_ref[g]) * scale_ref[:, cols]
        z = _dot(h, w_in_ref[:, C_WIDTH + g * C_GROUP:C_WIDTH + (g + 1) * C_GROUP])
        y_ref[:, cols] = (mixed * _silu(z)).astype(BF16)
    o_ref[...] = x + _dot(y_ref[...], w_out_ref[...])


def _layer_c(x3, ng, w_in16, w_grp16, scale, w_out16):
    bsz, seq, _ = x3.shape
    tm = LAYER_TILE
    return pl.pallas_call(
        _layer_c_kernel,
        grid=(bsz, seq // tm),
        in_specs=[
            pl.BlockSpec((None, tm, D_MODEL), lambda b, t: (b, t, 0)),
            _resident((1, D_MODEL), lambda b, t: (0, 0)),
            _resident((D_MODEL, 2 * C_WIDTH), lambda b, t: (0, 0)),
            _resident((N_POOL, C_GROUP, C_GROUP), lambda b, t: (0, 0, 0)),
            _resident((1, C_WIDTH), lambda b, t: (0, 0)),
            _resident((C_WIDTH, D_MODEL), lambda b, t: (0, 0)),
        ],
        out_specs=pl.BlockSpec((None, tm, D_MODEL), lambda b, t: (b, t, 0)),
        out_shape=jax.ShapeDtypeStruct((bsz, seq, D_MODEL), F32),
        scratch_shapes=[pltpu.VMEM((tm + HALO, C_WIDTH), F32), pltpu.VMEM((tm, C_WIDTH), BF16)],
        compiler_params=pltpu.CompilerParams(
            dimension_semantics=("arbitrary", "arbitrary"), vmem_limit_bytes=VMEM_LIMIT_BYTES),
        name="layer_c",
    )(x3, ng.reshape(1, D_MODEL), w_in16, w_grp16, scale.reshape(1, C_WIDTH), w_out16)


def _attn_proj_kernel(x_ref, ng_ref, freq_ref, wq_ref, wkt_ref, wv_ref, qg_ref, kg_ref,
                      q_ref, kt_ref, v_ref, h_scr, q_a, kt_a, q_b, kt_b,
                      *, dilation, n_steps, n_super):
    t = pl.program_id(0)
    rows = q_ref.shape[0]
    n_chunk = SUPER // rows
    per = SUPER // dilation
    n_cols = D_MODEL // LANES
    s = t - n_chunk
    cur = jnp.clip(s, 0, n_steps - 1)
    c = cur % n_chunk
    slot = (cur // n_chunk) % 2
    prev = jnp.clip(s - 1, 0, n_steps - 1)
    c_prev = prev % n_chunk
    n_prev = (prev // n_chunk) % n_super

    @pl.when(s == 0)
    def _():
        q_b[...] = jnp.zeros(q_b.shape, q_b.dtype)
        kt_b[...] = jnp.zeros(kt_b.shape, kt_b.dtype)

    def normalise_tile():
        tile = jnp.minimum(t, n_steps - 1)
        rs = pl.ds(pl.multiple_of((tile % n_chunk) * rows, rows), rows)
        hh = _rms_norm(x_ref[...], ng_ref[...])
        for j in range(n_cols):
            h_scr[(tile // n_chunk) % 2, j, rs, :] = hh[:, j * LANES:(j + 1) * LANES]

    def gather(j):
        if dilation == 1:
            return h_scr[slot, j, pl.ds(pl.multiple_of(c * rows, rows), rows), :]
        run = min(per, rows)
        first = c * (rows // run)
        if per > rows:
            raise NotImplementedError("a residue class longer than the chunk needs a row offset")
        pieces = [h_scr[slot, j, pl.ds(first + rr, run, stride=dilation), :] for rr in range(rows // run)]
        return pieces[0] if len(pieces) == 1 else jnp.concatenate(pieces, axis=0)

    def project(q_raw, kt_raw):
        h = jnp.concatenate([gather(j) for j in range(n_cols)], axis=1).astype(BF16)
        q_raw[...] = _dot(h, wq_ref[...])
        kt_raw[...] = _dot_nt(wkt_ref[...], h)
        v_ref[...] = _dot(h, wv_ref[...]).astype(BF16)

    def finish(q_raw, kt_raw):
        p = c_prev * rows + lax.broadcasted_iota(jnp.int32, (1, rows), 1)
        residue = lax.shift_right_logical(p, per.bit_length() - 1)
        strided = jnp.bitwise_and(p, per - 1)
        pos = n_prev * SUPER + strided * dilation + residue
        ang = pos.astype(F32) * freq_ref[...]
        cos_t, sin_t = jnp.cos(ang), jnp.sin(ang)

        for hd in range(B_HEADS):
            slab = kt_raw[hd * HEAD_DIM:(hd + 1) * HEAD_DIM, :]
            kn = slab * lax.rsqrt(jnp.mean(slab * slab, axis=0, keepdims=True) + EPS) * kg_ref[...]
            x1, x2 = kn[0:ROPE_HALF, :], kn[ROPE_HALF:ROPE_DIM, :]
            rotated = jnp.concatenate(
                [x1 * cos_t - x2 * sin_t, x2 * cos_t + x1 * sin_t, kn[ROPE_DIM:, :]], axis=0)
            kt_ref[hd * HEAD_DIM:(hd + 1) * HEAD_DIM, :] = rotated.astype(BF16)

        rest = HEAD_DIM - ROPE_DIM
        cos_tab = jnp.concatenate([cos_t, cos_t, jnp.ones((rest, rows), F32)], axis=0).T * Q_PRESCALE
        sin_tab = jnp.concatenate([-sin_t, sin_t, jnp.zeros((rest, rows), F32)], axis=0).T * Q_PRESCALE
        lane = lax.broadcasted_iota(jnp.int32, (rows, HEAD_DIM), 1)
        for hd in range(B_HEADS):
            cols = slice(hd * HEAD_DIM, (hd + 1) * HEAD_DIM)
            qn = _rms_norm(q_raw[:, cols], qg_ref[...])
            partner = jnp.where(lane < ROPE_HALF, pltpu.roll(qn, HEAD_DIM - ROPE_HALF, 1),
                                pltpu.roll(qn, ROPE_HALF, 1))
            q_ref[:, cols] = (qn * cos_tab + partner * sin_tab).astype(BF16)

    @pl.when(s < 0)
    def _():
        normalise_tile()

    @pl.when((s >= 0) & (s % 2 == 0))
    def _():
        project(q_a, kt_a)
        finish(q_b, kt_b)
        normalise_tile()

    @pl.when((s >= 0) & (s % 2 == 1))
    def _():
        project(q_b, kt_b)
        finish(q_a, kt_a)
        normalise_tile()


def _attn_proj(x3, ng, inv_freq, w_in16, w_kt16, q_gain, k_gain, group, dilation, cast_jobs=()):
    bsz, seq, _ = x3.shape
    rows = TOKEN_TILE
    n_super, n_chunk = seq // SUPER, SUPER // rows
    per_batch = n_super * n_chunk
    n_steps = bsz * per_batch
    tile = lambda t: jnp.minimum(t, n_steps - 1)
    cur = lambda t: jnp.clip(t - n_chunk, 0, n_steps - 1)
    prev = lambda t: jnp.clip(t - n_chunk - 1, 0, n_steps - 1)
    raw = [pltpu.VMEM((rows, B_WIDTH), F32), pltpu.VMEM((B_WIDTH, rows), F32)]
    cast_in, cast_out, cast_shapes = _cast_specs(cast_jobs, n_steps, lambda s: s)
    body = functools.partial(_attn_proj_kernel, dilation=dilation, n_steps=n_steps, n_super=n_super)
    q, kt, v, *casted = pl.pallas_call(
        _with_casts(body, 8, 3, len(cast_jobs)),
        grid=(n_chunk + n_steps + 1,),
        in_specs=[
            pl.BlockSpec((None, rows, D_MODEL), lambda t: (tile(t) // per_batch, tile(t) % per_batch, 0)),
            _resident((1, D_MODEL), lambda s: (0, 0)),
            _resident((ROPE_HALF, 1), lambda s: (0, 0)),
            _resident((D_MODEL, B_WIDTH), lambda s: (0, group)),
            _resident((B_WIDTH, D_MODEL), lambda s: (group, 0)),
            _resident((D_MODEL, B_WIDTH), lambda s: (0, 2 * N_B_GROUPS + group)),
            _resident((1, HEAD_DIM), lambda s: (0, 0)),
            _resident((HEAD_DIM, 1), lambda s: (0, 0)),
        ] + cast_in,
        out_specs=[
            pl.BlockSpec((None, rows, B_WIDTH), lambda s: (prev(s) // per_batch, prev(s) % per_batch, 0)),
            pl.BlockSpec((None, B_WIDTH, rows), lambda s: (prev(s) // per_batch, 0, prev(s) % per_batch)),
            pl.BlockSpec((None, rows, B_WIDTH), lambda s: (cur(s) // per_batch, cur(s) % per_batch, 0)),
        ] + cast_out,
        out_shape=[
            jax.ShapeDtypeStruct((bsz, seq, B_WIDTH), BF16),
            jax.ShapeDtypeStruct((bsz, B_WIDTH, seq), BF16),
            jax.ShapeDtypeStruct((bsz, seq, B_WIDTH), BF16),
        ] + cast_shapes,
        scratch_shapes=[pltpu.VMEM((2, D_MODEL // LANES, SUPER, LANES), F32)] + raw + raw,
        compiler_params=pltpu.CompilerParams(
            dimension_semantics=("arbitrary",), vmem_limit_bytes=VMEM_LIMIT_BYTES),
        name=f"attn_proj_d{dilation}",
    )(x3, ng.reshape(1, D_MODEL), inv_freq, w_in16, w_kt16, w_in16,
      q_gain.reshape(1, HEAD_DIM), k_gain.reshape(HEAD_DIM, 1), *[job[0] for job in cast_jobs])
    return (q, kt, v), casted


def _attn_kernel(*refs):
    qkv_refs = refs[:3 * N_B_GROUPS]
    y_ref = refs[3 * N_B_GROUPS]
    scratch = refs[3 * N_B_GROUPS + 1:]
    kt_scrs, v_scrs = scratch[0:N_B_GROUPS], scratch[N_B_GROUPS:2 * N_B_GROUPS]
    o_scr, lse_scr, bias_scr = scratch[2 * N_B_GROUPS:]
    n = pl.program_id(2)

    qi = lax.broadcasted_iota(jnp.int32, (SPAN, 2 * SPAN), 0)
    ki = lax.broadcasted_iota(jnp.int32, (SPAN, 2 * SPAN), 1)
    band = (ki >= qi) & (ki <= qi + SPAN)
    bias_scr[0] = jnp.where(band, 0.0, MASK_VALUE)
    bias_scr[1] = jnp.where(band & (ki >= jnp.where(n > 0, 0, SPAN)), 0.0, MASK_VALUE)

    for g, (_, d) in enumerate(B_PATTERNS):
        q_ref, kt_ref, v_ref = qkv_refs[3 * g:3 * g + 3]
        kt_scr, v_scr = kt_scrs[g], v_scrs[g]
        per = SUPER // d
        seg = SPAN + per

        @pl.when(n == 0)
        def _():
            v_scr[...] = jnp.ones(v_scr.shape, BF16)
            for r in range(d):
                kt_scr[:, r * seg:r * seg + SPAN] = jnp.zeros((kt_scr.shape[0], SPAN), BF16)
                for hh in range(HEADS_PER_STEP):
                    v_scr[r * seg:r * seg + SPAN, 2 * hh * HEAD_DIM:(2 * hh + 1) * HEAD_DIM] = (
                        jnp.zeros((SPAN, HEAD_DIM), BF16))

        @pl.when(n > 0)
        def _():
            for r in range(d):
                kt_scr[:, r * seg:r * seg + SPAN] = kt_scr[:, r * seg + per:r * seg + per + SPAN]
                v_scr[r * seg:r * seg + SPAN, :] = v_scr[r * seg + per:r * seg + per + SPAN, :]

        for r in range(d):
            kt_scr[:, r * seg + SPAN:(r + 1) * seg] = kt_ref[:, r * per:(r + 1) * per]
            for hh in range(HEADS_PER_STEP):
                v_scr[r * seg + SPAN:(r + 1) * seg, 2 * hh * HEAD_DIM:(2 * hh + 1) * HEAD_DIM] = (
                    v_ref[r * per:(r + 1) * per, hh * HEAD_DIM:(hh + 1) * HEAD_DIM])

        for r in range(d):
            for j in range(per // SPAN):
                q_rows = slice(r * per + j * SPAN, r * per + (j + 1) * SPAN)
                keys = slice(r * seg + j * SPAN, r * seg + (j + 2) * SPAN)
                if d < MERGE_DIL:
                    assert d == 1
                    nat = slice(j * SPAN, (j + 1) * SPAN)
                else:
                    fine = d // MERGE_DIL
                    start = (r % MERGE_DIL) * (SUPER // MERGE_DIL) + fine * j * SPAN + r // MERGE_DIL
                    nat = pl.ds(start, SPAN, stride=fine) if fine > 1 else slice(start, start + SPAN)
                for hh in range(HEADS_PER_STEP):
                    cols = slice(hh * HEAD_DIM, (hh + 1) * HEAD_DIM)
                    s = _dot(q_ref[q_rows, cols], kt_scr[cols, keys]) + bias_scr[1 if j == 0 else 0]
                    m = jnp.max(s, axis=-1, keepdims=True)
                    p = jnp.exp2(s - m)
                    pv = _dot(p.astype(BF16), v_scr[keys, 2 * hh * HEAD_DIM:(2 * hh + 2) * HEAD_DIM])
                    denom = pv[:, HEAD_DIM:]
                    o_scr[g, hh, nat, :] = pv[:, :HEAD_DIM] * (1.0 / denom)
                    lse_scr[g, hh, nat, :] = m + jnp.log2(denom)

    rows = MERGE_ROWS
    for hh in range(HEADS_PER_STEP):
        for b in range(MERGE_DIL):
            for u0 in range(0, SUPER // MERGE_DIL, rows):
                natural = pl.ds(b + MERGE_DIL * u0, rows, stride=MERGE_DIL)
                ordered = slice(b * (SUPER // MERGE_DIL) + u0, b * (SUPER // MERGE_DIL) + u0 + rows)
                idx = [natural if d < MERGE_DIL else ordered for _, d in B_PATTERNS]
                lses = [lse_scr[g, hh, idx[g], :] for g in range(N_B_GROUPS)]
                m = jnp.maximum(jnp.maximum(lses[0], lses[1]), lses[2])
                es = [jnp.exp2(l - m) for l in lses]
                mixed = (es[0] * o_scr[0, hh, idx[0], :] + es[1] * o_scr[1, hh, idx[1], :]
                         + es[2] * o_scr[2, hh, idx[2], :])
                y_ref[hh, natural, :] = mixed * (1.0 / (es[0] + es[1] + es[2]))


def _attn(qkvs, bsz, seq, cast_jobs=()):
    width = HEADS_PER_STEP * HEAD_DIM
    n_super = seq // SUPER
    in_specs, operands, kt_scr, v_scr = [], [], [], []
    for (q, kt, v), (_, d) in zip(qkvs, B_PATTERNS):
        in_specs += [
            pl.BlockSpec((None, SUPER, width), lambda b, hp, n: (b, n, hp)),
            pl.BlockSpec((None, width, SUPER), lambda b, hp, n: (b, hp, n)),
            pl.BlockSpec((None, SUPER, width), lambda b, hp, n: (b, n, hp)),
        ]
        operands += [q, kt, v]
        kt_scr.append(pltpu.VMEM((width, SUPER + d * SPAN), BF16))
        v_scr.append(pltpu.VMEM((SUPER + d * SPAN, 2 * width), BF16))
    n_pairs = B_HEADS // HEADS_PER_STEP
    cast_in, cast_out, cast_shapes = _cast_specs(
        cast_jobs, bsz * n_pairs * n_super, lambda b, hp, n: (b * n_pairs + hp) * n_super + n)
    out, *casted = pl.pallas_call(
        _with_casts(_attn_kernel, len(operands), 1, len(cast_jobs)),
        grid=(bsz, n_pairs, n_super),
        in_specs=in_specs + cast_in,
        out_specs=[pl.BlockSpec((None, HEADS_PER_STEP, SUPER, HEAD_DIM),
                                lambda b, hp, n: (b, hp, n, 0))] + cast_out,
        out_shape=[jax.ShapeDtypeStruct((bsz, B_HEADS, seq, HEAD_DIM), F32)] + cast_shapes,
        scratch_shapes=kt_scr + v_scr + [
            pltpu.VMEM((N_B_GROUPS, HEADS_PER_STEP, SUPER, HEAD_DIM), F32),
            pltpu.VMEM((N_B_GROUPS, HEADS_PER_STEP, SUPER, HEAD_DIM), F32),
            pltpu.VMEM((2, SPAN, 2 * SPAN), F32),
        ],
        compiler_params=pltpu.CompilerParams(
            dimension_semantics=("arbitrary", "arbitrary", "arbitrary"),
            vmem_limit_bytes=VMEM_LIMIT_BYTES),
        name="attn",
    )(*operands, *[job[0] for job in cast_jobs])
    return out, casted


def _attn_out_kernel(x_ref, ng_ref, wz_ref, o_ref, w_out_ref, out_ref):
    x = x_ref[...]
    h = _rms_norm(x, ng_ref[...]).astype(BF16)
    z = _dot(h, wz_ref[...])
    o = jnp.concatenate([o_ref[hd] for hd in range(B_HEADS)], axis=1)
    y = (o * _silu(z)).astype(BF16)
    out_ref[...] = x + _dot(y, w_out_ref[...])


def _attn_out(x3, ng, w_in16, o, w_out16):
    bsz, seq, _ = x3.shape
    tm = LAYER_TILE
    tile = pl.BlockSpec((None, tm, D_MODEL), lambda b, t: (b, t, 0))
    return pl.pallas_call(
        _attn_out_kernel,
        grid=(bsz, seq // tm),
        in_specs=[
            tile,
            _resident((1, D_MODEL), lambda b, t: (0, 0)),
            _resident((D_MODEL, B_WIDTH), lambda b, t: (0, 3 * N_B_GROUPS)),
            pl.BlockSpec((None, B_HEADS, tm, HEAD_DIM), lambda b, t: (b, 0, t, 0)),
            _resident((B_WIDTH, D_MODEL), lambda b, t: (0, 0)),
        ],
        out_specs=tile,
        out_shape=jax.ShapeDtypeStruct((bsz, seq, D_MODEL), F32),
        compiler_params=pltpu.CompilerParams(
            dimension_semantics=("arbitrary", "arbitrary"), vmem_limit_bytes=VMEM_LIMIT_BYTES),
        name="attn_out",
    )(x3, ng.reshape(1, D_MODEL), w_in16, o, w_out16)


def _layer_b(x3, ng, w_in16, w_kt16, q_gain, k_gain, w_out16, proj_cast_jobs, attn_cast_jobs):
    bsz, seq, _ = x3.shape
    inv_freq = jnp.power(jnp.float32(ROPE_THETA), -jnp.arange(ROPE_HALF, dtype=F32) / ROPE_HALF)
    inv_freq = inv_freq.reshape(ROPE_HALF, 1)
    qkvs, casted = [], []
    for g, (window, dilation) in enumerate(B_PATTERNS):
        assert window // dilation == SPAN
        qkv, done = _attn_proj(x3, ng, inv_freq, w_in16, w_kt16, q_gain[g], k_gain[g], g, dilation,
                               proj_cast_jobs[g])
        qkvs.append(qkv)
        casted += done
    o, done = _attn(qkvs, bsz, seq, attn_cast_jobs)
    return _attn_out(x3, ng, w_in16, o, w_out16), casted + done


def kernel(x, norm_gain, a_w_in, a_v_gain, a_w_s, a_b_s, a_w_out, b_w_in, b_q_gain, b_k_gain,
           b_w_out, c_w_in, c_w_grp, c_scale, c_w_out):
    bsz, seq, d_model = x.shape
    assert d_model == D_MODEL and seq % SUPER == 0 and SUPER % TOKEN_TILE == 0
    assert norm_gain.shape[0] == 4 and a_w_in.shape[0] == 2
    c_w_grp2 = c_w_grp.reshape(c_w_grp.shape[0], N_POOL * C_GROUP, C_GROUP)
    k_cols = (N_B_GROUPS * B_WIDTH, N_B_GROUPS * B_WIDTH)

    x2, (b_w_in16, b_w_out16, b_w_kt16) = _layer_a(
        x.reshape(bsz * seq, D_MODEL), norm_gain[0], a_w_in[0].astype(BF16), a_v_gain[0], a_w_s[0],
        a_b_s[0], a_w_out[0].astype(BF16),
        cast_jobs=[(b_w_in, 0, None), (b_w_out, 0, None), (b_w_in, 0, k_cols)])
    x, (c_w_in16, c_w_grp16, c_w_out16, a_w_in16, a_w_out16) = _layer_b(
        x2.reshape(bsz, seq, D_MODEL), norm_gain[1], b_w_in16, b_w_kt16, b_q_gain[0], b_k_gain[0],
        b_w_out16,
        proj_cast_jobs=[[(c_w_in, 0, None)], [(c_w_grp2, 0, None)], [(c_w_out, 0, None)]],
        attn_cast_jobs=[(a_w_in, 1, None), (a_w_out, 1, None)])
    x = _layer_c(x, norm_gain[2], c_w_in16, c_w_grp16.reshape(N_POOL, C_GROUP, C_GROUP), c_scale[0],
                 c_w_out16)
    x2, _ = _layer_a(x.reshape(bsz * seq, D_MODEL), norm_gain[3], a_w_in16, a_v_gain[1], a_w_s[1],
                     a_b_s[1], a_w_out16)
    return x2.reshape(bsz, seq, D_MODEL)
```

```python
import functools
import math

import jax
import jax.numpy as jnp
import numpy as np
from jax import lax
from jax.experimental import pallas as pl
from jax.experimental.pallas import tpu as pltpu

D_MODEL = 1024
EPS = 1e-6
CHUNK = 128
A_WIDTH = 2 * D_MODEL
A_GROUPS = 8
A_GROUP_DIM = A_WIDTH // A_GROUPS
HEAD_DIM = 128
B_HEADS = D_MODEL // HEAD_DIM
B_PATTERNS = ((128, 1), (512, 4), (2048, 16))
N_B_GROUPS = len(B_PATTERNS)
B_WIDTH = B_HEADS * HEAD_DIM
ROPE_DIM = HEAD_DIM // 4
ROPE_HALF = ROPE_DIM // 2
ROPE_THETA = 500000.0
Q_PRESCALE = math.log2(math.e) / math.sqrt(HEAD_DIM)
SPAN = 128
SUPER = SPAN * max(d for _, d in B_PATTERNS)
HEADS_PER_STEP = 2
MERGE_DIL = 4
POOL_SIZES = (2, 4, 8, 16)
N_POOL = len(POOL_SIZES)
C_WIDTH = 2 * D_MODEL
C_GROUP = C_WIDTH // N_POOL
HALO = 16

LANES = 128
BF16_ROWS = 16
MERGE_ROWS = 256
VMEM_LIMIT_BYTES = 60 * 1024 * 1024

TOKEN_TILE = 512
LAYER_TILE = 1024
MASK_VALUE = -1e30

F32 = jnp.float32
BF16 = jnp.bfloat16


def _resident(shape, index_map):
    return pl.BlockSpec(shape, index_map, pipeline_mode=pl.Buffered(1))


def _rms_norm(x, gain):
    return x * lax.rsqrt(jnp.mean(x * x, axis=-1, keepdims=True) + EPS) * gain


def _dot(a, b):
    return jnp.dot(a, b, preferred_element_type=F32)


def _dot_nt(a, b):
    return lax.dot_general(a, b, (((1,), (1,)), ((), ())), preferred_element_type=F32)


def _silu(z):
    return z * jax.nn.sigmoid(z)


def _cast_specs(jobs, n_steps, flat_step):
    in_specs, out_specs, out_shapes = [], [], []
    for arr, layer, col_range in jobs:
        _, rows, cols = arr.shape
        if col_range is None:
            n_chunks = n_steps
            chunk = rows // n_chunks
            assert chunk * n_chunks == rows and chunk % BF16_ROWS == 0, (arr.shape, n_steps)
            last = n_chunks - 1
            in_specs.append(pl.BlockSpec(
                (None, chunk, cols),
                lambda *ids, layer=layer, last=last: (layer, jnp.minimum(flat_step(*ids), last), 0)))
            out_specs.append(pl.BlockSpec(
                (chunk, cols), lambda *ids, last=last: (jnp.minimum(flat_step(*ids), last), 0)))
            out_shapes.append(jax.ShapeDtypeStruct((rows, cols), BF16))
        else:
            start, width = col_range
            n_chunks = rows // LANES
            assert n_chunks <= n_steps and start % width == 0, (arr.shape, n_steps, col_range)
            last = n_chunks - 1
            in_specs.append(pl.BlockSpec(
                (None, LANES, width),
                lambda *ids, layer=layer, last=last, blk=start // width: (
                    layer, jnp.minimum(flat_step(*ids), last), blk)))
            out_specs.append(pl.BlockSpec(
                (width, LANES), lambda *ids, last=last: (0, jnp.minimum(flat_step(*ids), last))))
            out_shapes.append(jax.ShapeDtypeStruct((width, rows), BF16))
    return in_specs, out_specs, out_shapes


def _with_casts(body, n_in, n_out, n_jobs):
    if n_jobs == 0:
        return body

    def wrapped(*refs):
        ins, cast_in = refs[:n_in], refs[n_in:n_in + n_jobs]
        outs = refs[n_in + n_jobs:n_in + n_jobs + n_out]
        cast_out = refs[n_in + n_jobs + n_out:n_in + 2 * n_jobs + n_out]
        body(*ins, *outs, *refs[n_in + 2 * n_jobs + n_out:])
        for src, dst in zip(cast_in, cast_out):
            transposed = src.shape != dst.shape
            dst[...] = (src[...].T if transposed else src[...]).astype(BF16)

    return wrapped


def _layer_a_kernel(x_ref, ng_ref, w_in_ref, vg_ref, ws_ref, bs_ref, w_out_ref,
                    o_ref, vn_ref, y_ref):
    x = x_ref[...]
    h = _rms_norm(x, ng_ref[...]).astype(BF16)
    v = _dot(h, w_in_ref[:, A_WIDTH:2 * A_WIDTH])
    vn_ref[...] = _rms_norm(v, vg_ref[...]).astype(BF16)
    row = lax.broadcasted_iota(jnp.int32, (CHUNK, CHUNK), 0)
    col = lax.broadcasted_iota(jnp.int32, (CHUNK, CHUNK), 1)
    causal = col <= row
    n_chunks = x.shape[0] // CHUNK
    for g in range(A_GROUPS):
        cols = slice(g * A_GROUP_DIM, (g + 1) * A_GROUP_DIM)
        u = _dot(h, w_in_ref[:, cols])
        z = _dot(h, w_in_ref[:, 2 * A_WIDTH + g * A_GROUP_DIM:2 * A_WIDTH + (g + 1) * A_GROUP_DIM])
        ws = jnp.where(causal, ws_ref[g], 0.0).astype(BF16)
        bias = bs_ref[:, g:g + 1]
        mixed = jnp.concatenate(
            [_dot(ws, vn_ref[c * CHUNK:(c + 1) * CHUNK, cols]) + bias for c in range(n_chunks)],
            axis=0)
        y_ref[:, cols] = (u * mixed * _silu(z)).astype(BF16)
    o_ref[...] = x + _dot(y_ref[...], w_out_ref[...])


def _layer_a(x2, ng, w_in16, v_gain, w_s, b_s, w_out16, cast_jobs=()):
    tokens = x2.shape[0]
    tm = LAYER_TILE
    n_steps = tokens // tm
    cast_in, cast_out, cast_shapes = _cast_specs(cast_jobs, n_steps, lambda i: i)
    out, *casted = pl.pallas_call(
        _with_casts(_layer_a_kernel, 7, 1, len(cast_jobs)),
        grid=(n_steps,),
        in_specs=[
            pl.BlockSpec((tm, D_MODEL), lambda i: (i, 0)),
            _resident((1, D_MODEL), lambda i: (0, 0)),
            _resident((D_MODEL, 3 * A_WIDTH), lambda i: (0, 0)),
            _resident((1, A_WIDTH), lambda i: (0, 0)),
            _resident((A_GROUPS, CHUNK, CHUNK), lambda i: (0, 0, 0)),
            _resident((CHUNK, A_GROUPS), lambda i: (0, 0)),
            _resident((A_WIDTH, D_MODEL), lambda i: (0, 0)),
        ] + cast_in,
        out_specs=[pl.BlockSpec((tm, D_MODEL), lambda i: (i, 0))] + cast_out,
        out_shape=[jax.ShapeDtypeStruct((tokens, D_MODEL), F32)] + cast_shapes,
        scratch_shapes=[pltpu.VMEM((tm, A_WIDTH), BF16), pltpu.VMEM((tm, A_WIDTH), BF16)],
        compiler_params=pltpu.CompilerParams(
            dimension_semantics=("arbitrary",), vmem_limit_bytes=VMEM_LIMIT_BYTES),
        name="layer_a",
    )(x2, ng.reshape(1, D_MODEL), w_in16, v_gain.reshape(1, A_WIDTH), w_s, b_s.T, w_out16,
      *[job[0] for job in cast_jobs])
    return out, casted


def _layer_c_kernel(x_ref, ng_ref, w_in_ref, w_grp_ref, scale_ref, w_out_ref,
                    o_ref, ext_ref, y_ref):
    t = pl.program_id(1)
    tm = x_ref.shape[0]
    x = x_ref[...]
    h = _rms_norm(x, ng_ref[...]).astype(BF16)

    @pl.when(t == 0)
    def _():
        ext_ref[0:HALO, :] = jnp.zeros((HALO, C_WIDTH), F32)

    @pl.when(t > 0)
    def _():
        ext_ref[0:HALO, :] = ext_ref[tm:tm + HALO, :]

    ext_ref[HALO:HALO + tm, :] = _dot(h, w_in_ref[:, 0:C_WIDTH])
    pos = t * tm + lax.broadcasted_iota(jnp.int32, (tm, 1), 0)
    for g, window in enumerate(POOL_SIZES):
        cols = slice(g * C_GROUP, (g + 1) * C_GROUP)
        xc = ext_ref[HALO:HALO + tm, cols]
        acc = xc
        for j in range(1, window):
            acc = acc + ext_ref[HALO - j:HALO - j + tm, cols]
        cnt = jnp.minimum(pos + 1, window).astype(F32)
        diff = (acc / cnt - xc).astype(BF16)
        mixed = _dot(diff, w_grp_ref[g]) * scale_ref[:, cols]
        z = _dot(h, w_in_ref[:, C_WIDTH + g * C_GROUP:C_WIDTH + (g + 1) * C_GROUP])
        y_ref[:, cols] = (mixed * _silu(z)).astype(BF16)
    o_ref[...] = x + _dot(y_ref[...], w_out_ref[...])


def _layer_c(x3, ng, w_in16, w_grp16, scale, w_out16):
    bsz, seq, _ = x3.shape
    tm = LAYER_TILE
    return pl.pallas_call(
        _layer_c_kernel,
        grid=(bsz, seq // tm),
        in_specs=[
            pl.BlockSpec((None, tm, D_MODEL), lambda b, t: (b, t, 0)),
            _resident((1, D_MODEL), lambda b, t: (0, 0)),
            _resident((D_MODEL, 2 * C_WIDTH), lambda b, t: (0, 0)),
            _resident((N_POOL, C_GROUP, C_GROUP), lambda b, t: (0, 0, 0)),
            _resident((1, C_WIDTH), lambda b, t: (0, 0)),
            _resident((C_WIDTH, D_MODEL), lambda b, t: (0, 0)),
        ],
        out_specs=pl.BlockSpec((None, tm, D_MODEL), lambda b, t: (b, t, 0)),
        out_shape=jax.ShapeDtypeStruct((bsz, seq, D_MODEL), F32),
        scratch_shapes=[pltpu.VMEM((tm + HALO, C_WIDTH), F32), pltpu.VMEM((tm, C_WIDTH), BF16)],
        compiler_params=pltpu.CompilerParams(
            dimension_semantics=("arbitrary", "arbitrary"), vmem_limit_bytes=VMEM_LIMIT_BYTES),
        name="layer_c",
    )(x3, ng.reshape(1, D_MODEL), w_in16, w_grp16, scale.reshape(1, C_WIDTH), w_out16)


def _attn_proj_kernel(x_ref, ng_ref, freq_ref, wq_ref, wkt_ref, wv_ref, qg_ref, kg_ref,
                      q_ref, kt_ref, v_ref, h_scr, tmp_scr, q_a, kt_a, q_b, kt_b,
                      *, dilation, n_steps, n_super):
    t = pl.program_id(0)
    rows = q_ref.shape[0]
    n_chunk = SUPER // rows
    per = SUPER // dilation
    n_cols = D_MODEL // LANES
    s = t - n_chunk
    cur = jnp.clip(s, 0, n_steps - 1)
    c = cur % n_chunk
    slot = (cur // n_chunk) % 2
    prev = jnp.clip(s - 1, 0, n_steps - 1)
    c_prev = prev % n_chunk
    n_prev = (prev // n_chunk) % n_super

    @pl.when(s == 0)
    def _():
        q_b[...] = jnp.zeros(q_b.shape, q_b.dtype)
        kt_b[...] = jnp.zeros(kt_b.shape, kt_b.dtype)

    coarse = dilation > MERGE_DIL
    quarter_rows = SUPER // MERGE_DIL

    def normalise_tile():
        tile = jnp.minimum(t, n_steps - 1)
        half = (tile // n_chunk) % 2
        hh = _rms_norm(x_ref[...], ng_ref[...])
        if not coarse:
            rs = pl.ds(pl.multiple_of((tile % n_chunk) * rows, rows), rows)
            for j in range(n_cols):
                h_scr[half, j, rs, :] = hh[:, j * LANES:(j + 1) * LANES]
            return
        sub = rows // MERGE_DIL
        base = pl.multiple_of((tile % n_chunk) * sub, sub)
        for j in range(n_cols):
            tmp_scr[j] = hh[:, j * LANES:(j + 1) * LANES]
        for j in range(n_cols):
            for b in range(MERGE_DIL):
                h_scr[half, j, pl.ds(b * quarter_rows + base, sub), :] = (
                    tmp_scr[j, pl.ds(b, sub, stride=MERGE_DIL), :])

    def gather(j):
        if dilation == 1:
            return h_scr[slot, j, pl.ds(pl.multiple_of(c * rows, rows), rows), :]
        run = min(per, rows)
        if per > rows:
            raise NotImplementedError("a residue class longer than the chunk needs a row offset")
        if not coarse:
            first = c * (rows // run)
            pieces = [h_scr[slot, j, pl.ds(first + rr, run, stride=dilation), :] for rr in range(rows // run)]
            return pieces[0] if len(pieces) == 1 else jnp.concatenate(pieces, axis=0)
        fine = dilation // MERGE_DIL
        if rows // run != MERGE_DIL:
            raise NotImplementedError("chunk / residue-run geometry not covered")
        pieces = [h_scr[slot, j, pl.ds(rr * quarter_rows + c, run, stride=fine), :]
                  for rr in range(MERGE_DIL)]
        return jnp.concatenate(pieces, axis=0)

    def project(q_raw, kt_raw):
        h = jnp.concatenate([gather(j) for j in range(n_cols)], axis=1).astype(BF16)
        q_raw[...] = _dot(h, wq_ref[...])
        kt_raw[...] = _dot_nt(wkt_ref[...], h)
        v_ref[...] = _dot(h, wv_ref[...]).astype(BF16)

    def finish(q_raw, kt_raw):
        p = c_prev * rows + lax.broadcasted_iota(jnp.int32, (1, rows), 1)
        residue = lax.shift_right_logical(p, per.bit_length() - 1)
        strided = jnp.bitwise_and(p, per - 1)
        pos = n_prev * SUPER + strided * dilation + residue
        ang = pos.astype(F32) * freq_ref[...]
        cos_t, sin_t = jnp.cos(ang), jnp.sin(ang)

        for hd in range(B_HEADS):
            slab = kt_raw[hd * HEAD_DIM:(hd + 1) * HEAD_DIM, :]
            kn = slab * lax.rsqrt(jnp.mean(slab * slab, axis=0, keepdims=True) + EPS) * kg_ref[...]
            x1, x2 = kn[0:ROPE_HALF, :], kn[ROPE_HALF:ROPE_DIM, :]
            rotated = jnp.concatenate(
                [x1 * cos_t - x2 * sin_t, x2 * cos_t + x1 * sin_t, kn[ROPE_DIM:, :]], axis=0)
            kt_ref[hd * HEAD_DIM:(hd + 1) * HEAD_DIM, :] = rotated.astype(BF16)

        rest = HEAD_DIM - ROPE_DIM
        cos_tab = jnp.concatenate([cos_t, cos_t, jnp.ones((rest, rows), F32)], axis=0).T * Q_PRESCALE
        sin_tab = jnp.concatenate([-sin_t, sin_t, jnp.zeros((rest, rows), F32)], axis=0).T * Q_PRESCALE
        lane = lax.broadcasted_iota(jnp.int32, (rows, HEAD_DIM), 1)
        for hd in range(B_HEADS):
            cols = slice(hd * HEAD_DIM, (hd + 1) * HEAD_DIM)
            qn = _rms_norm(q_raw[:, cols], qg_ref[...])
            partner = jnp.where(lane < ROPE_HALF, pltpu.roll(qn, HEAD_DIM - ROPE_HALF, 1),
                                pltpu.roll(qn, ROPE_HALF, 1))
            q_ref[:, cols] = (qn * cos_tab + partner * sin_tab).astype(BF16)

    @pl.when(s < 0)
    def _():
        normalise_tile()

    @pl.when((s >= 0) & (s % 2 == 0))
    def _():
        project(q_a, kt_a)
        finish(q_b, kt_b)
        normalise_tile()

    @pl.when((s >= 0) & (s % 2 == 1))
    def _():
        project(q_b, kt_b)
        finish(q_a, kt_a)
        normalise_tile()


def _attn_proj(x3, ng, inv_freq, w_in16, w_kt16, q_gain, k_gain, group, dilation, cast_jobs=()):
    bsz, seq, _ = x3.shape
    rows = TOKEN_TILE
    n_super, n_chunk = seq // SUPER, SUPER // rows
    per_batch = n_super * n_chunk
    n_steps = bsz * per_batch
    tile = lambda t: jnp.minimum(t, n_steps - 1)
    cur = lambda t: jnp.clip(t - n_chunk, 0, n_steps - 1)
    prev = lambda t: jnp.clip(t - n_chunk - 1, 0, n_steps - 1)
    raw = [pltpu.VMEM((rows, B_WIDTH), F32), pltpu.VMEM((B_WIDTH, rows), F32)]
    cast_in, cast_out, cast_shapes = _cast_specs(cast_jobs, n_steps, lambda s: s)
    body = functools.partial(_attn_proj_kernel, dilation=dilation, n_steps=n_steps, n_super=n_super)
    q, kt, v, *casted = pl.pallas_call(
        _with_casts(body, 8, 3, len(cast_jobs)),
        grid=(n_chunk + n_steps + 1,),
        in_specs=[
            pl.BlockSpec((None, rows, D_MODEL), lambda t: (tile(t) // per_batch, tile(t) % per_batch, 0)),
            _resident((1, D_MODEL), lambda s: (0, 0)),
            _resident((ROPE_HALF, 1), lambda s: (0, 0)),
            _resident((D_MODEL, B_WIDTH), lambda s: (0, group)),
            _resident((B_WIDTH, D_MODEL), lambda s: (group, 0)),
            _resident((D_MODEL, B_WIDTH), lambda s: (0, 2 * N_B_GROUPS + group)),
            _resident((1, HEAD_DIM), lambda s: (0, 0)),
            _resident((HEAD_DIM, 1), lambda s: (0, 0)),
        ] + cast_in,
        out_specs=[
            pl.BlockSpec((None, rows, B_WIDTH), lambda s: (prev(s) // per_batch, prev(s) % per_batch, 0)),
            pl.BlockSpec((None, B_WIDTH, rows), lambda s: (prev(s) // per_batch, 0, prev(s) % per_batch)),
            pl.BlockSpec((None, rows, B_WIDTH), lambda s: (cur(s) // per_batch, cur(s) % per_batch, 0)),
        ] + cast_out,
        out_shape=[
            jax.ShapeDtypeStruct((bsz, seq, B_WIDTH), BF16),
            jax.ShapeDtypeStruct((bsz, B_WIDTH, seq), BF16),
            jax.ShapeDtypeStruct((bsz, seq, B_WIDTH), BF16),
        ] + cast_shapes,
        scratch_shapes=[pltpu.VMEM((2, D_MODEL // LANES, SUPER, LANES), F32),
                        pltpu.VMEM((D_MODEL // LANES, rows, LANES), F32)] + raw + raw,
        compiler_params=pltpu.CompilerParams(
            dimension_semantics=("arbitrary",), vmem_limit_bytes=VMEM_LIMIT_BYTES),
        name=f"attn_proj_d{dilation}",
    )(x3, ng.reshape(1, D_MODEL), inv_freq, w_in16, w_kt16, w_in16,
      q_gain.reshape(1, HEAD_DIM), k_gain.reshape(HEAD_DIM, 1), *[job[0] for job in cast_jobs])
    return (q, kt, v), casted


def _attn_kernel(*refs):
    qkv_refs = refs[:3 * N_B_GROUPS]
    y_ref = refs[3 * N_B_GROUPS]
    scratch = refs[3 * N_B_GROUPS + 1:]
    kt_scrs, v_scrs = scratch[0:N_B_GROUPS], scratch[N_B_GROUPS:2 * N_B_GROUPS]
    o_scr, lse_scr, bias_scr = scratch[2 * N_B_GROUPS:]
    n = pl.program_id(2)

    qi = lax.broadcasted_iota(jnp.int32, (SPAN, 2 * SPAN), 0)
    ki = lax.broadcasted_iota(jnp.int32, (SPAN, 2 * SPAN), 1)
    band = (ki >= qi) & (ki <= qi + SPAN)
    bias_scr[0] = jnp.where(band, 0.0, MASK_VALUE)
    bias_scr[1] = jnp.where(band & (ki >= jnp.where(n > 0, 0, SPAN)), 0.0, MASK_VALUE)

    for g, (_, d) in enumerate(B_PATTERNS):
        q_ref, kt_ref, v_ref = qkv_refs[3 * g:3 * g + 3]
        kt_scr, v_scr = kt_scrs[g], v_scrs[g]
        per = SUPER // d
        seg = SPAN + per

        @pl.when(n == 0)
        def _():
            v_scr[...] = jnp.ones(v_scr.shape, BF16)
            for r in range(d):
                kt_scr[:, r * seg:r * seg + SPAN] = jnp.zeros((kt_scr.shape[0], SPAN), BF16)
                for hh in range(HEADS_PER_STEP):
                    v_scr[r * seg:r * seg + SPAN, 2 * hh * HEAD_DIM:(2 * hh + 1) * HEAD_DIM] = (
                        jnp.zeros((SPAN, HEAD_DIM), BF16))

        @pl.when(n > 0)
        def _():
            for r in range(d):
                kt_scr[:, r * seg:r * seg + SPAN] = kt_scr[:, r * seg + per:r * seg + per + SPAN]
                v_scr[r * seg:r * seg + SPAN, :] = v_scr[r * seg + per:r * seg + per + SPAN, :]

        for r in range(d):
            kt_scr[:, r * seg + SPAN:(r + 1) * seg] = kt_ref[:, r * per:(r + 1) * per]
            for hh in range(HEADS_PER_STEP):
                v_scr[r * seg + SPAN:(r + 1) * seg, 2 * hh * HEAD_DIM:(2 * hh + 1) * HEAD_DIM] = (
                    v_ref[r * per:(r + 1) * per, hh * HEAD_DIM:(hh + 1) * HEAD_DIM])

        for r in range(d):
            for j in range(per // SPAN):
                q_rows = slice(r * per + j * SPAN, r * per + (j + 1) * SPAN)
                keys = slice(r * seg + j * SPAN, r * seg + (j + 2) * SPAN)
                if d < MERGE_DIL:
                    assert d == 1
                    nat = slice(j * SPAN, (j + 1) * SPAN)
                else:
                    fine = d // MERGE_DIL
                    start = (r % MERGE_DIL) * (SUPER // MERGE_DIL) + fine * j * SPAN + r // MERGE_DIL
                    nat = pl.ds(start, SPAN, stride=fine) if fine > 1 else slice(start, start + SPAN)
                for hh in range(HEADS_PER_STEP):
                    cols = slice(hh * HEAD_DIM, (hh + 1) * HEAD_DIM)
                    s = _dot(q_ref[q_rows, cols], kt_scr[cols, keys]) + bias_scr[1 if j == 0 else 0]
                    m = jnp.max(s, axis=-1, keepdims=True)
                    p = jnp.exp2(s - m)
                    pv = _dot(p.astype(BF16), v_scr[keys, 2 * hh * HEAD_DIM:(2 * hh + 2) * HEAD_DIM])
                    denom = pv[:, HEAD_DIM:]
                    o_scr[g, hh, nat, :] = pv[:, :HEAD_DIM] * (1.0 / denom)
                    lse_scr[g, hh, nat, :] = m + jnp.log2(denom)

    rows = MERGE_ROWS
    for hh in range(HEADS_PER_STEP):
        for b in range(MERGE_DIL):
            for u0 in range(0, SUPER // MERGE_DIL, rows):
                natural = pl.ds(b + MERGE_DIL * u0, rows, stride=MERGE_DIL)
                ordered = slice(b * (SUPER // MERGE_DIL) + u0, b * (SUPER // MERGE_DIL) + u0 + rows)
                idx = [natural if d < MERGE_DIL else ordered for _, d in B_PATTERNS]
                lses = [lse_scr[g, hh, idx[g], :] for g in range(N_B_GROUPS)]
                m = jnp.maximum(jnp.maximum(lses[0], lses[1]), lses[2])
                es = [jnp.exp2(l - m) for l in lses]
                mixed = (es[0] * o_scr[0, hh, idx[0], :] + es[1] * o_scr[1, hh, idx[1], :]
                         + es[2] * o_scr[2, hh, idx[2], :])
                y_ref[hh, natural, :] = mixed * (1.0 / (es[0] + es[1] + es[2]))


def _attn(qkvs, bsz, seq, cast_jobs=()):
    width = HEADS_PER_STEP * HEAD_DIM
    n_super = seq // SUPER
    in_specs, operands, kt_scr, v_scr = [], [], [], []
    for (q, kt, v), (_, d) in zip(qkvs, B_PATTERNS):
        in_specs += [
            pl.BlockSpec((None, SUPER, width), lambda b, hp, n: (b, n, hp)),
            pl.BlockSpec((None, width, SUPER), lambda b, hp, n: (b, hp, n)),
            pl.BlockSpec((None, SUPER, width), lambda b, hp, n: (b, n, hp)),
        ]
        operands += [q, kt, v]
        kt_scr.append(pltpu.VMEM((width, SUPER + d * SPAN), BF16))
        v_scr.append(pltpu.VMEM((SUPER + d * SPAN, 2 * width), BF16))
    n_pairs = B_HEADS // HEADS_PER_STEP
    cast_in, cast_out, cast_shapes = _cast_specs(
        cast_jobs, bsz * n_pairs * n_super, lambda b, hp, n: (b * n_pairs + hp) * n_super + n)
    out, *casted = pl.pallas_call(
        _with_casts(_attn_kernel, len(operands), 1, len(cast_jobs)),
        grid=(bsz, n_pairs, n_super),
        in_specs=in_specs + cast_in,
        out_specs=[pl.BlockSpec((None, HEADS_PER_STEP, SUPER, HEAD_DIM),
                                lambda b, hp, n: (b, hp, n, 0))] + cast_out,
        out_shape=[jax.ShapeDtypeStruct((bsz, B_HEADS, seq, HEAD_DIM), F32)] + cast_shapes,
        scratch_shapes=kt_scr + v_scr + [
            pltpu.VMEM((N_B_GROUPS, HEADS_PER_STEP, SUPER, HEAD_DIM), F32),
            pltpu.VMEM((N_B_GROUPS, HEADS_PER_STEP, SUPER, HEAD_DIM), F32),
            pltpu.VMEM((2, SPAN, 2 * SPAN), F32),
        ],
        compiler_params=pltpu.CompilerParams(
            dimension_semantics=("arbitrary", "arbitrary", "arbitrary"),
            vmem_limit_bytes=VMEM_LIMIT_BYTES),
        name="attn",
    )(*operands, *[job[0] for job in cast_jobs])
    return out, casted


def _attn_out_kernel(x_ref, ng_ref, wz_ref, o_ref, w_out_ref, out_ref):
    x = x_ref[...]
    h = _rms_norm(x, ng_ref[...]).astype(BF16)
    z = _dot(h, wz_ref[...])
    o = jnp.concatenate([o_ref[hd] for hd in range(B_HEADS)], axis=1)
    y = (o * _silu(z)).astype(BF16)
    out_ref[...] = x + _dot(y, w_out_ref[...])


def _attn_out(x3, ng, w_in16, o, w_out16):
    bsz, seq, _ = x3.shape
    tm = LAYER_TILE
    tile = pl.BlockSpec((None, tm, D_MODEL), lambda b, t: (b, t, 0))
    return pl.pallas_call(
        _attn_out_kernel,
        grid=(bsz, seq // tm),
        in_specs=[
            tile,
            _resident((1, D_MODEL), lambda b, t: (0, 0)),
            _resident((D_MODEL, B_WIDTH), lambda b, t: (0, 3 * N_B_GROUPS)),
            pl.BlockSpec((None, B_HEADS, tm, HEAD_DIM), lambda b, t: (b, 0, t, 0)),
            _resident((B_WIDTH, D_MODEL), lambda b, t: (0, 0)),
        ],
        out_specs=tile,
        out_shape=jax.ShapeDtypeStruct((bsz, seq, D_MODEL), F32),
        compiler_params=pltpu.CompilerParams(
            dimension_semantics=("arbitrary", "arbitrary"), vmem_limit_bytes=VMEM_LIMIT_BYTES),
        name="attn_out",
    )(x3, ng.reshape(1, D_MODEL), w_in16, o, w_out16)


def _layer_b(x3, ng, w_in16, w_kt16, q_gain, k_gain, w_out16, proj_cast_jobs, attn_cast_jobs):
    bsz, seq, _ = x3.shape
    inv_freq = jnp.power(jnp.float32(ROPE_THETA), -jnp.arange(ROPE_HALF, dtype=F32) / ROPE_HALF)
    inv_freq = inv_freq.reshape(ROPE_HALF, 1)
    qkvs, casted = [], []
    for g, (window, dilation) in enumerate(B_PATTERNS):
        assert window // dilation == SPAN
        qkv, done = _attn_proj(x3, ng, inv_freq, w_in16, w_kt16, q_gain[g], k_gain[g], g, dilation,
                               proj_cast_jobs[g])
        qkvs.append(qkv)
        casted += done
    o, done = _attn(qkvs, bsz, seq, attn_cast_jobs)
    return _attn_out(x3, ng, w_in16, o, w_out16), casted + done


def kernel(x, norm_gain, a_w_in, a_v_gain, a_w_s, a_b_s, a_w_out, b_w_in, b_q_gain, b_k_gain,
           b_w_out, c_w_in, c_w_grp, c_scale, c_w_out):
    bsz, seq, d_model = x.shape
    assert d_model == D_MODEL and seq % SUPER == 0 and SUPER % TOKEN_TILE == 0
    assert norm_gain.shape[0] == 4 and a_w_in.shape[0] == 2
    c_w_grp2 = c_w_grp.reshape(c_w_grp.shape[0], N_POOL * C_GROUP, C_GROUP)
    k_cols = (N_B_GROUPS * B_WIDTH, N_B_GROUPS * B_WIDTH)

    x2, (b_w_in16, b_w_out16, b_w_kt16) = _layer_a(
        x.reshape(bsz * seq, D_MODEL), norm_gain[0], a_w_in[0].astype(BF16), a_v_gain[0], a_w_s[0],
        a_b_s[0], a_w_out[0].astype(BF16),
        cast_jobs=[(b_w_in, 0, None), (b_w_out, 0, None), (b_w_in, 0, k_cols)])
    x, (c_w_in16, c_w_grp16, c_w_out16, a_w_in16, a_w_out16) = _layer_b(
        x2.reshape(bsz, seq, D_MODEL), norm_gain[1], b_w_in16, b_w_kt16, b_q_gain[0], b_k_gain[0],
        b_w_out16,
        proj_cast_jobs=[[(c_w_in, 0, None)], [(c_w_grp2, 0, None)], [(c_w_out, 0, None)]],
        attn_cast_jobs=[(a_w_in, 1, None), (a_w_out, 1, None)])
    x = _layer_c(x, norm_gain[2], c_w_in16, c_w_grp16.reshape(N_POOL, C_GROUP, C_GROUP), c_scale[0],
                 c_w_out16)
    x2, _ = _layer_a(x.reshape(bsz * seq, D_MODEL), norm_gain[3], a_w_in16, a_v_gain[1], a_w_s[1],
                     a_b_s[1], a_w_out16)
    return x2.reshape(bsz, seq, D_MODEL)
```

```python
import functools
import math

import jax
import jax.numpy as jnp
import numpy as np
from jax import lax
from jax.experimental import pallas as pl
from jax.experimental.pallas import tpu as pltpu

D_MODEL = 1024
EPS = 1e-6
CHUNK = 128
A_WIDTH = 2 * D_MODEL
A_GROUPS = 8
A_GROUP_DIM = A_WIDTH // A_GROUPS
HEAD_DIM = 128
B_HEADS = D_MODEL // HEAD_DIM
B_PATTERNS = ((128, 1), (512, 4), (2048, 16))
N_B_GROUPS = len(B_PATTERNS)
B_WIDTH = B_HEADS * HEAD_DIM
ROPE_DIM = HEAD_DIM // 4
ROPE_HALF = ROPE_DIM // 2
ROPE_THETA = 500000.0
Q_PRESCALE = math.log2(math.e) / math.sqrt(HEAD_DIM)
SPAN = 128
SUPER = SPAN * max(d for _, d in B_PATTERNS)
HEADS_PER_STEP = 2
MERGE_DIL = 4
POOL_SIZES = (2, 4, 8, 16)
N_POOL = len(POOL_SIZES)
C_WIDTH = 2 * D_MODEL
C_GROUP = C_WIDTH // N_POOL
HALO = 32

LANES = 128
BF16_ROWS = 16
MERGE_ROWS = 256
VMEM_LIMIT_BYTES = 60 * 1024 * 1024

TOKEN_TILE = 512
LAYER_TILE = 1024
MASK_VALUE = -1e30

F32 = jnp.float32
BF16 = jnp.bfloat16


def _resident(shape, index_map):
    return pl.BlockSpec(shape, index_map, pipeline_mode=pl.Buffered(1))


def _rms_norm(x, gain):
    return x * lax.rsqrt(jnp.mean(x * x, axis=-1, keepdims=True) + EPS) * gain


def _dot(a, b):
    return jnp.dot(a, b, preferred_element_type=F32)


def _dot_nt(a, b):
    return lax.dot_general(a, b, (((1,), (1,)), ((), ())), preferred_element_type=F32)


def _silu(z):
    return z * jax.nn.sigmoid(z)


def _cast_specs(jobs, n_steps, flat_step):
    in_specs, out_specs, out_shapes = [], [], []
    for arr, layer, col_range in jobs:
        _, rows, cols = arr.shape
        if col_range is None:
            n_chunks = n_steps
            chunk = rows // n_chunks
            assert chunk * n_chunks == rows and chunk % BF16_ROWS == 0, (arr.shape, n_steps)
            last = n_chunks - 1
            in_specs.append(pl.BlockSpec(
                (None, chunk, cols),
                lambda *ids, layer=layer, last=last: (layer, jnp.minimum(flat_step(*ids), last), 0)))
            out_specs.append(pl.BlockSpec(
                (chunk, cols), lambda *ids, last=last: (jnp.minimum(flat_step(*ids), last), 0)))
            out_shapes.append(jax.ShapeDtypeStruct((rows, cols), BF16))
        else:
            start, width = col_range
            n_chunks = rows // LANES
            assert n_chunks <= n_steps and start % width == 0, (arr.shape, n_steps, col_range)
            last = n_chunks - 1
            in_specs.append(pl.BlockSpec(
                (None, LANES, width),
                lambda *ids, layer=layer, last=last, blk=start // width: (
                    layer, jnp.minimum(flat_step(*ids), last), blk)))
            out_specs.append(pl.BlockSpec(
                (width, LANES), lambda *ids, last=last: (0, jnp.minimum(flat_step(*ids), last))))
            out_shapes.append(jax.ShapeDtypeStruct((width, rows), BF16))
    return in_specs, out_specs, out_shapes


def _with_casts(body, n_in, n_out, n_jobs):
    if n_jobs == 0:
        return body

    def wrapped(*refs):
        ins, cast_in = refs[:n_in], refs[n_in:n_in + n_jobs]
        outs = refs[n_in + n_jobs:n_in + n_jobs + n_out]
        cast_out = refs[n_in + n_jobs + n_out:n_in + 2 * n_jobs + n_out]
        body(*ins, *outs, *refs[n_in + 2 * n_jobs + n_out:])
        for src, dst in zip(cast_in, cast_out):
            transposed = src.shape != dst.shape
            dst[...] = (src[...].T if transposed else src[...]).astype(BF16)

    return wrapped


def _layer_a_kernel(x_ref, ng_ref, w_in_ref, vg_ref, ws_ref, bs_ref, w_out_ref,
                    o_ref, vn_ref, y_ref):
    x = x_ref[...]
    h = _rms_norm(x, ng_ref[...]).astype(BF16)
    v = _dot(h, w_in_ref[:, A_WIDTH:2 * A_WIDTH])
    vn_ref[...] = _rms_norm(v, vg_ref[...]).astype(BF16)
    row = lax.broadcasted_iota(jnp.int32, (CHUNK, CHUNK), 0)
    col = lax.broadcasted_iota(jnp.int32, (CHUNK, CHUNK), 1)
    causal = col <= row
    n_chunks = x.shape[0] // CHUNK
    for g in range(A_GROUPS):
        cols = slice(g * A_GROUP_DIM, (g + 1) * A_GROUP_DIM)
        u = _dot(h, w_in_ref[:, cols])
        z = _dot(h, w_in_ref[:, 2 * A_WIDTH + g * A_GROUP_DIM:2 * A_WIDTH + (g + 1) * A_GROUP_DIM])
        ws = jnp.where(causal, ws_ref[g], 0.0).astype(BF16)
        bias = bs_ref[:, g:g + 1]
        mixed = jnp.concatenate(
            [_dot(ws, vn_ref[c * CHUNK:(c + 1) * CHUNK, cols]) + bias for c in range(n_chunks)],
            axis=0)
        y_ref[:, cols] = (u * mixed * _silu(z)).astype(BF16)
    o_ref[...] = x + _dot(y_ref[...], w_out_ref[...])


def _layer_a(x2, ng, w_in16, v_gain, w_s, b_s, w_out16, cast_jobs=()):
    tokens = x2.shape[0]
    tm = LAYER_TILE
    n_steps = tokens // tm
    cast_in, cast_out, cast_shapes = _cast_specs(cast_jobs, n_steps, lambda i: i)
    out, *casted = pl.pallas_call(
        _with_casts(_layer_a_kernel, 7, 1, len(cast_jobs)),
        grid=(n_steps,),
        in_specs=[
            pl.BlockSpec((tm, D_MODEL), lambda i: (i, 0)),
            _resident((1, D_MODEL), lambda i: (0, 0)),
            _resident((D_MODEL, 3 * A_WIDTH), lambda i: (0, 0)),
            _resident((1, A_WIDTH), lambda i: (0, 0)),
            _resident((A_GROUPS, CHUNK, CHUNK), lambda i: (0, 0, 0)),
            _resident((CHUNK, A_GROUPS), lambda i: (0, 0)),
            _resident((A_WIDTH, D_MODEL), lambda i: (0, 0)),
        ] + cast_in,
        out_specs=[pl.BlockSpec((tm, D_MODEL), lambda i: (i, 0))] + cast_out,
        out_shape=[jax.ShapeDtypeStruct((tokens, D_MODEL), F32)] + cast_shapes,
        scratch_shapes=[pltpu.VMEM((tm, A_WIDTH), BF16), pltpu.VMEM((tm, A_WIDTH), BF16)],
        compiler_params=pltpu.CompilerParams(
            dimension_semantics=("arbitrary",), vmem_limit_bytes=VMEM_LIMIT_BYTES),
        name="layer_a",
    )(x2, ng.reshape(1, D_MODEL), w_in16, v_gain.reshape(1, A_WIDTH), w_s, b_s.T, w_out16,
      *[job[0] for job in cast_jobs])
    return out, casted


def _layer_c_kernel(x_ref, ng_ref, w_in_ref, w_grp_ref, scale_ref, w_out_ref,
                    o_ref, ext_ref, sum_ref, y_ref):
    t = pl.program_id(1)
    tm = x_ref.shape[0]
    x = x_ref[...]
    h = _rms_norm(x, ng_ref[...]).astype(BF16)

    @pl.when(t == 0)
    def _():
        ext_ref[0:HALO, :] = jnp.zeros((HALO, C_WIDTH), F32)

    @pl.when(t > 0)
    def _():
        ext_ref[0:HALO, :] = ext_ref[tm:tm + HALO, :]

    ext_ref[HALO:HALO + tm, :] = _dot(h, w_in_ref[:, 0:C_WIDTH])
    pos = t * tm + lax.broadcasted_iota(jnp.int32, (tm, 1), 0)
    for g, window in enumerate(POOL_SIZES):
        cols = slice(g * C_GROUP, (g + 1) * C_GROUP)
        xc = ext_ref[HALO:HALO + tm, cols]
        span, lo = 1, 8
        while span < window:
            src = ext_ref if span == 1 else sum_ref
            src_cols = cols if span == 1 else slice(None)
            sum_ref[lo:HALO + tm, :] = (src[lo:HALO + tm, src_cols]
                                        + src[lo - span:HALO + tm - span, src_cols])
            span, lo = 2 * span, lo + 8
        assert lo <= HALO + 8
        cnt = jnp.minimum(pos + 1, window).astype(F32)
        diff = (sum_ref[HALO:HALO + tm, :] / cnt - xc).astype(BF16)
        mixed = _dot(diff, w_grp_ref[g]) * scale_ref[:, cols]
        z = _dot(h, w_in_ref[:, C_WIDTH + g * C_GROUP:C_WIDTH + (g + 1) * C_GROUP])
        y_ref[:, cols] = (mixed * _silu(z)).astype(BF16)
    o_ref[...] = x + _dot(y_ref[...], w_out_ref[...])


def _layer_c(x3, ng, w_in16, w_grp16, scale, w_out16):
    bsz, seq, _ = x3.shape
    tm = LAYER_TILE
    return pl.pallas_call(
        _layer_c_kernel,
        grid=(bsz, seq // tm),
        in_specs=[
            pl.BlockSpec((None, tm, D_MODEL), lambda b, t: (b, t, 0)),
            _resident((1, D_MODEL), lambda b, t: (0, 0)),
            _resident((D_MODEL, 2 * C_WIDTH), lambda b, t: (0, 0)),
            _resident((N_POOL, C_GROUP, C_GROUP), lambda b, t: (0, 0, 0)),
            _resident((1, C_WIDTH), lambda b, t: (0, 0)),
            _resident((C_WIDTH, D_MODEL), lambda b, t: (0, 0)),
        ],
        out_specs=pl.BlockSpec((None, tm, D_MODEL), lambda b, t: (b, t, 0)),
        out_shape=jax.ShapeDtypeStruct((bsz, seq, D_MODEL), F32),
        scratch_shapes=[pltpu.VMEM((tm + HALO, C_WIDTH), F32), pltpu.VMEM((tm + HALO, C_GROUP), F32),
                        pltpu.VMEM((tm, C_WIDTH), BF16)],
        compiler_params=pltpu.CompilerParams(
            dimension_semantics=("arbitrary", "arbitrary"), vmem_limit_bytes=VMEM_LIMIT_BYTES),
        name="layer_c",
    )(x3, ng.reshape(1, D_MODEL), w_in16, w_grp16, scale.reshape(1, C_WIDTH), w_out16)


def _attn_proj_kernel(x_ref, ng_ref, freq_ref, wq_ref, wkt_ref, wv_ref, qg_ref, kg_ref,
                      q_ref, kt_ref, v_ref, h_scr, tmp_scr, q_a, kt_a, q_b, kt_b,
                      *, dilation, n_steps, n_super):
    t = pl.program_id(0)
    rows = q_ref.shape[0]
    n_chunk = SUPER // rows
    per = SUPER // dilation
    n_cols = D_MODEL // LANES
    s = t - n_chunk
    cur = jnp.clip(s, 0, n_steps - 1)
    c = cur % n_chunk
    slot = (cur // n_chunk) % 2
    prev = jnp.clip(s - 1, 0, n_steps - 1)
    c_prev = prev % n_chunk
    n_prev = (prev // n_chunk) % n_super

    @pl.when(s == 0)
    def _():
        q_b[...] = jnp.zeros(q_b.shape, q_b.dtype)
        kt_b[...] = jnp.zeros(kt_b.shape, kt_b.dtype)

    coarse = dilation > MERGE_DIL
    quarter_rows = SUPER // MERGE_DIL

    def normalise_tile():
        tile = jnp.minimum(t, n_steps - 1)
        half = (tile // n_chunk) % 2
        hh = _rms_norm(x_ref[...], ng_ref[...])
        if not coarse:
            rs = pl.ds(pl.multiple_of((tile % n_chunk) * rows, rows), rows)
            for j in range(n_cols):
                h_scr[half, j, rs, :] = hh[:, j * LANES:(j + 1) * LANES]
            return
        sub = rows // MERGE_DIL
        base = pl.multiple_of((tile % n_chunk) * sub, sub)
        for j in range(n_cols):
            tmp_scr[j] = hh[:, j * LANES:(j + 1) * LANES]
        for j in range(n_cols):
            for b in range(MERGE_DIL):
                h_scr[half, j, pl.ds(b * quarter_rows + base, sub), :] = (
                    tmp_scr[j, pl.ds(b, sub, stride=MERGE_DIL), :])

    def gather(j):
        if dilation == 1:
            return h_scr[slot, j, pl.ds(pl.multiple_of(c * rows, rows), rows), :]
        run = min(per, rows)
        if per > rows:
            raise NotImplementedError("a residue class longer than the chunk needs a row offset")
        if not coarse:
            first = c * (rows // run)
            pieces = [h_scr[slot, j, pl.ds(first + rr, run, stride=dilation), :] for rr in range(rows // run)]
            return pieces[0] if len(pieces) == 1 else jnp.concatenate(pieces, axis=0)
        fine = dilation // MERGE_DIL
        if rows // run != MERGE_DIL:
            raise NotImplementedError("chunk / residue-run geometry not covered")
        pieces = [h_scr[slot, j, pl.ds(rr * quarter_rows + c, run, stride=fine), :]
                  for rr in range(MERGE_DIL)]
        return jnp.concatenate(pieces, axis=0)

    def project(q_raw, kt_raw):
        h = jnp.concatenate([gather(j) for j in range(n_cols)], axis=1).astype(BF16)
        q_raw[...] = _dot(h, wq_ref[...])
        kt_raw[...] = _dot_nt(wkt_ref[...], h)
        v_ref[...] = _dot(h, wv_ref[...]).astype(BF16)

    def finish(q_raw, kt_raw):
        p = c_prev * rows + lax.broadcasted_iota(jnp.int32, (1, rows), 1)
        residue = lax.shift_right_logical(p, per.bit_length() - 1)
        strided = jnp.bitwise_and(p, per - 1)
        pos = n_prev * SUPER + strided * dilation + residue
        ang = pos.astype(F32) * freq_ref[...]
        cos_t, sin_t = jnp.cos(ang), jnp.sin(ang)

        for hd in range(B_HEADS):
            slab = kt_raw[hd * HEAD_DIM:(hd + 1) * HEAD_DIM, :]
            kn = slab * lax.rsqrt(jnp.mean(slab * slab, axis=0, keepdims=True) + EPS) * kg_ref[...]
            x1, x2 = kn[0:ROPE_HALF, :], kn[ROPE_HALF:ROPE_DIM, :]
            rotated = jnp.concatenate(
                [x1 * cos_t - x2 * sin_t, x2 * cos_t + x1 * sin_t, kn[ROPE_DIM:, :]], axis=0)
            kt_ref[hd * HEAD_DIM:(hd + 1) * HEAD_DIM, :] = rotated.astype(BF16)

        rest = HEAD_DIM - ROPE_DIM
        cos_tab = jnp.concatenate([cos_t, cos_t, jnp.ones((rest, rows), F32)], axis=0).T * Q_PRESCALE
        sin_tab = jnp.concatenate([-sin_t, sin_t, jnp.zeros((rest, rows), F32)], axis=0).T * Q_PRESCALE
        lane = lax.broadcasted_iota(jnp.int32, (rows, HEAD_DIM), 1)
        for hd in range(B_HEADS):
            cols = slice(hd * HEAD_DIM, (hd + 1) * HEAD_DIM)
            qn = _rms_norm(q_raw[:, cols], qg_ref[...])
            partner = jnp.where(lane < ROPE_HALF, pltpu.roll(qn, HEAD_DIM - ROPE_HALF, 1),
                                pltpu.roll(qn, ROPE_HALF, 1))
            q_ref[:, cols] = (qn * cos_tab + partner * sin_tab).astype(BF16)

    @pl.when(s < 0)
    def _():
        normalise_tile()

    @pl.when((s >= 0) & (s % 2 == 0))
    def _():
        project(q_a, kt_a)
        finish(q_b, kt_b)
        normalise_tile()

    @pl.when((s >= 0) & (s % 2 == 1))
    def _():
        project(q_b, kt_b)
        finish(q_a, kt_a)
        normalise_tile()


def _attn_proj(x3, ng, inv_freq, w_in16, w_kt16, q_gain, k_gain, group, dilation, cast_jobs=()):
    bsz, seq, _ = x3.shape
    rows = TOKEN_TILE
    n_super, n_chunk = seq // SUPER, SUPER // rows
    per_batch = n_super * n_chunk
    n_steps = bsz * per_batch
    tile = lambda t: jnp.minimum(t, n_steps - 1)
    cur = lambda t: jnp.clip(t - n_chunk, 0, n_steps - 1)
    prev = lambda t: jnp.clip(t - n_chunk - 1, 0, n_steps - 1)
    raw = [pltpu.VMEM((rows, B_WIDTH), F32), pltpu.VMEM((B_WIDTH, rows), F32)]
    cast_in, cast_out, cast_shapes = _cast_specs(cast_jobs, n_steps, lambda s: s)
    body = functools.partial(_attn_proj_kernel, dilation=dilation, n_steps=n_steps, n_super=n_super)
    q, kt, v, *casted = pl.pallas_call(
        _with_casts(body, 8, 3, len(cast_jobs)),
        grid=(n_chunk + n_steps + 1,),
        in_specs=[
            pl.BlockSpec((None, rows, D_MODEL), lambda t: (tile(t) // per_batch, tile(t) % per_batch, 0)),
            _resident((1, D_MODEL), lambda s: (0, 0)),
            _resident((ROPE_HALF, 1), lambda s: (0, 0)),
            _resident((D_MODEL, B_WIDTH), lambda s: (0, group)),
            _resident((B_WIDTH, D_MODEL), lambda s: (group, 0)),
            _resident((D_MODEL, B_WIDTH), lambda s: (0, 2 * N_B_GROUPS + group)),
            _resident((1, HEAD_DIM), lambda s: (0, 0)),
            _resident((HEAD_DIM, 1), lambda s: (0, 0)),
        ] + cast_in,
        out_specs=[
            pl.BlockSpec((None, rows, B_WIDTH), lambda s: (prev(s) // per_batch, prev(s) % per_batch, 0)),
            pl.BlockSpec((None, B_WIDTH, rows), lambda s: (prev(s) // per_batch, 0, prev(s) % per_batch)),
            pl.BlockSpec((None, rows, B_WIDTH), lambda s: (cur(s) // per_batch, cur(s) % per_batch, 0)),
        ] + cast_out,
        out_shape=[
            jax.ShapeDtypeStruct((bsz, seq, B_WIDTH), BF16),
            jax.ShapeDtypeStruct((bsz, B_WIDTH, seq), BF16),
            jax.ShapeDtypeStruct((bsz, seq, B_WIDTH), BF16),
        ] + cast_shapes,
        scratch_shapes=[pltpu.VMEM((2, D_MODEL // LANES, SUPER, LANES), F32),
                        pltpu.VMEM((D_MODEL // LANES, rows, LANES), F32)] + raw + raw,
        compiler_params=pltpu.CompilerParams(
            dimension_semantics=("arbitrary",), vmem_limit_bytes=VMEM_LIMIT_BYTES),
        name=f"attn_proj_d{dilation}",
    )(x3, ng.reshape(1, D_MODEL), inv_freq, w_in16, w_kt16, w_in16,
      q_gain.reshape(1, HEAD_DIM), k_gain.reshape(HEAD_DIM, 1), *[job[0] for job in cast_jobs])
    return (q, kt, v), casted


def _attn_kernel(*refs):
    qkv_refs = refs[:3 * N_B_GROUPS]
    y_ref = refs[3 * N_B_GROUPS]
    scratch = refs[3 * N_B_GROUPS + 1:]
    kt_scrs, v_scrs = scratch[0:N_B_GROUPS], scratch[N_B_GROUPS:2 * N_B_GROUPS]
    o_scr, lse_scr, bias_scr = scratch[2 * N_B_GROUPS:]
    n = pl.program_id(2)

    qi = lax.broadcasted_iota(jnp.int32, (SPAN, 2 * SPAN), 0)
    ki = lax.broadcasted_iota(jnp.int32, (SPAN, 2 * SPAN), 1)
    band = (ki >= qi) & (ki <= qi + SPAN)
    bias_scr[0] = jnp.where(band, 0.0, MASK_VALUE)
    bias_scr[1] = jnp.where(band & (ki >= jnp.where(n > 0, 0, SPAN)), 0.0, MASK_VALUE)

    for g, (_, d) in enumerate(B_PATTERNS):
        q_ref, kt_ref, v_ref = qkv_refs[3 * g:3 * g + 3]
        kt_scr, v_scr = kt_scrs[g], v_scrs[g]
        per = SUPER // d
        seg = SPAN + per

        @pl.when(n == 0)
        def _():
            v_scr[...] = jnp.ones(v_scr.shape, BF16)
            for r in range(d):
                kt_scr[:, r * seg:r * seg + SPAN] = jnp.zeros((kt_scr.shape[0], SPAN), BF16)
                for hh in range(HEADS_PER_STEP):
                    v_scr[r * seg:r * seg + SPAN, 2 * hh * HEAD_DIM:(2 * hh + 1) * HEAD_DIM] = (
                        jnp.zeros((SPAN, HEAD_DIM), BF16))

        @pl.when(n > 0)
        def _():
            for r in range(d):
                kt_scr[:, r * seg:r * seg + SPAN] = kt_scr[:, r * seg + per:r * seg + per + SPAN]
                v_scr[r * seg:r * seg + SPAN, :] = v_scr[r * seg + per:r * seg + per + SPAN, :]

        for r in range(d):
            kt_scr[:, r * seg + SPAN:(r + 1) * seg] = kt_ref[:, r * per:(r + 1) * per]
            for hh in range(HEADS_PER_STEP):
                v_scr[r * seg + SPAN:(r + 1) * seg, 2 * hh * HEAD_DIM:(2 * hh + 1) * HEAD_DIM] = (
                    v_ref[r * per:(r + 1) * per, hh * HEAD_DIM:(hh + 1) * HEAD_DIM])

        for r in range(d):
            for j in range(per // SPAN):
                q_rows = slice(r * per + j * SPAN, r * per + (j + 1) * SPAN)
                keys = slice(r * seg + j * SPAN, r * seg + (j + 2) * SPAN)
                if d < MERGE_DIL:
                    assert d == 1
                    nat = slice(j * SPAN, (j + 1) * SPAN)
                else:
                    fine = d // MERGE_DIL
                    start = (r % MERGE_DIL) * (SUPER // MERGE_DIL) + fine * j * SPAN + r // MERGE_DIL
                    nat = pl.ds(start, SPAN, stride=fine) if fine > 1 else slice(start, start + SPAN)
                for hh in range(HEADS_PER_STEP):
                    cols = slice(hh * HEAD_DIM, (hh + 1) * HEAD_DIM)
                    s = _dot(q_ref[q_rows, cols], kt_scr[cols, keys]) + bias_scr[1 if j == 0 else 0]
                    m = jnp.max(s, axis=-1, keepdims=True)
                    p = jnp.exp2(s - m)
                    pv = _dot(p.astype(BF16), v_scr[keys, 2 * hh * HEAD_DIM:(2 * hh + 2) * HEAD_DIM])
                    denom = pv[:, HEAD_DIM:]
                    o_scr[g, hh, nat, :] = pv[:, :HEAD_DIM] * (1.0 / denom)
                    lse_scr[g, hh, nat, :] = m + jnp.log2(denom)

    rows = MERGE_ROWS
    for hh in range(HEADS_PER_STEP):
        for b in range(MERGE_DIL):
            for u0 in range(0, SUPER // MERGE_DIL, rows):
                natural = pl.ds(b + MERGE_DIL * u0, rows, stride=MERGE_DIL)
                ordered = slice(b * (SUPER // MERGE_DIL) + u0, b * (SUPER // MERGE_DIL) + u0 + rows)
                idx = [natural if d < MERGE_DIL else ordered for _, d in B_PATTERNS]
                lses = [lse_scr[g, hh, idx[g], :] for g in range(N_B_GROUPS)]
                m = jnp.maximum(jnp.maximum(lses[0], lses[1]), lses[2])
                es = [jnp.exp2(l - m) for l in lses]
                mixed = (es[0] * o_scr[0, hh, idx[0], :] + es[1] * o_scr[1, hh, idx[1], :]
                         + es[2] * o_scr[2, hh, idx[2], :])
                y_ref[hh, natural, :] = mixed * (1.0 / (es[0] + es[1] + es[2]))


def _attn(qkvs, bsz, seq, cast_jobs=()):
    width = HEADS_PER_STEP * HEAD_DIM
    n_super = seq // SUPER
    in_specs, operands, kt_scr, v_scr = [], [], [], []
    for (q, kt, v), (_, d) in zip(qkvs, B_PATTERNS):
        in_specs += [
            pl.BlockSpec((None, SUPER, width), lambda b, hp, n: (b, n, hp)),
            pl.BlockSpec((None, width, SUPER), lambda b, hp, n: (b, hp, n)),
            pl.BlockSpec((None, SUPER, width), lambda b, hp, n: (b, n, hp)),
        ]
        operands += [q, kt, v]
        kt_scr.append(pltpu.VMEM((width, SUPER + d * SPAN), BF16))
        v_scr.append(pltpu.VMEM((SUPER + d * SPAN, 2 * width), BF16))
    n_pairs = B_HEADS // HEADS_PER_STEP
    cast_in, cast_out, cast_shapes = _cast_specs(
        cast_jobs, bsz * n_pairs * n_super, lambda b, hp, n: (b * n_pairs + hp) * n_super + n)
    out, *casted = pl.pallas_call(
        _with_casts(_attn_kernel, len(operands), 1, len(cast_jobs)),
        grid=(bsz, n_pairs, n_super),
        in_specs=in_specs + cast_in,
        out_specs=[pl.BlockSpec((None, HEADS_PER_STEP, SUPER, HEAD_DIM),
                                lambda b, hp, n: (b, hp, n, 0))] + cast_out,
        out_shape=[jax.ShapeDtypeStruct((bsz, B_HEADS, seq, HEAD_DIM), F32)] + cast_shapes,
        scratch_shapes=kt_scr + v_scr + [
            pltpu.VMEM((N_B_GROUPS, HEADS_PER_STEP, SUPER, HEAD_DIM), F32),
            pltpu.VMEM((N_B_GROUPS, HEADS_PER_STEP, SUPER, HEAD_DIM), F32),
            pltpu.VMEM((2, SPAN, 2 * SPAN), F32),
        ],
        compiler_params=pltpu.CompilerParams(
            dimension_semantics=("arbitrary", "arbitrary", "arbitrary"),
            vmem_limit_bytes=VMEM_LIMIT_BYTES),
        name="attn",
    )(*operands, *[job[0] for job in cast_jobs])
    return out, casted


def _attn_out_kernel(x_ref, ng_ref, wz_ref, o_ref, w_out_ref, out_ref):
    x = x_ref[...]
    h = _rms_norm(x, ng_ref[...]).astype(BF16)
    z = _dot(h, wz_ref[...])
    o = jnp.concatenate([o_ref[hd] for hd in range(B_HEADS)], axis=1)
    y = (o * _silu(z)).astype(BF16)
    out_ref[...] = x + _dot(y, w_out_ref[...])


def _attn_out(x3, ng, w_in16, o, w_out16):
    bsz, seq, _ = x3.shape
    tm = LAYER_TILE
    tile = pl.BlockSpec((None, tm, D_MODEL), lambda b, t: (b, t, 0))
    return pl.pallas_call(
        _attn_out_kernel,
        grid=(bsz, seq // tm),
        in_specs=[
            tile,
            _resident((1, D_MODEL), lambda b, t: (0, 0)),
            _resident((D_MODEL, B_WIDTH), lambda b, t: (0, 3 * N_B_GROUPS)),
            pl.BlockSpec((None, B_HEADS, tm, HEAD_DIM), lambda b, t: (b, 0, t, 0)),
            _resident((B_WIDTH, D_MODEL), lambda b, t: (0, 0)),
        ],
        out_specs=tile,
        out_shape=jax.ShapeDtypeStruct((bsz, seq, D_MODEL), F32),
        compiler_params=pltpu.CompilerParams(
            dimension_semantics=("arbitrary", "arbitrary"), vmem_limit_bytes=VMEM_LIMIT_BYTES),
        name="attn_out",
    )(x3, ng.reshape(1, D_MODEL), w_in16, o, w_out16)


def _layer_b(x3, ng, w_in16, w_kt16, q_gain, k_gain, w_out16, proj_cast_jobs, attn_cast_jobs):
    bsz, seq, _ = x3.shape
    inv_freq = jnp.power(jnp.float32(ROPE_THETA), -jnp.arange(ROPE_HALF, dtype=F32) / ROPE_HALF)
    inv_freq = inv_freq.reshape(ROPE_HALF, 1)
    qkvs, casted = [], []
    for g, (window, dilation) in enumerate(B_PATTERNS):
        assert window // dilation == SPAN
        qkv, done = _attn_proj(x3, ng, inv_freq, w_in16, w_kt16, q_gain[g], k_gain[g], g, dilation,
                               proj_cast_jobs[g])
        qkvs.append(qkv)
        casted += done
    o, done = _attn(qkvs, bsz, seq, attn_cast_jobs)
    return _attn_out(x3, ng, w_in16, o, w_out16), casted + done


def kernel(x, norm_gain, a_w_in, a_v_gain, a_w_s, a_b_s, a_w_out, b_w_in, b_q_gain, b_k_gain,
           b_w_out, c_w_in, c_w_grp, c_scale, c_w_out):
    bsz, seq, d_model = x.shape
    assert d_model == D_MODEL and seq % SUPER == 0 and SUPER % TOKEN_TILE == 0
    assert norm_gain.shape[0] == 4 and a_w_in.shape[0] == 2
    c_w_grp2 = c_w_grp.reshape(c_w_grp.shape[0], N_POOL * C_GROUP, C_GROUP)
    k_cols = (N_B_GROUPS * B_WIDTH, N_B_GROUPS * B_WIDTH)

    x2, (b_w_in16, b_w_out16, b_w_kt16) = _layer_a(
        x.reshape(bsz * seq, D_MODEL), norm_gain[0], a_w_in[0].astype(BF16), a_v_gain[0], a_w_s[0],
        a_b_s[0], a_w_out[0].astype(BF16),
        cast_jobs=[(b_w_in, 0, None), (b_w_out, 0, None), (b_w_in, 0, k_cols)])
    x, (c_w_in16, c_w_grp16, c_w_out16, a_w_in16, a_w_out16) = _layer_b(
        x2.reshape(bsz, seq, D_MODEL), norm_gain[1], b_w_in16, b_w_kt16, b_q_gain[0], b_k_gain[0],
        b_w_out16,
        proj_cast_jobs=[[(c_w_in, 0, None)], [(c_w_grp2, 0, None)], [(c_w_out, 0, None)]],
        attn_cast_jobs=[(a_w_in, 1, None), (a_w_out, 1, None)])
    x = _layer_c(x, norm_gain[2], c_w_in16, c_w_grp16.reshape(N_POOL, C_GROUP, C_GROUP), c_scale[0],
                 c_w_out16)
    x2, _ = _layer_a(x.reshape(bsz * seq, D_MODEL), norm_gain[3], a_w_in16, a_v_gain[1], a_w_s[1],
                     a_b_s[1], a_w_out16)
    return x2.reshape(bsz, seq, D_MODEL)
```

```python
import functools
import math

import jax
import jax.numpy as jnp
import numpy as np
from jax import lax
from jax.experimental import pallas as pl
from jax.experimental.pallas import tpu as pltpu

D_MODEL = 1024
EPS = 1e-6
CHUNK = 128
A_WIDTH = 2 * D_MODEL
A_GROUPS = 8
A_GROUP_DIM = A_WIDTH // A_GROUPS
HEAD_DIM = 128
B_HEADS = D_MODEL // HEAD_DIM
B_PATTERNS = ((128, 1), (512, 4), (2048, 16))
N_B_GROUPS = len(B_PATTERNS)
B_WIDTH = B_HEADS * HEAD_DIM
ROPE_DIM = HEAD_DIM // 4
ROPE_HALF = ROPE_DIM // 2
ROPE_THETA = 500000.0
Q_PRESCALE = math.log2(math.e) / math.sqrt(HEAD_DIM)
SPAN = 128
SUPER = SPAN * max(d for _, d in B_PATTERNS)
HEADS_PER_STEP = 2
MERGE_DIL = 4
POOL_SIZES = (2, 4, 8, 16)
N_POOL = len(POOL_SIZES)
C_WIDTH = 2 * D_MODEL
C_GROUP = C_WIDTH // N_POOL
HALO = 32

LANES = 128
BF16_ROWS = 16
MERGE_ROWS = 256
VMEM_LIMIT_BYTES = 60 * 1024 * 1024

TOKEN_TILE = 512
LAYER_TILE = 1024
MASK_VALUE = -1e30

F32 = jnp.float32
BF16 = jnp.bfloat16


def _resident(shape, index_map):
    return pl.BlockSpec(shape, index_map, pipeline_mode=pl.Buffered(1))


def _rms_norm(x, gain):
    return x * lax.rsqrt(jnp.mean(x * x, axis=-1, keepdims=True) + EPS) * gain


def _dot(a, b):
    return jnp.dot(a, b, preferred_element_type=F32)


def _dot_nt(a, b):
    return lax.dot_general(a, b, (((1,), (1,)), ((), ())), preferred_element_type=F32)


def _silu(z):
    return z * jax.nn.sigmoid(z)


def _cast_specs(jobs, n_steps, flat_step):
    in_specs, out_specs, out_shapes = [], [], []
    for arr, layer, col_range in jobs:
        _, rows, cols = arr.shape
        if col_range is None:
            n_chunks = n_steps
            chunk = rows // n_chunks
            assert chunk * n_chunks == rows and chunk % BF16_ROWS == 0, (arr.shape, n_steps)
            last = n_chunks - 1
            in_specs.append(pl.BlockSpec(
                (None, chunk, cols),
                lambda *ids, layer=layer, last=last: (layer, jnp.minimum(flat_step(*ids), last), 0)))
            out_specs.append(pl.BlockSpec(
                (chunk, cols), lambda *ids, last=last: (jnp.minimum(flat_step(*ids), last), 0)))
            out_shapes.append(jax.ShapeDtypeStruct((rows, cols), BF16))
        else:
            start, width = col_range
            n_chunks = rows // LANES
            assert n_chunks <= n_steps and start % width == 0, (arr.shape, n_steps, col_range)
            last = n_chunks - 1
            in_specs.append(pl.BlockSpec(
                (None, LANES, width),
                lambda *ids, layer=layer, last=last, blk=start // width: (
                    layer, jnp.minimum(flat_step(*ids), last), blk)))
            out_specs.append(pl.BlockSpec(
                (width, LANES), lambda *ids, last=last: (0, jnp.minimum(flat_step(*ids), last))))
            out_shapes.append(jax.ShapeDtypeStruct((width, rows), BF16))
    return in_specs, out_specs, out_shapes


def _with_casts(body, n_in, n_out, n_jobs):
    if n_jobs == 0:
        return body

    def wrapped(*refs):
        ins, cast_in = refs[:n_in], refs[n_in:n_in + n_jobs]
        outs = refs[n_in + n_jobs:n_in + n_jobs + n_out]
        cast_out = refs[n_in + n_jobs + n_out:n_in + 2 * n_jobs + n_out]
        body(*ins, *outs, *refs[n_in + 2 * n_jobs + n_out:])
        for src, dst in zip(cast_in, cast_out):
            transposed = src.shape != dst.shape
            dst[...] = (src[...].T if transposed else src[...]).astype(BF16)

    return wrapped


def _layer_a_kernel(x_ref, ng_ref, w_in_ref, vg_ref, ws_ref, bs_ref, w_out_ref,
                    o_ref, vn_ref, y_ref):
    x = x_ref[...]
    h = _rms_norm(x, ng_ref[...]).astype(BF16)
    v = _dot(h, w_in_ref[:, A_WIDTH:2 * A_WIDTH])
    vn_ref[...] = _rms_norm(v, vg_ref[...]).astype(BF16)
    row = lax.broadcasted_iota(jnp.int32, (CHUNK, CHUNK), 0)
    col = lax.broadcasted_iota(jnp.int32, (CHUNK, CHUNK), 1)
    causal = col <= row
    n_chunks = x.shape[0] // CHUNK
    for g in range(A_GROUPS):
        cols = slice(g * A_GROUP_DIM, (g + 1) * A_GROUP_DIM)
        u = _dot(h, w_in_ref[:, cols])
        z = _dot(h, w_in_ref[:, 2 * A_WIDTH + g * A_GROUP_DIM:2 * A_WIDTH + (g + 1) * A_GROUP_DIM])
        ws = jnp.where(causal, ws_ref[g], 0.0).astype(BF16)
        bias = bs_ref[:, g:g + 1]
        mixed = jnp.concatenate(
            [_dot(ws, vn_ref[c * CHUNK:(c + 1) * CHUNK, cols]) + bias for c in range(n_chunks)],
            axis=0)
        y_ref[:, cols] = (u * mixed * _silu(z)).astype(BF16)
    o_ref[...] = x + _dot(y_ref[...], w_out_ref[...])


def _layer_a(x2, ng, w_in16, v_gain, w_s, b_s, w_out16, cast_jobs=()):
    tokens = x2.shape[0]
    tm = LAYER_TILE
    n_steps = tokens // tm
    cast_in, cast_out, cast_shapes = _cast_specs(cast_jobs, n_steps, lambda i: i)
    out, *casted = pl.pallas_call(
        _with_casts(_layer_a_kernel, 7, 1, len(cast_jobs)),
        grid=(n_steps,),
        in_specs=[
            pl.BlockSpec((tm, D_MODEL), lambda i: (i, 0)),
            _resident((1, D_MODEL), lambda i: (0, 0)),
            _resident((D_MODEL, 3 * A_WIDTH), lambda i: (0, 0)),
            _resident((1, A_WIDTH), lambda i: (0, 0)),
            _resident((A_GROUPS, CHUNK, CHUNK), lambda i: (0, 0, 0)),
            _resident((CHUNK, A_GROUPS), lambda i: (0, 0)),
            _resident((A_WIDTH, D_MODEL), lambda i: (0, 0)),
        ] + cast_in,
        out_specs=[pl.BlockSpec((tm, D_MODEL), lambda i: (i, 0))] + cast_out,
        out_shape=[jax.ShapeDtypeStruct((tokens, D_MODEL), F32)] + cast_shapes,
        scratch_shapes=[pltpu.VMEM((tm, A_WIDTH), BF16), pltpu.VMEM((tm, A_WIDTH), BF16)],
        compiler_params=pltpu.CompilerParams(
            dimension_semantics=("arbitrary",), vmem_limit_bytes=VMEM_LIMIT_BYTES),
        name="layer_a",
    )(x2, ng.reshape(1, D_MODEL), w_in16, v_gain.reshape(1, A_WIDTH), w_s, b_s.T, w_out16,
      *[job[0] for job in cast_jobs])
    return out, casted


def _layer_c_kernel(x_ref, ng_ref, w_in_ref, w_grp_ref, scale_ref, w_out_ref,
                    o_ref, ext_ref, sum_ref, y_ref):
    t = pl.program_id(1)
    tm = x_ref.shape[0]
    x = x_ref[...]
    h = _rms_norm(x, ng_ref[...]).astype(BF16)

    @pl.when(t == 0)
    def _():
        ext_ref[0:HALO, :] = jnp.zeros((HALO, C_WIDTH), F32)

    @pl.when(t > 0)
    def _():
        ext_ref[0:HALO, :] = ext_ref[tm:tm + HALO, :]

    ext_ref[HALO:HALO + tm, :] = _dot(h, w_in_ref[:, 0:C_WIDTH])
    pos = t * tm + lax.broadcasted_iota(jnp.int32, (tm, 1), 0)
    for g, window in enumerate(POOL_SIZES):
        cols = slice(g * C_GROUP, (g + 1) * C_GROUP)
        xc = ext_ref[HALO:HALO + tm, cols]
        span, lo = 1, 8
        while span < window:
            src = ext_ref if span == 1 else sum_ref
            src_cols = cols if span == 1 else slice(None)
            sum_ref[lo:HALO + tm, :] = (src[lo:HALO + tm, src_cols]
                                        + src[lo - span:HALO + tm - span, src_cols])
            span, lo = 2 * span, lo + 8
        assert lo <= HALO + 8
        cnt = jnp.minimum(pos + 1, window).astype(F32)
        diff = (sum_ref[HALO:HALO + tm, :] / cnt - xc).astype(BF16)
        mixed = _dot(diff, w_grp_ref[g]) * scale_ref[:, cols]
        z = _dot(h, w_in_ref[:, C_WIDTH + g * C_GROUP:C_WIDTH + (g + 1) * C_GROUP])
        y_ref[:, cols] = (mixed * _silu(z)).astype(BF16)
    o_ref[...] = x + _dot(y_ref[...], w_out_ref[...])


def _layer_c(x3, ng, w_in16, w_grp16, scale, w_out16):
    bsz, seq, _ = x3.shape
    tm = LAYER_TILE
    return pl.pallas_call(
        _layer_c_kernel,
        grid=(bsz, seq // tm),
        in_specs=[
            pl.BlockSpec((None, tm, D_MODEL), lambda b, t: (b, t, 0)),
            _resident((1, D_MODEL), lambda b, t: (0, 0)),
            _resident((D_MODEL, 2 * C_WIDTH), lambda b, t: (0, 0)),
            _resident((N_POOL, C_GROUP, C_GROUP), lambda b, t: (0, 0, 0)),
            _resident((1, C_WIDTH), lambda b, t: (0, 0)),
            _resident((C_WIDTH, D_MODEL), lambda b, t: (0, 0)),
        ],
        out_specs=pl.BlockSpec((None, tm, D_MODEL), lambda b, t: (b, t, 0)),
        out_shape=jax.ShapeDtypeStruct((bsz, seq, D_MODEL), F32),
        scratch_shapes=[pltpu.VMEM((tm + HALO, C_WIDTH), F32), pltpu.VMEM((tm + HALO, C_GROUP), F32),
                        pltpu.VMEM((tm, C_WIDTH), BF16)],
        compiler_params=pltpu.CompilerParams(
            dimension_semantics=("arbitrary", "arbitrary"), vmem_limit_bytes=VMEM_LIMIT_BYTES),
        name="layer_c",
    )(x3, ng.reshape(1, D_MODEL), w_in16, w_grp16, scale.reshape(1, C_WIDTH), w_out16)


def _attn_proj_kernel(x_ref, ng_ref, freq_ref, wq_ref, wkt_ref, wv_ref, qg_ref, kg_ref,
                      q_ref, kt_ref, v_ref, h_scr, tmp_scr, q_a, kt_a, q_b, kt_b,
                      *, dilation, n_steps, n_super):
    t = pl.program_id(0)
    rows = q_ref.shape[0]
    n_chunk = SUPER // rows
    per = SUPER // dilation
    n_cols = D_MODEL // LANES
    s = t - n_chunk
    cur = jnp.clip(s, 0, n_steps - 1)
    c = cur % n_chunk
    slot = (cur // n_chunk) % 2
    prev = jnp.clip(s - 1, 0, n_steps - 1)
    c_prev = prev % n_chunk
    n_prev = (prev // n_chunk) % n_super

    @pl.when(s == 0)
    def _():
        q_b[...] = jnp.zeros(q_b.shape, q_b.dtype)
        kt_b[...] = jnp.zeros(kt_b.shape, kt_b.dtype)

    coarse = dilation > MERGE_DIL
    quarter_rows = SUPER // MERGE_DIL

    def normalise_tile():
        tile = jnp.minimum(t, n_steps - 1)
        half = (tile // n_chunk) % 2
        hh = _rms_norm(x_ref[...], ng_ref[...])
        if not coarse:
            rs = pl.ds(pl.multiple_of((tile % n_chunk) * rows, rows), rows)
            for j in range(n_cols):
                h_scr[half, j, rs, :] = hh[:, j * LANES:(j + 1) * LANES]
            return
        sub = rows // MERGE_DIL
        base = pl.multiple_of((tile % n_chunk) * sub, sub)
        for j in range(n_cols):
            tmp_scr[j] = hh[:, j * LANES:(j + 1) * LANES]
        for j in range(n_cols):
            for b in range(MERGE_DIL):
                h_scr[half, j, pl.ds(b * quarter_rows + base, sub), :] = (
                    tmp_scr[j, pl.ds(b, sub, stride=MERGE_DIL), :])

    def gather(j):
        if dilation == 1:
            return h_scr[slot, j, pl.ds(pl.multiple_of(c * rows, rows), rows), :]
        run = min(per, rows)
        if per > rows:
            raise NotImplementedError("a residue class longer than the chunk needs a row offset")
        if not coarse:
            first = c * (rows // run)
            pieces = [h_scr[slot, j, pl.ds(first + rr, run, stride=dilation), :] for rr in range(rows // run)]
            return pieces[0] if len(pieces) == 1 else jnp.concatenate(pieces, axis=0)
        fine = dilation // MERGE_DIL
        if rows // run != MERGE_DIL:
            raise NotImplementedError("chunk / residue-run geometry not covered")
        pieces = [h_scr[slot, j, pl.ds(rr * quarter_rows + c, run, stride=fine), :]
                  for rr in range(MERGE_DIL)]
        return jnp.concatenate(pieces, axis=0)

    def project(q_raw, kt_raw):
        h = jnp.concatenate([gather(j) for j in range(n_cols)], axis=1).astype(BF16)
        q_raw[...] = _dot(h, wq_ref[...])
        kt_raw[...] = _dot_nt(wkt_ref[...], h)
        v_ref[...] = _dot(h, wv_ref[...]).astype(BF16)

    def finish(q_raw, kt_raw):
        p = c_prev * rows + lax.broadcasted_iota(jnp.int32, (1, rows), 1)
        residue = lax.shift_right_logical(p, per.bit_length() - 1)
        strided = jnp.bitwise_and(p, per - 1)
        pos = n_prev * SUPER + strided * dilation + residue
        ang = pos.astype(F32) * freq_ref[...]
        cos_t, sin_t = jnp.cos(ang), jnp.sin(ang)

        for hd in range(B_HEADS):
            slab = kt_raw[hd * HEAD_DIM:(hd + 1) * HEAD_DIM, :]
            kn = slab * lax.rsqrt(jnp.mean(slab * slab, axis=0, keepdims=True) + EPS) * kg_ref[...]
            x1, x2 = kn[0:ROPE_HALF, :], kn[ROPE_HALF:ROPE_DIM, :]
            rotated = jnp.concatenate(
                [x1 * cos_t - x2 * sin_t, x2 * cos_t + x1 * sin_t, kn[ROPE_DIM:, :]], axis=0)
            kt_ref[hd * HEAD_DIM:(hd + 1) * HEAD_DIM, :] = rotated.astype(BF16)

        rest = HEAD_DIM - ROPE_DIM
        cos_tab = jnp.concatenate([cos_t, cos_t, jnp.ones((rest, rows), F32)], axis=0).T * Q_PRESCALE
        sin_tab = jnp.concatenate([-sin_t, sin_t, jnp.zeros((rest, rows), F32)], axis=0).T * Q_PRESCALE
        lane = lax.broadcasted_iota(jnp.int32, (rows, HEAD_DIM), 1)
        for hd in range(B_HEADS):
            cols = slice(hd * HEAD_DIM, (hd + 1) * HEAD_DIM)
            qn = _rms_norm(q_raw[:, cols], qg_ref[...])
            partner = jnp.where(lane < ROPE_HALF, pltpu.roll(qn, HEAD_DIM - ROPE_HALF, 1),
                                pltpu.roll(qn, ROPE_HALF, 1))
            q_ref[:, cols] = (qn * cos_tab + partner * sin_tab).astype(BF16)

    @pl.when(s < 0)
    def _():
        normalise_tile()

    @pl.when((s >= 0) & (s < n_steps) & (s % 2 == 0))
    def _():
        project(q_a, kt_a)
        finish(q_b, kt_b)
        normalise_tile()

    @pl.when((s >= 0) & (s < n_steps) & (s % 2 == 1))
    def _():
        project(q_b, kt_b)
        finish(q_a, kt_a)
        normalise_tile()

    @pl.when(s == n_steps)
    def _():
        if (n_steps - 1) % 2 == 0:
            finish(q_a, kt_a)
        else:
            finish(q_b, kt_b)


def _attn_proj(x3, ng, inv_freq, w_in16, w_kt16, q_gain, k_gain, group, dilation, cast_jobs=()):
    bsz, seq, _ = x3.shape
    rows = TOKEN_TILE
    n_super, n_chunk = seq // SUPER, SUPER // rows
    per_batch = n_super * n_chunk
    n_steps = bsz * per_batch
    tile = lambda t: jnp.minimum(t, n_steps - 1)
    cur = lambda t: jnp.clip(t - n_chunk, 0, n_steps - 1)
    prev = lambda t: jnp.clip(t - n_chunk - 1, 0, n_steps - 1)
    raw = [pltpu.VMEM((rows, B_WIDTH), F32), pltpu.VMEM((B_WIDTH, rows), F32)]
    cast_in, cast_out, cast_shapes = _cast_specs(cast_jobs, n_steps, lambda s: s)
    body = functools.partial(_attn_proj_kernel, dilation=dilation, n_steps=n_steps, n_super=n_super)
    q, kt, v, *casted = pl.pallas_call(
        _with_casts(body, 8, 3, len(cast_jobs)),
        grid=(n_chunk + n_steps + 1,),
        in_specs=[
            pl.BlockSpec((None, rows, D_MODEL), lambda t: (tile(t) // per_batch, tile(t) % per_batch, 0)),
            _resident((1, D_MODEL), lambda s: (0, 0)),
            _resident((ROPE_HALF, 1), lambda s: (0, 0)),
            _resident((D_MODEL, B_WIDTH), lambda s: (0, group)),
            _resident((B_WIDTH, D_MODEL), lambda s: (group, 0)),
            _resident((D_MODEL, B_WIDTH), lambda s: (0, 2 * N_B_GROUPS + group)),
            _resident((1, HEAD_DIM), lambda s: (0, 0)),
            _resident((HEAD_DIM, 1), lambda s: (0, 0)),
        ] + cast_in,
        out_specs=[
            pl.BlockSpec((None, rows, B_WIDTH), lambda s: (prev(s) // per_batch, prev(s) % per_batch, 0)),
            pl.BlockSpec((None, B_WIDTH, rows), lambda s: (prev(s) // per_batch, 0, prev(s) % per_batch)),
            pl.BlockSpec((None, rows, B_WIDTH), lambda s: (cur(s) // per_batch, cur(s) % per_batch, 0)),
        ] + cast_out,
        out_shape=[
            jax.ShapeDtypeStruct((bsz, seq, B_WIDTH), BF16),
            jax.ShapeDtypeStruct((bsz, B_WIDTH, seq), BF16),
            jax.ShapeDtypeStruct((bsz, seq, B_WIDTH), BF16),
        ] + cast_shapes,
        scratch_shapes=[pltpu.VMEM((2, D_MODEL // LANES, SUPER, LANES), F32),
                        pltpu.VMEM((D_MODEL // LANES, rows, LANES), F32)] + raw + raw,
        compiler_params=pltpu.CompilerParams(
            dimension_semantics=("arbitrary",), vmem_limit_bytes=VMEM_LIMIT_BYTES),
        name=f"attn_proj_d{dilation}",
    )(x3, ng.reshape(1, D_MODEL), inv_freq, w_in16, w_kt16, w_in16,
      q_gain.reshape(1, HEAD_DIM), k_gain.reshape(HEAD_DIM, 1), *[job[0] for job in cast_jobs])
    return (q, kt, v), casted


def _attn_kernel(*refs):
    qkv_refs = refs[:3 * N_B_GROUPS]
    y_ref = refs[3 * N_B_GROUPS]
    scratch = refs[3 * N_B_GROUPS + 1:]
    kt_scrs, v_scrs = scratch[0:N_B_GROUPS], scratch[N_B_GROUPS:2 * N_B_GROUPS]
    o_scr, lse_scr, bias_scr = scratch[2 * N_B_GROUPS:]
    n = pl.program_id(2)

    qi = lax.broadcasted_iota(jnp.int32, (SPAN, 2 * SPAN), 0)
    ki = lax.broadcasted_iota(jnp.int32, (SPAN, 2 * SPAN), 1)
    band = (ki >= qi) & (ki <= qi + SPAN)
    bias_scr[0] = jnp.where(band, 0.0, MASK_VALUE)
    bias_scr[1] = jnp.where(band & (ki >= jnp.where(n > 0, 0, SPAN)), 0.0, MASK_VALUE)

    for g, (_, d) in enumerate(B_PATTERNS):
        q_ref, kt_ref, v_ref = qkv_refs[3 * g:3 * g + 3]
        kt_scr, v_scr = kt_scrs[g], v_scrs[g]
        per = SUPER // d
        seg = SPAN + per

        @pl.when(n == 0)
        def _():
            v_scr[...] = jnp.ones(v_scr.shape, BF16)
            for r in range(d):
                kt_scr[:, r * seg:r * seg + SPAN] = jnp.zeros((kt_scr.shape[0], SPAN), BF16)
                for hh in range(HEADS_PER_STEP):
                    v_scr[r * seg:r * seg + SPAN, 2 * hh * HEAD_DIM:(2 * hh + 1) * HEAD_DIM] = (
                        jnp.zeros((SPAN, HEAD_DIM), BF16))

        @pl.when(n > 0)
        def _():
            for r in range(d):
                kt_scr[:, r * seg:r * seg + SPAN] = kt_scr[:, r * seg + per:r * seg + per + SPAN]
                v_scr[r * seg:r * seg + SPAN, :] = v_scr[r * seg + per:r * seg + per + SPAN, :]

        for r in range(d):
            kt_scr[:, r * seg + SPAN:(r + 1) * seg] = kt_ref[:, r * per:(r + 1) * per]
            for hh in range(HEADS_PER_STEP):
                v_scr[r * seg + SPAN:(r + 1) * seg, 2 * hh * HEAD_DIM:(2 * hh + 1) * HEAD_DIM] = (
                    v_ref[r * per:(r + 1) * per, hh * HEAD_DIM:(hh + 1) * HEAD_DIM])

        for r in range(d):
            for j in range(per // SPAN):
                q_rows = slice(r * per + j * SPAN, r * per + (j + 1) * SPAN)
                keys = slice(r * seg + j * SPAN, r * seg + (j + 2) * SPAN)
                if d < MERGE_DIL:
                    assert d == 1
                    nat = slice(j * SPAN, (j + 1) * SPAN)
                else:
                    fine = d // MERGE_DIL
                    start = (r % MERGE_DIL) * (SUPER // MERGE_DIL) + fine * j * SPAN + r // MERGE_DIL
                    nat = pl.ds(start, SPAN, stride=fine) if fine > 1 else slice(start, start + SPAN)
                for hh in range(HEADS_PER_STEP):
                    cols = slice(hh * HEAD_DIM, (hh + 1) * HEAD_DIM)
                    s = _dot(q_ref[q_rows, cols], kt_scr[cols, keys]) + bias_scr[1 if j == 0 else 0]
                    m = jnp.max(s, axis=-1, keepdims=True)
                    p = jnp.exp2(s - m)
                    pv = _dot(p.astype(BF16), v_scr[keys, 2 * hh * HEAD_DIM:(2 * hh + 2) * HEAD_DIM])
                    denom = pv[:, HEAD_DIM:]
                    o_scr[g, hh, nat, :] = pv[:, :HEAD_DIM] * (1.0 / denom)
                    lse_scr[g, hh, nat, :] = m + jnp.log2(denom)

    rows = MERGE_ROWS
    for hh in range(HEADS_PER_STEP):
        for b in range(MERGE_DIL):
            for u0 in range(0, SUPER // MERGE_DIL, rows):
                natural = pl.ds(b + MERGE_DIL * u0, rows, stride=MERGE_DIL)
                ordered = slice(b * (SUPER // MERGE_DIL) + u0, b * (SUPER // MERGE_DIL) + u0 + rows)
                idx = [natural if d < MERGE_DIL else ordered for _, d in B_PATTERNS]
                lses = [lse_scr[g, hh, idx[g], :] for g in range(N_B_GROUPS)]
                m = jnp.maximum(jnp.maximum(lses[0], lses[1]), lses[2])
                es = [jnp.exp2(l - m) for l in lses]
                mixed = (es[0] * o_scr[0, hh, idx[0], :] + es[1] * o_scr[1, hh, idx[1], :]
                         + es[2] * o_scr[2, hh, idx[2], :])
                y_ref[hh, natural, :] = mixed * (1.0 / (es[0] + es[1] + es[2]))


def _attn(qkvs, bsz, seq, cast_jobs=()):
    width = HEADS_PER_STEP * HEAD_DIM
    n_super = seq // SUPER
    in_specs, operands, kt_scr, v_scr = [], [], [], []
    for (q, kt, v), (_, d) in zip(qkvs, B_PATTERNS):
        in_specs += [
            pl.BlockSpec((None, SUPER, width), lambda b, hp, n: (b, n, hp)),
            pl.BlockSpec((None, width, SUPER), lambda b, hp, n: (b, hp, n)),
            pl.BlockSpec((None, SUPER, width), lambda b, hp, n: (b, n, hp)),
        ]
        operands += [q, kt, v]
        kt_scr.append(pltpu.VMEM((width, SUPER + d * SPAN), BF16))
        v_scr.append(pltpu.VMEM((SUPER + d * SPAN, 2 * width), BF16))
    n_pairs = B_HEADS // HEADS_PER_STEP
    cast_in, cast_out, cast_shapes = _cast_specs(
        cast_jobs, bsz * n_pairs * n_super, lambda b, hp, n: (b * n_pairs + hp) * n_super + n)
    out, *casted = pl.pallas_call(
        _with_casts(_attn_kernel, len(operands), 1, len(cast_jobs)),
        grid=(bsz, n_pairs, n_super),
        in_specs=in_specs + cast_in,
        out_specs=[pl.BlockSpec((None, HEADS_PER_STEP, SUPER, HEAD_DIM),
                                lambda b, hp, n: (b, hp, n, 0))] + cast_out,
        out_shape=[jax.ShapeDtypeStruct((bsz, B_HEADS, seq, HEAD_DIM), F32)] + cast_shapes,
        scratch_shapes=kt_scr + v_scr + [
            pltpu.VMEM((N_B_GROUPS, HEADS_PER_STEP, SUPER, HEAD_DIM), F32),
            pltpu.VMEM((N_B_GROUPS, HEADS_PER_STEP, SUPER, HEAD_DIM), F32),
            pltpu.VMEM((2, SPAN, 2 * SPAN), F32),
        ],
        compiler_params=pltpu.CompilerParams(
            dimension_semantics=("arbitrary", "arbitrary", "arbitrary"),
            vmem_limit_bytes=VMEM_LIMIT_BYTES),
        name="attn",
    )(*operands, *[job[0] for job in cast_jobs])
    return out, casted


def _attn_out_kernel(x_ref, ng_ref, wz_ref, o_ref, w_out_ref, out_ref):
    x = x_ref[...]
    h = _rms_norm(x, ng_ref[...]).astype(BF16)
    z = _dot(h, wz_ref[...])
    o = jnp.concatenate([o_ref[hd] for hd in range(B_HEADS)], axis=1)
    y = (o * _silu(z)).astype(BF16)
    out_ref[...] = x + _dot(y, w_out_ref[...])


def _attn_out(x3, ng, w_in16, o, w_out16):
    bsz, seq, _ = x3.shape
    tm = LAYER_TILE
    tile = pl.BlockSpec((None, tm, D_MODEL), lambda b, t: (b, t, 0))
    return pl.pallas_call(
        _attn_out_kernel,
        grid=(bsz, seq // tm),
        in_specs=[
            tile,
            _resident((1, D_MODEL), lambda b, t: (0, 0)),
            _resident((D_MODEL, B_WIDTH), lambda b, t: (0, 3 * N_B_GROUPS)),
            pl.BlockSpec((None, B_HEADS, tm, HEAD_DIM), lambda b, t: (b, 0, t, 0)),
            _resident((B_WIDTH, D_MODEL), lambda b, t: (0, 0)),
        ],
        out_specs=tile,
        out_shape=jax.ShapeDtypeStruct((bsz, seq, D_MODEL), F32),
        compiler_params=pltpu.CompilerParams(
            dimension_semantics=("arbitrary", "arbitrary"), vmem_limit_bytes=VMEM_LIMIT_BYTES),
        name="attn_out",
    )(x3, ng.reshape(1, D_MODEL), w_in16, o, w_out16)


def _layer_b(x3, ng, w_in16, w_kt16, q_gain, k_gain, w_out16, proj_cast_jobs, attn_cast_jobs):
    bsz, seq, _ = x3.shape
    inv_freq = jnp.power(jnp.float32(ROPE_THETA), -jnp.arange(ROPE_HALF, dtype=F32) / ROPE_HALF)
    inv_freq = inv_freq.reshape(ROPE_HALF, 1)
    qkvs, casted = [], []
    for g, (window, dilation) in enumerate(B_PATTERNS):
        assert window // dilation == SPAN
        qkv, done = _attn_proj(x3, ng, inv_freq, w_in16, w_kt16, q_gain[g], k_gain[g], g, dilation,
                               proj_cast_jobs[g])
        qkvs.append(qkv)
        casted += done
    o, done = _attn(qkvs, bsz, seq, attn_cast_jobs)
    return _attn_out(x3, ng, w_in16, o, w_out16), casted + done


def kernel(x, norm_gain, a_w_in, a_v_gain, a_w_s, a_b_s, a_w_out, b_w_in, b_q_gain, b_k_gain,
           b_w_out, c_w_in, c_w_grp, c_scale, c_w_out):
    bsz, seq, d_model = x.shape
    assert d_model == D_MODEL and seq % SUPER == 0 and SUPER % TOKEN_TILE == 0
    assert norm_gain.shape[0] == 4 and a_w_in.shape[0] == 2
    c_w_grp2 = c_w_grp.reshape(c_w_grp.shape[0], N_POOL * C_GROUP, C_GROUP)
    k_cols = (N_B_GROUPS * B_WIDTH, N_B_GROUPS * B_WIDTH)

    x2, (b_w_in16, b_w_out16, b_w_kt16) = _layer_a(
        x.reshape(bsz * seq, D_MODEL), norm_gain[0], a_w_in[0].astype(BF16), a_v_gain[0], a_w_s[0],
        a_b_s[0], a_w_out[0].astype(BF16),
        cast_jobs=[(b_w_in, 0, None), (b_w_out, 0, None), (b_w_in, 0, k_cols)])
    x, (c_w_in16, c_w_grp16, c_w_out16, a_w_in16, a_w_out16) = _layer_b(
        x2.reshape(bsz, seq, D_MODEL), norm_gain[1], b_w_in16, b_w_kt16, b_q_gain[0], b_k_gain[0],
        b_w_out16,
        proj_cast_jobs=[[(c_w_in, 0, None)], [(c_w_grp2, 0, None)], [(c_w_out, 0, None)]],
        attn_cast_jobs=[(a_w_in, 1, None), (a_w_out, 1, None)])
    x = _layer_c(x, norm_gain[2], c_w_in16, c_w_grp16.reshape(N_POOL, C_GROUP, C_GROUP), c_scale[0],
                 c_w_out16)
    x2, _ = _layer_a(x.reshape(bsz * seq, D_MODEL), norm_gain[3], a_w_in16, a_v_gain[1], a_w_s[1],
                     a_b_s[1], a_w_out16)
    return x2.reshape(bsz, seq, D_MODEL)
```

```python
import functools
import math

import jax
import jax.numpy as jnp
import numpy as np
from jax import lax
from jax.experimental import pallas as pl
from jax.experimental.pallas import tpu as pltpu

D_MODEL = 1024
EPS = 1e-6
CHUNK = 128
A_WIDTH = 2 * D_MODEL
A_GROUPS = 8
A_GROUP_DIM = A_WIDTH // A_GROUPS
HEAD_DIM = 128
B_HEADS = D_MODEL // HEAD_DIM
B_PATTERNS = ((128, 1), (512, 4), (2048, 16))
N_B_GROUPS = len(B_PATTERNS)
B_WIDTH = B_HEADS * HEAD_DIM
ROPE_DIM = HEAD_DIM // 4
ROPE_HALF = ROPE_DIM // 2
ROPE_THETA = 500000.0
Q_PRESCALE = math.log2(math.e) / math.sqrt(HEAD_DIM)
SPAN = 128
SUPER = SPAN * max(d for _, d in B_PATTERNS)
HEADS_PER_STEP = 2
MERGE_DIL = 4
POOL_SIZES = (2, 4, 8, 16)
N_POOL = len(POOL_SIZES)
C_WIDTH = 2 * D_MODEL
C_GROUP = C_WIDTH // N_POOL
HALO = 32

LANES = 128
BF16_ROWS = 16
MERGE_ROWS = 256
VMEM_LIMIT_BYTES = 60 * 1024 * 1024

TOKEN_TILE = 512
LAYER_TILE = 1024
MASK_VALUE = -1e30

F32 = jnp.float32
BF16 = jnp.bfloat16


def _resident(shape, index_map):
    return pl.BlockSpec(shape, index_map, pipeline_mode=pl.Buffered(1))


def _rms_norm(x, gain):
    return x * lax.rsqrt(jnp.mean(x * x, axis=-1, keepdims=True) + EPS) * gain


def _dot(a, b):
    return jnp.dot(a, b, preferred_element_type=F32)


def _dot_nt(a, b):
    return lax.dot_general(a, b, (((1,), (1,)), ((), ())), preferred_element_type=F32)


def _silu(z):
    return z * jax.nn.sigmoid(z)


def _cast_specs(jobs, n_steps, flat_step):
    in_specs, out_specs, out_shapes = [], [], []
    for arr, layer, col_range in jobs:
        _, rows, cols = arr.shape
        if col_range is None:
            n_chunks = n_steps
            chunk = rows // n_chunks
            assert chunk * n_chunks == rows and chunk % BF16_ROWS == 0, (arr.shape, n_steps)
            last = n_chunks - 1
            in_specs.append(pl.BlockSpec(
                (None, chunk, cols),
                lambda *ids, layer=layer, last=last: (layer, jnp.minimum(flat_step(*ids), last), 0)))
            out_specs.append(pl.BlockSpec(
                (chunk, cols), lambda *ids, last=last: (jnp.minimum(flat_step(*ids), last), 0)))
            out_shapes.append(jax.ShapeDtypeStruct((rows, cols), BF16))
        else:
            start, width = col_range
            n_chunks = rows // LANES
            assert n_chunks <= n_steps and start % width == 0, (arr.shape, n_steps, col_range)
            last = n_chunks - 1
            in_specs.append(pl.BlockSpec(
                (None, LANES, width),
                lambda *ids, layer=layer, last=last, blk=start // width: (
                    layer, jnp.minimum(flat_step(*ids), last), blk)))
            out_specs.append(pl.BlockSpec(
                (width, LANES), lambda *ids, last=last: (0, jnp.minimum(flat_step(*ids), last))))
            out_shapes.append(jax.ShapeDtypeStruct((width, rows), BF16))
    return in_specs, out_specs, out_shapes


def _with_casts(body, n_in, n_out, n_jobs):
    if n_jobs == 0:
        return body

    def wrapped(*refs):
        ins, cast_in = refs[:n_in], refs[n_in:n_in + n_jobs]
        outs = refs[n_in + n_jobs:n_in + n_jobs + n_out]
        cast_out = refs[n_in + n_jobs + n_out:n_in + 2 * n_jobs + n_out]
        body(*ins, *outs, *refs[n_in + 2 * n_jobs + n_out:])
        for src, dst in zip(cast_in, cast_out):
            transposed = src.shape != dst.shape
            dst[...] = (src[...].T if transposed else src[...]).astype(BF16)

    return wrapped


def _layer_a_kernel(x_ref, ng_ref, w_in_ref, vg_ref, ws_ref, bs_ref, w_out_ref,
                    o_ref, vn_ref, y_ref):
    x = x_ref[...]
    h = _rms_norm(x, ng_ref[...]).astype(BF16)
    v = _dot(h, w_in_ref[:, A_WIDTH:2 * A_WIDTH])
    vn_ref[...] = _rms_norm(v, vg_ref[...]).astype(BF16)
    row = lax.broadcasted_iota(jnp.int32, (CHUNK, CHUNK), 0)
    col = lax.broadcasted_iota(jnp.int32, (CHUNK, CHUNK), 1)
    causal = col <= row
    n_chunks = x.shape[0] // CHUNK
    for g in range(A_GROUPS):
        cols = slice(g * A_GROUP_DIM, (g + 1) * A_GROUP_DIM)
        u = _dot(h, w_in_ref[:, cols])
        z = _dot(h, w_in_ref[:, 2 * A_WIDTH + g * A_GROUP_DIM:2 * A_WIDTH + (g + 1) * A_GROUP_DIM])
        ws = jnp.where(causal, ws_ref[g], 0.0).astype(BF16)
        bias = bs_ref[:, g:g + 1]
        mixed = jnp.concatenate(
            [_dot(ws, vn_ref[c * CHUNK:(c + 1) * CHUNK, cols]) + bias for c in range(n_chunks)],
            axis=0)
        y_ref[:, cols] = (u * mixed * _silu(z)).astype(BF16)
    o_ref[...] = x + _dot(y_ref[...], w_out_ref[...])


def _layer_a(x2, ng, w_in16, v_gain, w_s, b_s, w_out16, cast_jobs=()):
    tokens = x2.shape[0]
    tm = LAYER_TILE
    n_steps = tokens // tm
    cast_in, cast_out, cast_shapes = _cast_specs(cast_jobs, n_steps, lambda i: i)
    out, *casted = pl.pallas_call(
        _with_casts(_layer_a_kernel, 7, 1, len(cast_jobs)),
        grid=(n_steps,),
        in_specs=[
            pl.BlockSpec((tm, D_MODEL), lambda i: (i, 0)),
            _resident((1, D_MODEL), lambda i: (0, 0)),
            _resident((D_MODEL, 3 * A_WIDTH), lambda i: (0, 0)),
            _resident((1, A_WIDTH), lambda i: (0, 0)),
            _resident((A_GROUPS, CHUNK, CHUNK), lambda i: (0, 0, 0)),
            _resident((CHUNK, A_GROUPS), lambda i: (0, 0)),
            _resident((A_WIDTH, D_MODEL), lambda i: (0, 0)),
        ] + cast_in,
        out_specs=[pl.BlockSpec((tm, D_MODEL), lambda i: (i, 0))] + cast_out,
        out_shape=[jax.ShapeDtypeStruct((tokens, D_MODEL), F32)] + cast_shapes,
        scratch_shapes=[pltpu.VMEM((tm, A_WIDTH), BF16), pltpu.VMEM((tm, A_WIDTH), BF16)],
        compiler_params=pltpu.CompilerParams(
            dimension_semantics=("arbitrary",), vmem_limit_bytes=VMEM_LIMIT_BYTES),
        name="layer_a",
    )(x2, ng.reshape(1, D_MODEL), w_in16, v_gain.reshape(1, A_WIDTH), w_s, b_s.T, w_out16,
      *[job[0] for job in cast_jobs])
    return out, casted


def _layer_c_kernel(x_ref, ng_ref, w_in_ref, w_grp_ref, scale_ref, w_out_ref,
                    o_ref, ext_ref, sum_ref, y_ref):
    t = pl.program_id(1)
    tm = x_ref.shape[0]
    x = x_ref[...]
    h = _rms_norm(x, ng_ref[...]).astype(BF16)

    @pl.when(t == 0)
    def _():
        ext_ref[0:HALO, :] = jnp.zeros((HALO, C_WIDTH), F32)

    @pl.when(t > 0)
    def _():
        ext_ref[0:HALO, :] = ext_ref[tm:tm + HALO, :]

    ext_ref[HALO:HALO + tm, :] = _dot(h, w_in_ref[:, 0:C_WIDTH])
    pos = t * tm + lax.broadcasted_iota(jnp.int32, (tm, 1), 0)
    for g, window in enumerate(POOL_SIZES):
        cols = slice(g * C_GROUP, (g + 1) * C_GROUP)
        xc = ext_ref[HALO:HALO + tm, cols]
        span, lo = 1, 8
        while span < window:
            src = ext_ref if span == 1 else sum_ref
            src_cols = cols if span == 1 else slice(None)
            sum_ref[lo:HALO + tm, :] = (src[lo:HALO + tm, src_cols]
                                        + src[lo - span:HALO + tm - span, src_cols])
            span, lo = 2 * span, lo + 8
        assert lo <= HALO + 8
        cnt = jnp.minimum(pos + 1, window).astype(F32)
        diff = (sum_ref[HALO:HALO + tm, :] / cnt - xc).astype(BF16)
        mixed = _dot(diff, w_grp_ref[g]) * scale_ref[:, cols]
        z = _dot(h, w_in_ref[:, C_WIDTH + g * C_GROUP:C_WIDTH + (g + 1) * C_GROUP])
        y_ref[:, cols] = (mixed * _silu(z)).astype(BF16)
    o_ref[...] = x + _dot(y_ref[...], w_out_ref[...])


def _layer_c(x3, ng, w_in16, w_grp16, scale, w_out16):
    bsz, seq, _ = x3.shape
    tm = LAYER_TILE
    return pl.pallas_call(
        _layer_c_kernel,
        grid=(bsz, seq // tm),
        in_specs=[
            pl.BlockSpec((None, tm, D_MODEL), lambda b, t: (b, t, 0)),
            _resident((1, D_MODEL), lambda b, t: (0, 0)),
            _resident((D_MODEL, 2 * C_WIDTH), lambda b, t: (0, 0)),
            _resident((N_POOL, C_GROUP, C_GROUP), lambda b, t: (0, 0, 0)),
            _resident((1, C_WIDTH), lambda b, t: (0, 0)),
            _resident((C_WIDTH, D_MODEL), lambda b, t: (0, 0)),
        ],
        out_specs=pl.BlockSpec((None, tm, D_MODEL), lambda b, t: (b, t, 0)),
        out_shape=jax.ShapeDtypeStruct((bsz, seq, D_MODEL), F32),
        scratch_shapes=[pltpu.VMEM((tm + HALO, C_WIDTH), F32), pltpu.VMEM((tm + HALO, C_GROUP), F32),
                        pltpu.VMEM((tm, C_WIDTH), BF16)],
        compiler_params=pltpu.CompilerParams(
            dimension_semantics=("arbitrary", "arbitrary"), vmem_limit_bytes=VMEM_LIMIT_BYTES),
        name="layer_c",
    )(x3, ng.reshape(1, D_MODEL), w_in16, w_grp16, scale.reshape(1, C_WIDTH), w_out16)


def _attn_proj_kernel(x_ref, ng_ref, freq_ref, wq_ref, wkt_ref, wv_ref, qg_ref, kg_ref,
                      q_ref, kt_ref, v_ref, h_scr, tmp_scr, rot_scr, q_a, kt_a, q_b, kt_b,
                      *, dilation, n_steps, n_super):
    t = pl.program_id(0)
    rows = q_ref.shape[0]
    n_chunk = SUPER // rows
    per = SUPER // dilation
    n_cols = D_MODEL // LANES
    s = t - n_chunk
    cur = jnp.clip(s, 0, n_steps - 1)
    c = cur % n_chunk
    slot = (cur // n_chunk) % 2
    prev = jnp.clip(s - 1, 0, n_steps - 1)
    c_prev = prev % n_chunk
    n_prev = (prev // n_chunk) % n_super

    @pl.when(s == 0)
    def _():
        q_b[...] = jnp.zeros(q_b.shape, q_b.dtype)
        kt_b[...] = jnp.zeros(kt_b.shape, kt_b.dtype)

    coarse = dilation > MERGE_DIL
    quarter_rows = SUPER // MERGE_DIL

    def normalise_tile():
        tile = jnp.minimum(t, n_steps - 1)
        half = (tile // n_chunk) % 2
        hh = _rms_norm(x_ref[...], ng_ref[...])
        if not coarse:
            rs = pl.ds(pl.multiple_of((tile % n_chunk) * rows, rows), rows)
            for j in range(n_cols):
                h_scr[half, j, rs, :] = hh[:, j * LANES:(j + 1) * LANES]
            return
        sub = rows // MERGE_DIL
        base = pl.multiple_of((tile % n_chunk) * sub, sub)
        for j in range(n_cols):
            tmp_scr[j] = hh[:, j * LANES:(j + 1) * LANES]
        for j in range(n_cols):
            for b in range(MERGE_DIL):
                h_scr[half, j, pl.ds(b * quarter_rows + base, sub), :] = (
                    tmp_scr[j, pl.ds(b, sub, stride=MERGE_DIL), :])

    def gather(j):
        if dilation == 1:
            return h_scr[slot, j, pl.ds(pl.multiple_of(c * rows, rows), rows), :]
        run = min(per, rows)
        if per > rows:
            raise NotImplementedError("a residue class longer than the chunk needs a row offset")
        if not coarse:
            first = c * (rows // run)
            pieces = [h_scr[slot, j, pl.ds(first + rr, run, stride=dilation), :] for rr in range(rows // run)]
            return pieces[0] if len(pieces) == 1 else jnp.concatenate(pieces, axis=0)
        fine = dilation // MERGE_DIL
        if rows // run != MERGE_DIL:
            raise NotImplementedError("chunk / residue-run geometry not covered")
        pieces = [h_scr[slot, j, pl.ds(rr * quarter_rows + c, run, stride=fine), :]
                  for rr in range(MERGE_DIL)]
        return jnp.concatenate(pieces, axis=0)

    def project(q_raw, kt_raw):
        h = jnp.concatenate([gather(j) for j in range(n_cols)], axis=1).astype(BF16)
        q_raw[...] = _dot(h, wq_ref[...])
        kt_raw[...] = _dot_nt(wkt_ref[...], h)
        v_ref[...] = _dot(h, wv_ref[...]).astype(BF16)

    def position(p):
        return jnp.bitwise_and(p, per - 1) * dilation + lax.shift_right_logical(p, per.bit_length() - 1)

    @pl.when(t == 0)
    def _():
        ang = position(lax.broadcasted_iota(jnp.int32, (1, rows), 1)).astype(F32) * freq_ref[...]
        rot_scr[0:ROPE_HALF, :] = jnp.cos(ang)
        rot_scr[ROPE_HALF:ROPE_DIM, :] = jnp.sin(ang)

    def finish(q_raw, kt_raw):
        base = n_prev * SUPER + position(c_prev * rows)
        ang = base.astype(F32) * freq_ref[...]
        cos_b, sin_b = jnp.cos(ang), jnp.sin(ang)
        cos_o, sin_o = rot_scr[0:ROPE_HALF, :], rot_scr[ROPE_HALF:ROPE_DIM, :]
        cos_t = cos_b * cos_o - sin_b * sin_o
        sin_t = sin_b * cos_o + cos_b * sin_o

        for hd in range(B_HEADS):
            slab = kt_raw[hd * HEAD_DIM:(hd + 1) * HEAD_DIM, :]
            kn = slab * lax.rsqrt(jnp.mean(slab * slab, axis=0, keepdims=True) + EPS) * kg_ref[...]
            x1, x2 = kn[0:ROPE_HALF, :], kn[ROPE_HALF:ROPE_DIM, :]
            rotated = jnp.concatenate(
                [x1 * cos_t - x2 * sin_t, x2 * cos_t + x1 * sin_t, kn[ROPE_DIM:, :]], axis=0)
            kt_ref[hd * HEAD_DIM:(hd + 1) * HEAD_DIM, :] = rotated.astype(BF16)

        rest = HEAD_DIM - ROPE_DIM
        cos_tab = jnp.concatenate([cos_t, cos_t, jnp.ones((rest, rows), F32)], axis=0).T * Q_PRESCALE
        sin_tab = jnp.concatenate([-sin_t, sin_t, jnp.zeros((rest, rows), F32)], axis=0).T * Q_PRESCALE
        lane = lax.broadcasted_iota(jnp.int32, (rows, HEAD_DIM), 1)
        for hd in range(B_HEADS):
            cols = slice(hd * HEAD_DIM, (hd + 1) * HEAD_DIM)
            qn = _rms_norm(q_raw[:, cols], qg_ref[...])
            partner = jnp.where(lane < ROPE_HALF, pltpu.roll(qn, HEAD_DIM - ROPE_HALF, 1),
                                pltpu.roll(qn, ROPE_HALF, 1))
            q_ref[:, cols] = (qn * cos_tab + partner * sin_tab).astype(BF16)

    @pl.when(s < 0)
    def _():
        normalise_tile()

    @pl.when((s >= 0) & (s < n_steps) & (s % 2 == 0))
    def _():
        project(q_a, kt_a)
        finish(q_b, kt_b)
        normalise_tile()

    @pl.when((s >= 0) & (s < n_steps) & (s % 2 == 1))
    def _():
        project(q_b, kt_b)
        finish(q_a, kt_a)
        normalise_tile()

    @pl.when(s == n_steps)
    def _():
        if (n_steps - 1) % 2 == 0:
            finish(q_a, kt_a)
        else:
            finish(q_b, kt_b)


def _attn_proj(x3, ng, inv_freq, w_in16, w_kt16, q_gain, k_gain, group, dilation, cast_jobs=()):
    bsz, seq, _ = x3.shape
    rows = TOKEN_TILE
    n_super, n_chunk = seq // SUPER, SUPER // rows
    per_batch = n_super * n_chunk
    n_steps = bsz * per_batch
    tile = lambda t: jnp.minimum(t, n_steps - 1)
    cur = lambda t: jnp.clip(t - n_chunk, 0, n_steps - 1)
    prev = lambda t: jnp.clip(t - n_chunk - 1, 0, n_steps - 1)
    raw = [pltpu.VMEM((rows, B_WIDTH), F32), pltpu.VMEM((B_WIDTH, rows), F32)]
    cast_in, cast_out, cast_shapes = _cast_specs(cast_jobs, n_steps, lambda s: s)
    body = functools.partial(_attn_proj_kernel, dilation=dilation, n_steps=n_steps, n_super=n_super)
    q, kt, v, *casted = pl.pallas_call(
        _with_casts(body, 8, 3, len(cast_jobs)),
        grid=(n_chunk + n_steps + 1,),
        in_specs=[
            pl.BlockSpec((None, rows, D_MODEL), lambda t: (tile(t) // per_batch, tile(t) % per_batch, 0)),
            _resident((1, D_MODEL), lambda s: (0, 0)),
            _resident((ROPE_HALF, 1), lambda s: (0, 0)),
            _resident((D_MODEL, B_WIDTH), lambda s: (0, group)),
            _resident((B_WIDTH, D_MODEL), lambda s: (group, 0)),
            _resident((D_MODEL, B_WIDTH), lambda s: (0, 2 * N_B_GROUPS + group)),
            _resident((1, HEAD_DIM), lambda s: (0, 0)),
            _resident((HEAD_DIM, 1), lambda s: (0, 0)),
        ] + cast_in,
        out_specs=[
            pl.BlockSpec((None, rows, B_WIDTH), lambda s: (prev(s) // per_batch, prev(s) % per_batch, 0)),
            pl.BlockSpec((None, B_WIDTH, rows), lambda s: (prev(s) // per_batch, 0, prev(s) % per_batch)),
            pl.BlockSpec((None, rows, B_WIDTH), lambda s: (cur(s) // per_batch, cur(s) % per_batch, 0)),
        ] + cast_out,
        out_shape=[
            jax.ShapeDtypeStruct((bsz, seq, B_WIDTH), BF16),
            jax.ShapeDtypeStruct((bsz, B_WIDTH, seq), BF16),
            jax.ShapeDtypeStruct((bsz, seq, B_WIDTH), BF16),
        ] + cast_shapes,
        scratch_shapes=[pltpu.VMEM((2, D_MODEL // LANES, SUPER, LANES), F32),
                        pltpu.VMEM((D_MODEL // LANES, rows, LANES), F32),
                        pltpu.VMEM((ROPE_DIM, rows), F32)] + raw + raw,
        compiler_params=pltpu.CompilerParams(
            dimension_semantics=("arbitrary",), vmem_limit_bytes=VMEM_LIMIT_BYTES),
        name=f"attn_proj_d{dilation}",
    )(x3, ng.reshape(1, D_MODEL), inv_freq, w_in16, w_kt16, w_in16,
      q_gain.reshape(1, HEAD_DIM), k_gain.reshape(HEAD_DIM, 1), *[job[0] for job in cast_jobs])
    return (q, kt, v), casted


def _attn_kernel(*refs):
    qkv_refs = refs[:3 * N_B_GROUPS]
    y_ref = refs[3 * N_B_GROUPS]
    scratch = refs[3 * N_B_GROUPS + 1:]
    kt_scrs, v_scrs = scratch[0:N_B_GROUPS], scratch[N_B_GROUPS:2 * N_B_GROUPS]
    o_scr, lse_scr, bias_scr = scratch[2 * N_B_GROUPS:]
    n = pl.program_id(2)

    qi = lax.broadcasted_iota(jnp.int32, (SPAN, 2 * SPAN), 0)
    ki = lax.broadcasted_iota(jnp.int32, (SPAN, 2 * SPAN), 1)
    band = (ki >= qi) & (ki <= qi + SPAN)
    bias_scr[0] = jnp.where(band, 0.0, MASK_VALUE)
    bias_scr[1] = jnp.where(band & (ki >= jnp.where(n > 0, 0, SPAN)), 0.0, MASK_VALUE)

    for g, (_, d) in enumerate(B_PATTERNS):
        q_ref, kt_ref, v_ref = qkv_refs[3 * g:3 * g + 3]
        kt_scr, v_scr = kt_scrs[g], v_scrs[g]
        per = SUPER // d
        seg = SPAN + per

        @pl.when(n == 0)
        def _():
            v_scr[...] = jnp.ones(v_scr.shape, BF16)
            for r in range(d):
                kt_scr[:, r * seg:r * seg + SPAN] = jnp.zeros((kt_scr.shape[0], SPAN), BF16)
                for hh in range(HEADS_PER_STEP):
                    v_scr[r * seg:r * seg + SPAN, 2 * hh * HEAD_DIM:(2 * hh + 1) * HEAD_DIM] = (
                        jnp.zeros((SPAN, HEAD_DIM), BF16))

        @pl.when(n > 0)
        def _():
            for r in range(d):
                kt_scr[:, r * seg:r * seg + SPAN] = kt_scr[:, r * seg + per:r * seg + per + SPAN]
                v_scr[r * seg:r * seg + SPAN, :] = v_scr[r * seg + per:r * seg + per + SPAN, :]

        for r in range(d):
            kt_scr[:, r * seg + SPAN:(r + 1) * seg] = kt_ref[:, r * per:(r + 1) * per]
            for hh in range(HEADS_PER_STEP):
                v_scr[r * seg + SPAN:(r + 1) * seg, 2 * hh * HEAD_DIM:(2 * hh + 1) * HEAD_DIM] = (
                    v_ref[r * per:(r + 1) * per, hh * HEAD_DIM:(hh + 1) * HEAD_DIM])

        for r in range(d):
            for j in range(per // SPAN):
                q_rows = slice(r * per + j * SPAN, r * per + (j + 1) * SPAN)
                keys = slice(r * seg + j * SPAN, r * seg + (j + 2) * SPAN)
                if d < MERGE_DIL:
                    assert d == 1
                    nat = slice(j * SPAN, (j + 1) * SPAN)
                else:
                    fine = d // MERGE_DIL
                    start = (r % MERGE_DIL) * (SUPER // MERGE_DIL) + fine * j * SPAN + r // MERGE_DIL
                    nat = pl.ds(start, SPAN, stride=fine) if fine > 1 else slice(start, start + SPAN)
                for hh in range(HEADS_PER_STEP):
                    cols = slice(hh * HEAD_DIM, (hh + 1) * HEAD_DIM)
                    s = _dot(q_ref[q_rows, cols], kt_scr[cols, keys]) + bias_scr[1 if j == 0 else 0]
                    m = jnp.max(s, axis=-1, keepdims=True)
                    p = jnp.exp2(s - m)
                    pv = _dot(p.astype(BF16), v_scr[keys, 2 * hh * HEAD_DIM:(2 * hh + 2) * HEAD_DIM])
                    denom = pv[:, HEAD_DIM:]
                    o_scr[g, hh, nat, :] = pv[:, :HEAD_DIM] * (1.0 / denom)
                    lse_scr[g, hh, nat, :] = m + jnp.log2(denom)

    rows = MERGE_ROWS
    for hh in range(HEADS_PER_STEP):
        for b in range(MERGE_DIL):
            for u0 in range(0, SUPER // MERGE_DIL, rows):
                natural = pl.ds(b + MERGE_DIL * u0, rows, stride=MERGE_DIL)
                ordered = slice(b * (SUPER // MERGE_DIL) + u0, b * (SUPER // MERGE_DIL) + u0 + rows)
                idx = [natural if d < MERGE_DIL else ordered for _, d in B_PATTERNS]
                lses = [lse_scr[g, hh, idx[g], :] for g in range(N_B_GROUPS)]
                m = jnp.maximum(jnp.maximum(lses[0], lses[1]), lses[2])
                es = [jnp.exp2(l - m) for l in lses]
                mixed = (es[0] * o_scr[0, hh, idx[0], :] + es[1] * o_scr[1, hh, idx[1], :]
                         + es[2] * o_scr[2, hh, idx[2], :])
                y_ref[hh, natural, :] = mixed * (1.0 / (es[0] + es[1] + es[2]))


def _attn(qkvs, bsz, seq, cast_jobs=()):
    width = HEADS_PER_STEP * HEAD_DIM
    n_super = seq // SUPER
    in_specs, operands, kt_scr, v_scr = [], [], [], []
    for (q, kt, v), (_, d) in zip(qkvs, B_PATTERNS):
        in_specs += [
            pl.BlockSpec((None, SUPER, width), lambda b, hp, n: (b, n, hp)),
            pl.BlockSpec((None, width, SUPER), lambda b, hp, n: (b, hp, n)),
            pl.BlockSpec((None, SUPER, width), lambda b, hp, n: (b, n, hp)),
        ]
        operands += [q, kt, v]
        kt_scr.append(pltpu.VMEM((width, SUPER + d * SPAN), BF16))
        v_scr.append(pltpu.VMEM((SUPER + d * SPAN, 2 * width), BF16))
    n_pairs = B_HEADS // HEADS_PER_STEP
    cast_in, cast_out, cast_shapes = _cast_specs(
        cast_jobs, bsz * n_pairs * n_super, lambda b, hp, n: (b * n_pairs + hp) * n_super + n)
    out, *casted = pl.pallas_call(
        _with_casts(_attn_kernel, len(operands), 1, len(cast_jobs)),
        grid=(bsz, n_pairs, n_super),
        in_specs=in_specs + cast_in,
        out_specs=[pl.BlockSpec((None, HEADS_PER_STEP, SUPER, HEAD_DIM),
                                lambda b, hp, n: (b, hp, n, 0))] + cast_out,
        out_shape=[jax.ShapeDtypeStruct((bsz, B_HEADS, seq, HEAD_DIM), F32)] + cast_shapes,
        scratch_shapes=kt_scr + v_scr + [
            pltpu.VMEM((N_B_GROUPS, HEADS_PER_STEP, SUPER, HEAD_DIM), F32),
            pltpu.VMEM((N_B_GROUPS, HEADS_PER_STEP, SUPER, HEAD_DIM), F32),
            pltpu.VMEM((2, SPAN, 2 * SPAN), F32),
        ],
        compiler_params=pltpu.CompilerParams(
            dimension_semantics=("arbitrary", "arbitrary", "arbitrary"),
            vmem_limit_bytes=VMEM_LIMIT_BYTES),
        name="attn",
    )(*operands, *[job[0] for job in cast_jobs])
    return out, casted


def _attn_out_kernel(x_ref, ng_ref, wz_ref, o_ref, w_out_ref, out_ref):
    x = x_ref[...]
    h = _rms_norm(x, ng_ref[...]).astype(BF16)
    z = _dot(h, wz_ref[...])
    o = jnp.concatenate([o_ref[hd] for hd in range(B_HEADS)], axis=1)
    y = (o * _silu(z)).astype(BF16)
    out_ref[...] = x + _dot(y, w_out_ref[...])


def _attn_out(x3, ng, w_in16, o, w_out16):
    bsz, seq, _ = x3.shape
    tm = LAYER_TILE
    tile = pl.BlockSpec((None, tm, D_MODEL), lambda b, t: (b, t, 0))
    return pl.pallas_call(
        _attn_out_kernel,
        grid=(bsz, seq // tm),
        in_specs=[
            tile,
            _resident((1, D_MODEL), lambda b, t: (0, 0)),
            _resident((D_MODEL, B_WIDTH), lambda b, t: (0, 3 * N_B_GROUPS)),
            pl.BlockSpec((None, B_HEADS, tm, HEAD_DIM), lambda b, t: (b, 0, t, 0)),
            _resident((B_WIDTH, D_MODEL), lambda b, t: (0, 0)),
        ],
        out_specs=tile,
        out_shape=jax.ShapeDtypeStruct((bsz, seq, D_MODEL), F32),
        compiler_params=pltpu.CompilerParams(
            dimension_semantics=("arbitrary", "arbitrary"), vmem_limit_bytes=VMEM_LIMIT_BYTES),
        name="attn_out",
    )(x3, ng.reshape(1, D_MODEL), w_in16, o, w_out16)


def _layer_b(x3, ng, w_in16, w_kt16, q_gain, k_gain, w_out16, proj_cast_jobs, attn_cast_jobs):
    bsz, seq, _ = x3.shape
    inv_freq = jnp.power(jnp.float32(ROPE_THETA), -jnp.arange(ROPE_HALF, dtype=F32) / ROPE_HALF)
    inv_freq = inv_freq.reshape(ROPE_HALF, 1)
    qkvs, casted = [], []
    for g, (window, dilation) in enumerate(B_PATTERNS):
        assert window // dilation == SPAN
        qkv, done = _attn_proj(x3, ng, inv_freq, w_in16, w_kt16, q_gain[g], k_gain[g], g, dilation,
                               proj_cast_jobs[g])
        qkvs.append(qkv)
        casted += done
    o, done = _attn(qkvs, bsz, seq, attn_cast_jobs)
    return _attn_out(x3, ng, w_in16, o, w_out16), casted + done


def kernel(x, norm_gain, a_w_in, a_v_gain, a_w_s, a_b_s, a_w_out, b_w_in, b_q_gain, b_k_gain,
           b_w_out, c_w_in, c_w_grp, c_scale, c_w_out):
    bsz, seq, d_model = x.shape
    assert d_model == D_MODEL and seq % SUPER == 0 and SUPER % TOKEN_TILE == 0
    assert norm_gain.shape[0] == 4 and a_w_in.shape[0] == 2
    c_w_grp2 = c_w_grp.reshape(c_w_grp.shape[0], N_POOL * C_GROUP, C_GROUP)
    k_cols = (N_B_GROUPS * B_WIDTH, N_B_GROUPS * B_WIDTH)

    x2, (b_w_in16, b_w_out16, b_w_kt16) = _layer_a(
        x.reshape(bsz * seq, D_MODEL), norm_gain[0], a_w_in[0].astype(BF16), a_v_gain[0], a_w_s[0],
        a_b_s[0], a_w_out[0].astype(BF16),
        cast_jobs=[(b_w_in, 0, None), (b_w_out, 0, None), (b_w_in, 0, k_cols)])
    x, (c_w_in16, c_w_grp16, c_w_out16, a_w_in16, a_w_out16) = _layer_b(
        x2.reshape(bsz, seq, D_MODEL), norm_gain[1], b_w_in16, b_w_kt16, b_q_gain[0], b_k_gain[0],
        b_w_out16,
        proj_cast_jobs=[[(c_w_in, 0, None)], [(c_w_grp2, 0, None)], [(c_w_out, 0, None)]],
        attn_cast_jobs=[(a_w_in, 1, None), (a_w_out, 1, None)])
    x = _layer_c(x, norm_gain[2], c_w_in16, c_w_grp16.reshape(N_POOL, C_GROUP, C_GROUP), c_scale[0],
                 c_w_out16)
    x2, _ = _layer_a(x.reshape(bsz * seq, D_MODEL), norm_gain[3], a_w_in16, a_v_gain[1], a_w_s[1],
                     a_b_s[1], a_w_out16)
    return x2.reshape(bsz, seq, D_MODEL)
```

```python
import functools
import math

import jax
import jax.numpy as jnp
import numpy as np
from jax import lax
from jax.experimental import pallas as pl
from jax.experimental.pallas import tpu as pltpu

D_MODEL = 1024
EPS = 1e-6
CHUNK = 128
A_WIDTH = 2 * D_MODEL
A_GROUPS = 8
A_GROUP_DIM = A_WIDTH // A_GROUPS
HEAD_DIM = 128
B_HEADS = D_MODEL // HEAD_DIM
B_PATTERNS = ((128, 1), (512, 4), (2048, 16))
N_B_GROUPS = len(B_PATTERNS)
B_WIDTH = B_HEADS * HEAD_DIM
ROPE_DIM = HEAD_DIM // 4
ROPE_HALF = ROPE_DIM // 2
ROPE_THETA = 500000.0
Q_PRESCALE = math.log2(math.e) / math.sqrt(HEAD_DIM)
SPAN = 128
SUPER = SPAN * max(d for _, d in B_PATTERNS)
HEADS_PER_STEP = 2
MERGE_DIL = 4
POOL_SIZES = (2, 4, 8, 16)
N_POOL = len(POOL_SIZES)
C_WIDTH = 2 * D_MODEL
C_GROUP = C_WIDTH // N_POOL
HALO = 32

LANES = 128
BF16_ROWS = 16
MERGE_ROWS = 256
VMEM_LIMIT_BYTES = 60 * 1024 * 1024

TOKEN_TILE = 512
LAYER_TILE = 1024
MASK_VALUE = -1e30

F32 = jnp.float32
BF16 = jnp.bfloat16


def _resident(shape, index_map):
    return pl.BlockSpec(shape, index_map, pipeline_mode=pl.Buffered(1))


def _rms_norm(x, gain):
    return x * lax.rsqrt(jnp.mean(x * x, axis=-1, keepdims=True) + EPS) * gain


def _dot(a, b):
    return jnp.dot(a, b, preferred_element_type=F32)


def _dot_nt(a, b):
    return lax.dot_general(a, b, (((1,), (1,)), ((), ())), preferred_element_type=F32)


def _silu(z):
    return z * jax.nn.sigmoid(z)


def _cast_specs(jobs, n_steps, flat_step):
    in_specs, out_specs, out_shapes = [], [], []
    for arr, layer, col_range in jobs:
        _, rows, cols = arr.shape
        if col_range is None:
            n_chunks = n_steps
            chunk = rows // n_chunks
            assert chunk * n_chunks == rows and chunk % BF16_ROWS == 0, (arr.shape, n_steps)
            last = n_chunks - 1
            in_specs.append(pl.BlockSpec(
                (None, chunk, cols),
                lambda *ids, layer=layer, last=last: (layer, jnp.minimum(flat_step(*ids), last), 0)))
            out_specs.append(pl.BlockSpec(
                (chunk, cols), lambda *ids, last=last: (jnp.minimum(flat_step(*ids), last), 0)))
            out_shapes.append(jax.ShapeDtypeStruct((rows, cols), BF16))
        else:
            start, width = col_range
            n_chunks = rows // LANES
            assert n_chunks <= n_steps and start % width == 0, (arr.shape, n_steps, col_range)
            last = n_chunks - 1
            in_specs.append(pl.BlockSpec(
                (None, LANES, width),
                lambda *ids, layer=layer, last=last, blk=start // width: (
                    layer, jnp.minimum(flat_step(*ids), last), blk)))
            out_specs.append(pl.BlockSpec(
                (width, LANES), lambda *ids, last=last: (0, jnp.minimum(flat_step(*ids), last))))
            out_shapes.append(jax.ShapeDtypeStruct((width, rows), BF16))
    return in_specs, out_specs, out_shapes


def _with_casts(body, n_in, n_out, n_jobs):
    if n_jobs == 0:
        return body

    def wrapped(*refs):
        ins, cast_in = refs[:n_in], refs[n_in:n_in + n_jobs]
        outs = refs[n_in + n_jobs:n_in + n_jobs + n_out]
        cast_out = refs[n_in + n_jobs + n_out:n_in + 2 * n_jobs + n_out]
        body(*ins, *outs, *refs[n_in + 2 * n_jobs + n_out:])
        for src, dst in zip(cast_in, cast_out):
            transposed = src.shape != dst.shape
            dst[...] = (src[...].T if transposed else src[...]).astype(BF16)

    return wrapped


def _layer_a_kernel(x_ref, ng_ref, w_in_ref, vg_ref, ws_ref, bs_ref, w_out_ref,
                    o_ref, vn_ref, y_ref):
    x = x_ref[...]
    h = _rms_norm(x, ng_ref[...]).astype(BF16)
    v = _dot(h, w_in_ref[:, A_WIDTH:2 * A_WIDTH])
    vn_ref[...] = _rms_norm(v, vg_ref[...]).astype(BF16)
    row = lax.broadcasted_iota(jnp.int32, (CHUNK, CHUNK), 0)
    col = lax.broadcasted_iota(jnp.int32, (CHUNK, CHUNK), 1)
    causal = col <= row
    n_chunks = x.shape[0] // CHUNK
    for g in range(A_GROUPS):
        cols = slice(g * A_GROUP_DIM, (g + 1) * A_GROUP_DIM)
        u = _dot(h, w_in_ref[:, cols])
        z = _dot(h, w_in_ref[:, 2 * A_WIDTH + g * A_GROUP_DIM:2 * A_WIDTH + (g + 1) * A_GROUP_DIM])
        ws = jnp.where(causal, ws_ref[g], 0.0).astype(BF16)
        bias = bs_ref[:, g:g + 1]
        mixed = jnp.concatenate(
            [_dot(ws, vn_ref[c * CHUNK:(c + 1) * CHUNK, cols]) + bias for c in range(n_chunks)],
            axis=0)
        y_ref[:, cols] = (u * mixed * _silu(z)).astype(BF16)
    o_ref[...] = x + _dot(y_ref[...], w_out_ref[...])


def _layer_a(x2, ng, w_in16, v_gain, w_s, b_s, w_out16, cast_jobs=()):
    tokens = x2.shape[0]
    tm = LAYER_TILE
    n_steps = tokens // tm
    cast_in, cast_out, cast_shapes = _cast_specs(cast_jobs, n_steps, lambda i: i)
    out, *casted = pl.pallas_call(
        _with_casts(_layer_a_kernel, 7, 1, len(cast_jobs)),
        grid=(n_steps,),
        in_specs=[
            pl.BlockSpec((tm, D_MODEL), lambda i: (i, 0)),
            _resident((1, D_MODEL), lambda i: (0, 0)),
            _resident((D_MODEL, 3 * A_WIDTH), lambda i: (0, 0)),
            _resident((1, A_WIDTH), lambda i: (0, 0)),
            _resident((A_GROUPS, CHUNK, CHUNK), lambda i: (0, 0, 0)),
            _resident((CHUNK, A_GROUPS), lambda i: (0, 0)),
            _resident((A_WIDTH, D_MODEL), lambda i: (0, 0)),
        ] + cast_in,
        out_specs=[pl.BlockSpec((tm, D_MODEL), lambda i: (i, 0))] + cast_out,
        out_shape=[jax.ShapeDtypeStruct((tokens, D_MODEL), F32)] + cast_shapes,
        scratch_shapes=[pltpu.VMEM((tm, A_WIDTH), BF16), pltpu.VMEM((tm, A_WIDTH), BF16)],
        compiler_params=pltpu.CompilerParams(
            dimension_semantics=("arbitrary",), vmem_limit_bytes=VMEM_LIMIT_BYTES),
        name="layer_a",
    )(x2, ng.reshape(1, D_MODEL), w_in16, v_gain.reshape(1, A_WIDTH), w_s, b_s.T, w_out16,
      *[job[0] for job in cast_jobs])
    return out, casted


def _layer_c_kernel(x_ref, ng_ref, w_in_ref, w_grp_ref, scale_ref, w_out_ref,
                    o_ref, ext_ref, sum_ref, y_ref):
    t = pl.program_id(1)
    tm = x_ref.shape[0]
    x = x_ref[...]
    h = _rms_norm(x, ng_ref[...]).astype(BF16)

    @pl.when(t == 0)
    def _():
        ext_ref[0:HALO, :] = jnp.zeros((HALO, C_WIDTH), F32)

    @pl.when(t > 0)
    def _():
        ext_ref[0:HALO, :] = ext_ref[tm:tm + HALO, :]

    ext_ref[HALO:HALO + tm, :] = _dot(h, w_in_ref[:, 0:C_WIDTH])
    pos = t * tm + lax.broadcasted_iota(jnp.int32, (tm, 1), 0)
    for g, window in enumerate(POOL_SIZES):
        cols = slice(g * C_GROUP, (g + 1) * C_GROUP)
        xc = ext_ref[HALO:HALO + tm, cols]
        span, lo = 1, 8
        while span < window:
            src = ext_ref if span == 1 else sum_ref
            src_cols = cols if span == 1 else slice(None)
            sum_ref[lo:HALO + tm, :] = (src[lo:HALO + tm, src_cols]
                                        + src[lo - span:HALO + tm - span, src_cols])
            span, lo = 2 * span, lo + 8
        assert lo <= HALO + 8
        cnt = jnp.minimum(pos + 1, window).astype(F32)
        diff = (sum_ref[HALO:HALO + tm, :] / cnt - xc).astype(BF16)
        mixed = _dot(diff, w_grp_ref[g]) * scale_ref[:, cols]
        z = _dot(h, w_in_ref[:, C_WIDTH + g * C_GROUP:C_WIDTH + (g + 1) * C_GROUP])
        y_ref[:, cols] = (mixed * _silu(z)).astype(BF16)
    o_ref[...] = x + _dot(y_ref[...], w_out_ref[...])


def _layer_c(x3, ng, w_in16, w_grp16, scale, w_out16):
    bsz, seq, _ = x3.shape
    tm = LAYER_TILE
    return pl.pallas_call(
        _layer_c_kernel,
        grid=(bsz, seq // tm),
        in_specs=[
            pl.BlockSpec((None, tm, D_MODEL), lambda b, t: (b, t, 0)),
            _resident((1, D_MODEL), lambda b, t: (0, 0)),
            _resident((D_MODEL, 2 * C_WIDTH), lambda b, t: (0, 0)),
            _resident((N_POOL, C_GROUP, C_GROUP), lambda b, t: (0, 0, 0)),
            _resident((1, C_WIDTH), lambda b, t: (0, 0)),
            _resident((C_WIDTH, D_MODEL), lambda b, t: (0, 0)),
        ],
        out_specs=pl.BlockSpec((None, tm, D_MODEL), lambda b, t: (b, t, 0)),
        out_shape=jax.ShapeDtypeStruct((bsz, seq, D_MODEL), F32),
        scratch_shapes=[pltpu.VMEM((tm + HALO, C_WIDTH), F32), pltpu.VMEM((tm + HALO, C_GROUP), F32),
                        pltpu.VMEM((tm, C_WIDTH), BF16)],
        compiler_params=pltpu.CompilerParams(
            dimension_semantics=("arbitrary", "arbitrary"), vmem_limit_bytes=VMEM_LIMIT_BYTES),
        name="layer_c",
    )(x3, ng.reshape(1, D_MODEL), w_in16, w_grp16, scale.reshape(1, C_WIDTH), w_out16)


def _attn_proj_kernel(x_ref, ng_ref, freq_ref, wq_ref, wkt_ref, wv_ref, qg_ref, kg_ref,
                      q_ref, kt_ref, v_ref, h_scr, tmp_scr, rot_scr, q_a, kt_a, q_b, kt_b,
                      *, dilation, n_steps, n_super):
    t = pl.program_id(0)
    rows = q_ref.shape[0]
    n_chunk = SUPER // rows
    per = SUPER // dilation
    n_cols = D_MODEL // LANES
    s = t - n_chunk
    cur = jnp.clip(s, 0, n_steps - 1)
    c = cur % n_chunk
    slot = (cur // n_chunk) % 2
    prev = jnp.clip(s - 1, 0, n_steps - 1)
    c_prev = prev % n_chunk
    n_prev = (prev // n_chunk) % n_super

    @pl.when(s == 0)
    def _():
        q_b[...] = jnp.zeros(q_b.shape, q_b.dtype)
        kt_b[...] = jnp.zeros(kt_b.shape, kt_b.dtype)

    coarse = dilation > MERGE_DIL
    quarter_rows = SUPER // MERGE_DIL

    def normalise_tile():
        tile = jnp.minimum(t, n_steps - 1)
        half = (tile // n_chunk) % 2
        hh = _rms_norm(x_ref[...], ng_ref[...])
        if not coarse:
            rs = pl.ds(pl.multiple_of((tile % n_chunk) * rows, rows), rows)
            for j in range(n_cols):
                h_scr[half, j, rs, :] = hh[:, j * LANES:(j + 1) * LANES]
            return
        sub = rows // MERGE_DIL
        base = pl.multiple_of((tile % n_chunk) * sub, sub)
        for j in range(n_cols):
            tmp_scr[j] = hh[:, j * LANES:(j + 1) * LANES]
        for j in range(n_cols):
            for b in range(MERGE_DIL):
                h_scr[half, j, pl.ds(b * quarter_rows + base, sub), :] = (
                    tmp_scr[j, pl.ds(b, sub, stride=MERGE_DIL), :])

    def gather(j):
        if dilation == 1:
            return h_scr[slot, j, pl.ds(pl.multiple_of(c * rows, rows), rows), :]
        run = min(per, rows)
        if per > rows:
            raise NotImplementedError("a residue class longer than the chunk needs a row offset")
        if not coarse:
            first = c * (rows // run)
            pieces = [h_scr[slot, j, pl.ds(first + rr, run, stride=dilation), :] for rr in range(rows // run)]
            return pieces[0] if len(pieces) == 1 else jnp.concatenate(pieces, axis=0)
        fine = dilation // MERGE_DIL
        if rows // run != MERGE_DIL:
            raise NotImplementedError("chunk / residue-run geometry not covered")
        pieces = [h_scr[slot, j, pl.ds(rr * quarter_rows + c, run, stride=fine), :]
                  for rr in range(MERGE_DIL)]
        return jnp.concatenate(pieces, axis=0)

    def project(q_raw, kt_raw):
        h = jnp.concatenate([gather(j) for j in range(n_cols)], axis=1).astype(BF16)
        q_raw[...] = _dot(h, wq_ref[...])
        kt_raw[...] = _dot_nt(wkt_ref[...], h)
        v_ref[...] = _dot(h, wv_ref[...]).astype(BF16)

    def position(p):
        return jnp.bitwise_and(p, per - 1) * dilation + lax.shift_right_logical(p, per.bit_length() - 1)

    @pl.when(t == 0)
    def _():
        ang = position(lax.broadcasted_iota(jnp.int32, (1, rows), 1)).astype(F32) * freq_ref[...]
        rot_scr[0:ROPE_HALF, :] = jnp.cos(ang)
        rot_scr[ROPE_HALF:ROPE_DIM, :] = jnp.sin(ang)

    def finish(q_raw, kt_raw):
        base = n_prev * SUPER + position(c_prev * rows)
        ang = base.astype(F32) * freq_ref[...]
        cos_b, sin_b = jnp.cos(ang), jnp.sin(ang)
        cos_o, sin_o = rot_scr[0:ROPE_HALF, :], rot_scr[ROPE_HALF:ROPE_DIM, :]
        cos_t = cos_b * cos_o - sin_b * sin_o
        sin_t = sin_b * cos_o + cos_b * sin_o

        for hd in range(B_HEADS):
            slab = kt_raw[hd * HEAD_DIM:(hd + 1) * HEAD_DIM, :]
            kn = slab * lax.rsqrt(jnp.mean(slab * slab, axis=0, keepdims=True) + EPS) * kg_ref[...]
            x1, x2 = kn[0:ROPE_HALF, :], kn[ROPE_HALF:ROPE_DIM, :]
            rotated = jnp.concatenate(
                [x1 * cos_t - x2 * sin_t, x2 * cos_t + x1 * sin_t, kn[ROPE_DIM:, :]], axis=0)
            kt_ref[hd * HEAD_DIM:(hd + 1) * HEAD_DIM, :] = rotated.astype(BF16)

        rest = HEAD_DIM - ROPE_DIM
        cos_tab = jnp.concatenate([cos_t, cos_t, jnp.ones((rest, rows), F32)], axis=0).T * Q_PRESCALE
        sin_tab = jnp.concatenate([-sin_t, sin_t, jnp.zeros((rest, rows), F32)], axis=0).T * Q_PRESCALE
        lane = lax.broadcasted_iota(jnp.int32, (rows, HEAD_DIM), 1)
        for hd in range(B_HEADS):
            cols = slice(hd * HEAD_DIM, (hd + 1) * HEAD_DIM)
            qn = _rms_norm(q_raw[:, cols], qg_ref[...])
            partner = jnp.where(lane < ROPE_HALF, pltpu.roll(qn, HEAD_DIM - ROPE_HALF, 1),
                                pltpu.roll(qn, ROPE_HALF, 1))
            q_ref[:, cols] = (qn * cos_tab + partner * sin_tab).astype(BF16)

    @pl.when(s < 0)
    def _():
        normalise_tile()

    @pl.when((s >= 0) & (s < n_steps) & (s % 2 == 0))
    def _():
        project(q_a, kt_a)
        finish(q_b, kt_b)
        normalise_tile()

    @pl.when((s >= 0) & (s < n_steps) & (s % 2 == 1))
    def _():
        project(q_b, kt_b)
        finish(q_a, kt_a)
        normalise_tile()

    @pl.when(s == n_steps)
    def _():
        if (n_steps - 1) % 2 == 0:
            finish(q_a, kt_a)
        else:
            finish(q_b, kt_b)


def _attn_proj(x3, ng, inv_freq, w_in16, w_kt16, q_gain, k_gain, group, dilation, cast_jobs=()):
    bsz, seq, _ = x3.shape
    rows = TOKEN_TILE
    n_super, n_chunk = seq // SUPER, SUPER // rows
    per_batch = n_super * n_chunk
    n_steps = bsz * per_batch
    tile = lambda t: jnp.minimum(t, n_steps - 1)
    cur = lambda t: jnp.clip(t - n_chunk, 0, n_steps - 1)
    prev = lambda t: jnp.clip(t - n_chunk - 1, 0, n_steps - 1)
    raw = [pltpu.VMEM((rows, B_WIDTH), F32), pltpu.VMEM((B_WIDTH, rows), F32)]
    cast_in, cast_out, cast_shapes = _cast_specs(cast_jobs, n_steps, lambda s: s)
    body = functools.partial(_attn_proj_kernel, dilation=dilation, n_steps=n_steps, n_super=n_super)
    q, kt, v, *casted = pl.pallas_call(
        _with_casts(body, 8, 3, len(cast_jobs)),
        grid=(n_chunk + n_steps + 1,),
        in_specs=[
            pl.BlockSpec((None, rows, D_MODEL), lambda t: (tile(t) // per_batch, tile(t) % per_batch, 0)),
            _resident((1, D_MODEL), lambda s: (0, 0)),
            _resident((ROPE_HALF, 1), lambda s: (0, 0)),
            _resident((D_MODEL, B_WIDTH), lambda s: (0, group)),
            _resident((B_WIDTH, D_MODEL), lambda s: (group, 0)),
            _resident((D_MODEL, B_WIDTH), lambda s: (0, 2 * N_B_GROUPS + group)),
            _resident((1, HEAD_DIM), lambda s: (0, 0)),
            _resident((HEAD_DIM, 1), lambda s: (0, 0)),
        ] + cast_in,
        out_specs=[
            pl.BlockSpec((None, rows, B_WIDTH), lambda s: (prev(s) // per_batch, prev(s) % per_batch, 0)),
            pl.BlockSpec((None, B_WIDTH, rows), lambda s: (prev(s) // per_batch, 0, prev(s) % per_batch)),
            pl.BlockSpec((None, rows, B_WIDTH), lambda s: (cur(s) // per_batch, cur(s) % per_batch, 0)),
        ] + cast_out,
        out_shape=[
            jax.ShapeDtypeStruct((bsz, seq, B_WIDTH), BF16),
            jax.ShapeDtypeStruct((bsz, B_WIDTH, seq), BF16),
            jax.ShapeDtypeStruct((bsz, seq, B_WIDTH), BF16),
        ] + cast_shapes,
        scratch_shapes=[pltpu.VMEM((2, D_MODEL // LANES, SUPER, LANES), F32),
                        pltpu.VMEM((D_MODEL // LANES, rows, LANES), F32),
                        pltpu.VMEM((ROPE_DIM, rows), F32)] + raw + raw,
        compiler_params=pltpu.CompilerParams(
            dimension_semantics=("arbitrary",), vmem_limit_bytes=VMEM_LIMIT_BYTES),
        name=f"attn_proj_d{dilation}",
    )(x3, ng.reshape(1, D_MODEL), inv_freq, w_in16, w_kt16, w_in16,
      q_gain.reshape(1, HEAD_DIM), k_gain.reshape(HEAD_DIM, 1), *[job[0] for job in cast_jobs])
    return (q, kt, v), casted


def _attn_kernel(*refs):
    qkv_refs = refs[:3 * N_B_GROUPS]
    y_ref = refs[3 * N_B_GROUPS]
    scratch = refs[3 * N_B_GROUPS + 1:]
    kt_scrs, v_scrs = scratch[0:N_B_GROUPS], scratch[N_B_GROUPS:2 * N_B_GROUPS]
    o_scr, lse_scr, bias_scr = scratch[2 * N_B_GROUPS:]
    n = pl.program_id(2)
    first_step = (pl.program_id(0) == 0) & (pl.program_id(1) == 0) & (n == 0)

    @pl.when(first_step)
    def _():
        qi = lax.broadcasted_iota(jnp.int32, (SPAN, 2 * SPAN), 0)
        ki = lax.broadcasted_iota(jnp.int32, (SPAN, 2 * SPAN), 1)
        band = (ki >= qi) & (ki <= qi + SPAN)
        bias_scr[0] = jnp.where(band, 0.0, MASK_VALUE)
        bias_scr[1] = jnp.where(band & (ki >= SPAN), 0.0, MASK_VALUE)

    first_block_table = jnp.where(n > 0, 0, 1)

    for g, (_, d) in enumerate(B_PATTERNS):
        q_ref, kt_ref, v_ref = qkv_refs[3 * g:3 * g + 3]
        kt_scr, v_scr = kt_scrs[g], v_scrs[g]
        per = SUPER // d
        seg = SPAN + per

        @pl.when(first_step)
        def _():
            v_scr[...] = jnp.ones(v_scr.shape, BF16)

        @pl.when(n == 0)
        def _():
            for r in range(d):
                kt_scr[:, r * seg:r * seg + SPAN] = jnp.zeros((kt_scr.shape[0], SPAN), BF16)
                for hh in range(HEADS_PER_STEP):
                    v_scr[r * seg:r * seg + SPAN, 2 * hh * HEAD_DIM:(2 * hh + 1) * HEAD_DIM] = (
                        jnp.zeros((SPAN, HEAD_DIM), BF16))

        @pl.when(n > 0)
        def _():
            for r in range(d):
                kt_scr[:, r * seg:r * seg + SPAN] = kt_scr[:, r * seg + per:r * seg + per + SPAN]
                v_scr[r * seg:r * seg + SPAN, :] = v_scr[r * seg + per:r * seg + per + SPAN, :]

        for r in range(d):
            kt_scr[:, r * seg + SPAN:(r + 1) * seg] = kt_ref[:, r * per:(r + 1) * per]
            for hh in range(HEADS_PER_STEP):
                v_scr[r * seg + SPAN:(r + 1) * seg, 2 * hh * HEAD_DIM:(2 * hh + 1) * HEAD_DIM] = (
                    v_ref[r * per:(r + 1) * per, hh * HEAD_DIM:(hh + 1) * HEAD_DIM])

        for r in range(d):
            for j in range(per // SPAN):
                q_rows = slice(r * per + j * SPAN, r * per + (j + 1) * SPAN)
                keys = slice(r * seg + j * SPAN, r * seg + (j + 2) * SPAN)
                if d < MERGE_DIL:
                    assert d == 1
                    nat = slice(j * SPAN, (j + 1) * SPAN)
                else:
                    fine = d // MERGE_DIL
                    start = (r % MERGE_DIL) * (SUPER // MERGE_DIL) + fine * j * SPAN + r // MERGE_DIL
                    nat = pl.ds(start, SPAN, stride=fine) if fine > 1 else slice(start, start + SPAN)
                for hh in range(HEADS_PER_STEP):
                    cols = slice(hh * HEAD_DIM, (hh + 1) * HEAD_DIM)
                    bias = bias_scr[first_block_table] if j == 0 else bias_scr[0]
                    s = _dot(q_ref[q_rows, cols], kt_scr[cols, keys]) + bias
                    m = jnp.max(s, axis=-1, keepdims=True)
                    p = jnp.exp2(s - m)
                    pv = _dot(p.astype(BF16), v_scr[keys, 2 * hh * HEAD_DIM:(2 * hh + 2) * HEAD_DIM])
                    denom = pv[:, HEAD_DIM:]
                    o_scr[g, hh, nat, :] = pv[:, :HEAD_DIM] * (1.0 / denom)
                    lse_scr[g, hh, nat, :] = m + jnp.log2(denom)

    rows = MERGE_ROWS
    for hh in range(HEADS_PER_STEP):
        for b in range(MERGE_DIL):
            for u0 in range(0, SUPER // MERGE_DIL, rows):
                natural = pl.ds(b + MERGE_DIL * u0, rows, stride=MERGE_DIL)
                ordered = slice(b * (SUPER // MERGE_DIL) + u0, b * (SUPER // MERGE_DIL) + u0 + rows)
                idx = [natural if d < MERGE_DIL else ordered for _, d in B_PATTERNS]
                lses = [lse_scr[g, hh, idx[g], :] for g in range(N_B_GROUPS)]
                m = jnp.maximum(jnp.maximum(lses[0], lses[1]), lses[2])
                es = [jnp.exp2(l - m) for l in lses]
                mixed = (es[0] * o_scr[0, hh, idx[0], :] + es[1] * o_scr[1, hh, idx[1], :]
                         + es[2] * o_scr[2, hh, idx[2], :])
                y_ref[hh, natural, :] = mixed * (1.0 / (es[0] + es[1] + es[2]))


def _attn(qkvs, bsz, seq, cast_jobs=()):
    width = HEADS_PER_STEP * HEAD_DIM
    n_super = seq // SUPER
    in_specs, operands, kt_scr, v_scr = [], [], [], []
    for (q, kt, v), (_, d) in zip(qkvs, B_PATTERNS):
        in_specs += [
            pl.BlockSpec((None, SUPER, width), lambda b, hp, n: (b, n, hp)),
            pl.BlockSpec((None, width, SUPER), lambda b, hp, n: (b, hp, n)),
            pl.BlockSpec((None, SUPER, width), lambda b, hp, n: (b, n, hp)),
        ]
        operands += [q, kt, v]
        kt_scr.append(pltpu.VMEM((width, SUPER + d * SPAN), BF16))
        v_scr.append(pltpu.VMEM((SUPER + d * SPAN, 2 * width), BF16))
    n_pairs = B_HEADS // HEADS_PER_STEP
    cast_in, cast_out, cast_shapes = _cast_specs(
        cast_jobs, bsz * n_pairs * n_super, lambda b, hp, n: (b * n_pairs + hp) * n_super + n)
    out, *casted = pl.pallas_call(
        _with_casts(_attn_kernel, len(operands), 1, len(cast_jobs)),
        grid=(bsz, n_pairs, n_super),
        in_specs=in_specs + cast_in,
        out_specs=[pl.BlockSpec((None, HEADS_PER_STEP, SUPER, HEAD_DIM),
                                lambda b, hp, n: (b, hp, n, 0))] + cast_out,
        out_shape=[jax.ShapeDtypeStruct((bsz, B_HEADS, seq, HEAD_DIM), F32)] + cast_shapes,
        scratch_shapes=kt_scr + v_scr + [
            pltpu.VMEM((N_B_GROUPS, HEADS_PER_STEP, SUPER, HEAD_DIM), F32),
            pltpu.VMEM((N_B_GROUPS, HEADS_PER_STEP, SUPER, HEAD_DIM), F32),
            pltpu.VMEM((2, SPAN, 2 * SPAN), F32),
        ],
        compiler_params=pltpu.CompilerParams(
            dimension_semantics=("arbitrary", "arbitrary", "arbitrary"),
            vmem_limit_bytes=VMEM_LIMIT_BYTES),
        name="attn",
    )(*operands, *[job[0] for job in cast_jobs])
    return out, casted


def _attn_out_kernel(x_ref, ng_ref, wz_ref, o_ref, w_out_ref, out_ref):
    x = x_ref[...]
    h = _rms_norm(x, ng_ref[...]).astype(BF16)
    z = _dot(h, wz_ref[...])
    o = jnp.concatenate([o_ref[hd] for hd in range(B_HEADS)], axis=1)
    y = (o * _silu(z)).astype(BF16)
    out_ref[...] = x + _dot(y, w_out_ref[...])


def _attn_out(x3, ng, w_in16, o, w_out16):
    bsz, seq, _ = x3.shape
    tm = LAYER_TILE
    tile = pl.BlockSpec((None, tm, D_MODEL), lambda b, t: (b, t, 0))
    return pl.pallas_call(
        _attn_out_kernel,
        grid=(bsz, seq // tm),
        in_specs=[
            tile,
            _resident((1, D_MODEL), lambda b, t: (0, 0)),
            _resident((D_MODEL, B_WIDTH), lambda b, t: (0, 3 * N_B_GROUPS)),
            pl.BlockSpec((None, B_HEADS, tm, HEAD_DIM), lambda b, t: (b, 0, t, 0)),
            _resident((B_WIDTH, D_MODEL), lambda b, t: (0, 0)),
        ],
        out_specs=tile,
        out_shape=jax.ShapeDtypeStruct((bsz, seq, D_MODEL), F32),
        compiler_params=pltpu.CompilerParams(
            dimension_semantics=("arbitrary", "arbitrary"), vmem_limit_bytes=VMEM_LIMIT_BYTES),
        name="attn_out",
    )(x3, ng.reshape(1, D_MODEL), w_in16, o, w_out16)


def _layer_b(x3, ng, w_in16, w_kt16, q_gain, k_gain, w_out16, proj_cast_jobs, attn_cast_jobs):
    bsz, seq, _ = x3.shape
    inv_freq = jnp.power(jnp.float32(ROPE_THETA), -jnp.arange(ROPE_HALF, dtype=F32) / ROPE_HALF)
    inv_freq = inv_freq.reshape(ROPE_HALF, 1)
    qkvs, casted = [], []
    for g, (window, dilation) in enumerate(B_PATTERNS):
        assert window // dilation == SPAN
        qkv, done = _attn_proj(x3, ng, inv_freq, w_in16, w_kt16, q_gain[g], k_gain[g], g, dilation,
                               proj_cast_jobs[g])
        qkvs.append(qkv)
        casted += done
    o, done = _attn(qkvs, bsz, seq, attn_cast_jobs)
    return _attn_out(x3, ng, w_in16, o, w_out16), casted + done


def kernel(x, norm_gain, a_w_in, a_v_gain, a_w_s, a_b_s, a_w_out, b_w_in, b_q_gain, b_k_gain,
           b_w_out, c_w_in, c_w_grp, c_scale, c_w_out):
    bsz, seq, d_model = x.shape
    assert d_model == D_MODEL and seq % SUPER == 0 and SUPER % TOKEN_TILE == 0
    assert norm_gain.shape[0] == 4 and a_w_in.shape[0] == 2
    c_w_grp2 = c_w_grp.reshape(c_w_grp.shape[0], N_POOL * C_GROUP, C_GROUP)
    k_cols = (N_B_GROUPS * B_WIDTH, N_B_GROUPS * B_WIDTH)

    x2, (b_w_in16, b_w_out16, b_w_kt16) = _layer_a(
        x.reshape(bsz * seq, D_MODEL), norm_gain[0], a_w_in[0].astype(BF16), a_v_gain[0], a_w_s[0],
        a_b_s[0], a_w_out[0].astype(BF16),
        cast_jobs=[(b_w_in, 0, None), (b_w_out, 0, None), (b_w_in, 0, k_cols)])
    x, (c_w_in16, c_w_grp16, c_w_out16, a_w_in16, a_w_out16) = _layer_b(
        x2.reshape(bsz, seq, D_MODEL), norm_gain[1], b_w_in16, b_w_kt16, b_q_gain[0], b_k_gain[0],
        b_w_out16,
        proj_cast_jobs=[[(c_w_in, 0, None)], [(c_w_grp2, 0, None)], [(c_w_out, 0, None)]],
        attn_cast_jobs=[(a_w_in, 1, None), (a_w_out, 1, None)])
    x = _layer_c(x, norm_gain[2], c_w_in16, c_w_grp16.reshape(N_POOL, C_GROUP, C_GROUP), c_scale[0],
                 c_w_out16)
    x2, _ = _layer_a(x.reshape(bsz * seq, D_MODEL), norm_gain[3], a_w_in16, a_v_gain[1], a_w_s[1],
                     a_b_s[1], a_w_out16)
    return x2.reshape(bsz, seq, D_MODEL)
```
